```python
import math
import jax, jax.numpy as jnp
from jax import lax
import numpy as np

D_MODEL = 1024
BATCH = 8
SEQ = 8192
DEPTH = 1
DEC_BATCH = 4
DEC_SEQ = 4096
PAST_LEN = 128

N_META = 16
GRID_W = 64
HEAD_DIM = 64
NA_HEADS = 8
NA_WIDTH = NA_HEADS * HEAD_DIM
NA_MAX_KH = 8
NA_KW = 16
DIFF_HEADS = 4
DIFF_WIDTH = DIFF_HEADS * 2 * HEAD_DIM
MIX_WIDTH = NA_WIDTH + DIFF_WIDTH
PROJ_WIDTH = 3 * NA_WIDTH + 3 * DIFF_WIDTH
Q_BLOCK = 128
ROPE_THETA = 10000.0
N_EXPERTS = 16
EC_CAPACITY_FACTOR = 2
D_FF_EXPERT = 2048
RMS_EPS = 1e-6
SUBLN_EPS = 1e-5

kernel_name = 'hymba_na_diffattn_ec_encoder'


def _rmsnorm(x, g, eps=RMS_EPS):
    x32 = x.astype(jnp.float32)
    y = x32 * lax.rsqrt(jnp.mean(x32 * x32, axis=-1, keepdims=True) + eps)
    return (y * g.astype(jnp.float32)).astype(x.dtype)


def _rotary(t, cos, sin):
    half = t.shape[-1] // 2
    t32 = t.astype(jnp.float32)
    t1, t2 = t32[..., :half], t32[..., half:]
    c = cos[None, :, None, None, :]
    s = sin[None, :, None, None, :]
    return jnp.concatenate([t1 * c - t2 * s, t2 * c + t1 * s], axis=-1).astype(t.dtype)


def _neighbourhood_attention(q, k, v, rpb, meta_bias):
    B, L, H, dh = q.shape
    n_real = L - N_META
    rows = n_real // GRID_W
    kh = min(NA_MAX_KH, rows)
    scale = dh ** -0.5
    qm, km, vm = q[:, :N_META], k[:, :N_META], v[:, :N_META]
    qg = q[:, N_META:].reshape(B, rows, GRID_W, H, dh)
    kg = k[:, N_META:].reshape(B, rows, GRID_W, H, dh)
    vg = v[:, N_META:].reshape(B, rows, GRID_W, H, dh)
    r_idx = jnp.arange(rows)
    row_start = jnp.clip(r_idx - kh // 2, 0, rows - kh)
    c_idx = jnp.arange(GRID_W)
    col_idx = jnp.clip(c_idx - NA_KW // 2, 0, GRID_W - NA_KW)[:, None] + jnp.arange(NA_KW)
    col_off = col_idx - c_idx[:, None] + NA_KW - 1
    mb = meta_bias.astype(jnp.float32)

    s_mm = jnp.einsum('bqhd,bkhd->bhqk', qm, km).astype(jnp.float32) * scale + mb[None, :, None, :]
    out_meta = jnp.einsum('bhqk,bkhd->bqhd', jax.nn.softmax(s_mm, axis=-1).astype(v.dtype), vm)

    def row_fn(args):
        qr, r, rs = args
        kr = lax.dynamic_slice_in_dim(kg, rs, kh, axis=1)[:, :, col_idx]
        vr = lax.dynamic_slice_in_dim(vg, rs, kh, axis=1)[:, :, col_idx]
        row_off = rs + jnp.arange(kh) - r + NA_MAX_KH - 1
        bias = rpb[:, row_off[:, None, None], col_off[None]]
        bias = bias.transpose(0, 2, 1, 3).reshape(H, GRID_W, kh * NA_KW).astype(jnp.float32)
        s_loc = jnp.einsum('bqhd,bkqjhd->bhqkj', qr, kr).reshape(B, H, GRID_W, kh * NA_KW)
        s_loc = s_loc.astype(jnp.float32) * scale + bias[None]
        s_met = jnp.einsum('bqhd,bmhd->bhqm', qr, km).astype(jnp.float32) * scale + mb[None, :, None, :]
        p = jax.nn.softmax(jnp.concatenate([s_loc, s_met], axis=-1), axis=-1).astype(v.dtype)
        p_loc = p[..., :kh * NA_KW].reshape(B, H, GRID_W, kh, NA_KW)
        p_met = p[..., kh * NA_KW:]
        return (jnp.einsum('bhqkj,bkqjhd->bqhd', p_loc, vr)
                + jnp.einsum('bhqm,bmhd->bqhd', p_met, vm))

    out = lax.map(row_fn, (qg.transpose(1, 0, 2, 3, 4), r_idx, row_start))
    out = out.transpose(1, 0, 2, 3, 4).reshape(B, n_real, H, dh)
    return jnp.concatenate([out_meta, out], axis=1).reshape(B, L, H * dh)


def _diff_attention(q, k, v, lam, lambda_init, subln):
    B, L, H, _, dh = q.shape
    scale = dh ** -0.5

    def attend(qb):
        s = jnp.einsum('bqhcd,bkhcd->bhcqk', qb, k).astype(jnp.float32) * scale
        p = jax.nn.softmax(s, axis=-1)
        a = (p[:, :, 0] - lam * p[:, :, 1]).astype(v.dtype)
        return jnp.einsum('bhqk,bkhe->bqhe', a, v)

    n_real = L - N_META
    nblk = n_real // Q_BLOCK
    out_meta = attend(q[:, :N_META])
    qb = q[:, N_META:].reshape(B, nblk, Q_BLOCK, H, 2, dh).transpose(1, 0, 2, 3, 4, 5)
    out_real = lax.map(attend, qb).transpose(1, 0, 2, 3, 4).reshape(B, n_real, H, 2 * dh)
    o = jnp.concatenate([out_meta, out_real], axis=1)
    o = _rmsnorm(o, subln, SUBLN_EPS) * (1.0 - lambda_init)
    return o.reshape(B, L, H * 2 * dh)


def _expert_choice_ffn(h, w_router, w_gate, w_up, w_down):
    B, L, D = h.shape
    n_tok = B * L
    hf = h.reshape(n_tok, D)
    affinity = jax.nn.softmax((hf @ w_router).astype(jnp.float32), axis=-1)
    cap = EC_CAPACITY_FACTOR * n_tok // N_EXPERTS
    gate, idx = lax.top_k(affinity.T, cap)
    xe = hf[idx]
    a = jnp.einsum('ecd,edf->ecf', xe, w_gate)
    b = jnp.einsum('ecd,edf->ecf', xe, w_up)
    ye = jnp.einsum('ecf,efd->ecd', jax.nn.silu(a) * b, w_down) * gate[..., None].astype(h.dtype)
    out = jnp.zeros_like(hf).at[idx.reshape(-1)].add(ye.reshape(-1, D))
    return out.reshape(B, L, D)


def _layer(x, layer_idx, cos, sin, norm_mix, w_in, na_rpb, na_meta_bias, lambda_q1, lambda_k1,
           lambda_q2, lambda_k2, diff_subln, w_out, norm_ffn, w_router, w_gate, w_up, w_down):
    B, L, D = x.shape
    h = _rmsnorm(x, norm_mix)
    proj = h @ w_in
    splits = [NA_WIDTH, 2 * NA_WIDTH, 3 * NA_WIDTH, 3 * NA_WIDTH + DIFF_WIDTH, 3 * NA_WIDTH + 2 * DIFF_WIDTH]
    na_q, na_k, na_v, d_q, d_k, d_v = jnp.split(proj, splits, axis=-1)

    na_out = _neighbourhood_attention(na_q.reshape(B, L, NA_HEADS, HEAD_DIM),
                                      na_k.reshape(B, L, NA_HEADS, HEAD_DIM),
                                      na_v.reshape(B, L, NA_HEADS, HEAD_DIM),
                                      na_rpb, na_meta_bias)

    lambda_init = 0.8 - 0.6 * math.exp(-0.3 * layer_idx)
    lam = (jnp.exp(jnp.sum(lambda_q1.astype(jnp.float32) * lambda_k1.astype(jnp.float32)))
           - jnp.exp(jnp.sum(lambda_q2.astype(jnp.float32) * lambda_k2.astype(jnp.float32)))
           + lambda_init)
    dq = _rotary(d_q.reshape(B, L, DIFF_HEADS, 2, HEAD_DIM), cos, sin)
    dk = _rotary(d_k.reshape(B, L, DIFF_HEADS, 2, HEAD_DIM), cos, sin)
    dv = d_v.reshape(B, L, DIFF_HEADS, 2 * HEAD_DIM)
    diff_out = _diff_attention(dq, dk, dv, lam, lambda_init, diff_subln)

    x = x + jnp.concatenate([na_out, diff_out], axis=-1) @ w_out
    x = x + _expert_choice_ffn(_rmsnorm(x, norm_ffn), w_router, w_gate, w_up, w_down)
    return x


def _encode(x, meta_tokens, norm_mix, w_in, na_rpb, na_meta_bias, lambda_q1, lambda_k1, lambda_q2,
            lambda_k2, diff_subln, w_out, norm_ffn, w_router, w_gate, w_up, w_down, final_norm):
    B, T, D = x.shape
    L = N_META + T
    meta = jnp.broadcast_to(meta_tokens.astype(x.dtype)[None], (B, N_META, D))
    h = jnp.concatenate([meta, x], axis=1)
    pos = jnp.arange(L, dtype=jnp.float32)
    inv_freq = 1.0 / (ROPE_THETA ** (jnp.arange(0, HEAD_DIM, 2, dtype=jnp.float32) / HEAD_DIM))
    ang = pos[:, None] * inv_freq[None, :]
    cos, sin = jnp.cos(ang), jnp.sin(ang)
    for l in range(DEPTH):
        h = _layer(h, l, cos, sin, norm_mix[l], w_in[l], na_rpb[l], na_meta_bias[l], lambda_q1[l],
                   lambda_k1[l], lambda_q2[l], lambda_k2[l], diff_subln[l], w_out[l], norm_ffn[l],
                   w_router[l], w_gate[l], w_up[l], w_down[l])
    return _rmsnorm(h, final_norm)[:, N_META:]


def setup_inputs(seed: int = 0) -> dict:
    key = jax.random.key(seed)
    ks = jax.random.split(key, 20)
    f32 = jnp.float32
    nrm = lambda k, shape, s: jax.random.normal(k, shape, f32) * s
    return {
        'x_prompt': nrm(ks[0], (BATCH, SEQ, D_MODEL), 1.0),
        'x_sample': nrm(ks[1], (DEC_BATCH, DEC_SEQ, D_MODEL), 1.0),
        'meta_tokens': nrm(ks[2], (N_META, D_MODEL), 1.0),
        'norm_mix': 1.0 + nrm(ks[3], (DEPTH, D_MODEL), 0.01),
        'w_in': nrm(ks[4], (DEPTH, D_MODEL, PROJ_WIDTH), D_MODEL ** -0.5),
        'na_rpb': nrm(ks[5], (DEPTH, NA_HEADS, 2 * NA_MAX_KH - 1, 2 * NA_KW - 1), 0.02),
        'na_meta_bias': nrm(ks[6], (DEPTH, NA_HEADS, N_META), 0.02),
        'lambda_q1': nrm(ks[7], (DEPTH, HEAD_DIM), 0.1),
        'lambda_k1': nrm(ks[8], (DEPTH, HEAD_DIM), 0.1),
        'lambda_q2': nrm(ks[9], (DEPTH, HEAD_DIM), 0.1),
        'lambda_k2': nrm(ks[10], (DEPTH, HEAD_DIM), 0.1),
        'diff_subln': 1.0 + nrm(ks[11], (DEPTH, 2 * HEAD_DIM), 0.01),
        'w_out': nrm(ks[12], (DEPTH, MIX_WIDTH, D_MODEL), MIX_WIDTH ** -0.5),
        'norm_ffn': 1.0 + nrm(ks[13], (DEPTH, D_MODEL), 0.01),
        'w_router': nrm(ks[14], (DEPTH, D_MODEL, N_EXPERTS), D_MODEL ** -0.5),
        'w_gate': nrm(ks[15], (DEPTH, N_EXPERTS, D_MODEL, D_FF_EXPERT), D_MODEL ** -0.5),
        'w_up': nrm(ks[16], (DEPTH, N_EXPERTS, D_MODEL, D_FF_EXPERT), D_MODEL ** -0.5),
        'w_down': nrm(ks[17], (DEPTH, N_EXPERTS, D_FF_EXPERT, D_MODEL), D_FF_EXPERT ** -0.5),
        'final_norm': 1.0 + nrm(ks[18], (D_MODEL,), 0.01),
    }


def reference(x_prompt, x_sample, meta_tokens, norm_mix, w_in, na_rpb, na_meta_bias, lambda_q1,
              lambda_k1, lambda_q2, lambda_k2, diff_subln, w_out, norm_ffn, w_router, w_gate, w_up,
              w_down, final_norm):
    y_prompt = _encode(x_prompt, meta_tokens, norm_mix, w_in, na_rpb, na_meta_bias, lambda_q1,
                       lambda_k1, lambda_q2, lambda_k2, diff_subln, w_out, norm_ffn, w_router,
                       w_gate, w_up, w_down, final_norm)
    y_sample = _encode(x_sample, meta_tokens, norm_mix, w_in, na_rpb, na_meta_bias, lambda_q1,
                       lambda_k1, lambda_q2, lambda_k2, diff_subln, w_out, norm_ffn, w_router,
                       w_gate, w_up, w_down, final_norm)
    return (y_prompt, y_sample)
```

```python
import functools
import math

import jax
import jax.numpy as jnp
from jax import lax
from jax.experimental import pallas as pl
from jax.experimental.pallas import tpu as pltpu

BF16 = jnp.bfloat16
F32 = jnp.float32
I32 = jnp.int32

N_META = 16
GRID_W = 64
HEAD_DIM = 64
NA_HEADS = 8
NA_WIDTH = NA_HEADS * HEAD_DIM
NA_KH = 8
NA_KW = 16
DIFF_HEADS = 4
DIFF_WIDTH = DIFF_HEADS * 2 * HEAD_DIM
PROJ_WIDTH = 3 * NA_WIDTH + 3 * DIFF_WIDTH
ROPE_THETA = 10000.0
N_EXPERTS = 16
EC_CAPACITY_FACTOR = 2
RMS_EPS = 1e-6
SUBLN_EPS = 1e-5
LAMBDA_INIT = 0.8 - 0.6 * math.exp(-0.3 * 0)

LANES = 128
NEG_BIG = -1e30
VMEM_LIMIT = 56 * 1024 * 1024

NAQ_BLK, NAK_BLK, NAV_BLK = 0, 4, 8
DQ_BLK, DK_BLK, DV_BLK = 12, 16, 20

ROW_WIN = 144
TOK_BLK = 512


def _cparams(sem, vmem=VMEM_LIMIT):
    return pltpu.CompilerParams(dimension_semantics=sem, vmem_limit_bytes=vmem)


def _dot(a, b):
    return jnp.dot(a, b, preferred_element_type=F32)


def _dot_nt(a, b):
    return lax.dot_general(a, b, (((1,), (1,)), ((), ())), preferred_element_type=F32)


def _inproj_body(x_ref, g_ref, w_ref, cos_ref, sin_ref, o_ref, *, tn):
    x = x_ref[...]
    ms = jnp.mean(x * x, axis=-1, keepdims=True)
    h = (x * lax.rsqrt(ms + RMS_EPS) * g_ref[...]).astype(BF16)
    cos = cos_ref[...]
    sin = sin_ref[...]
    for c in range(PROJ_WIDTH // tn):
        lo = c * tn
        acc = _dot(h, w_ref[:, lo:lo + tn])
        for s in range(tn // LANES):
            blk = (lo + s * LANES) // LANES
            a = acc[:, s * LANES:(s + 1) * LANES]
            if DQ_BLK <= blk < DV_BLK:
                a = a * cos + pltpu.roll(a, 64, 1) * sin
            if blk < NAK_BLK or DQ_BLK <= blk < DK_BLK:
                a = a * (HEAD_DIM ** -0.5)
            o_ref[:, lo + s * LANES:lo + (s + 1) * LANES] = a.astype(BF16)


def _inproj(x, g, w, cos, sin, *, tm):
    rows, d = x.shape
    nt = cos.shape[0] // tm
    return pl.pallas_call(
        functools.partial(_inproj_body, tn=512),
        grid=(rows // tm,),
        in_specs=[
            pl.BlockSpec((tm, d), lambda i: (i, 0)),
            pl.BlockSpec((1, d), lambda i: (0, 0)),
            pl.BlockSpec((d, PROJ_WIDTH), lambda i: (0, 0)),
            pl.BlockSpec((tm, LANES), lambda i: (i % nt, 0)),
            pl.BlockSpec((tm, LANES), lambda i: (i % nt, 0)),
        ],
        out_specs=pl.BlockSpec((tm, PROJ_WIDTH), lambda i: (i, 0)),
        out_shape=jax.ShapeDtypeStruct((rows, PROJ_WIDTH), BF16),
        compiler_params=_cparams(("parallel",)),
        name="inproj",
    )(x, g, w, cos, sin)


def _na_bias_body(rpb_ref, o_ref):
    h = pl.program_id(0)
    c = lax.broadcasted_iota(I32, (GRID_W, GRID_W), 0)
    kc = lax.broadcasted_iota(I32, (GRID_W, GRID_W), 1)
    cs = jnp.clip(c - NA_KW // 2, 0, GRID_W - NA_KW)
    valid = (kc >= cs) & (kc < cs + NA_KW)
    jm = kc - c + NA_KW - 1
    n_dr = 2 * NA_KH - 1
    n_j = 2 * NA_KW - 1
    tiles = []
    for dr in range(n_dr):
        base = (h * n_dr + dr) * n_j

        def body(j, acc, base=base):
            return jnp.where(jm == j, rpb_ref[base + j], acc)

        t = lax.fori_loop(0, n_j, body, jnp.zeros((GRID_W, GRID_W), F32))
        tiles.append(jnp.where(valid, t, NEG_BIG))
    for d in range(NA_KH):
        o_ref[0, d] = jnp.concatenate([tiles[kr - d + NA_KH - 1] for kr in range(NA_KH)], axis=1)


def _na_bias(rpb):
    h = rpb.shape[0]
    return pl.pallas_call(
        _na_bias_body,
        grid_spec=pltpu.PrefetchScalarGridSpec(
            num_scalar_prefetch=1,
            grid=(h,),
            in_specs=[],
            out_specs=pl.BlockSpec((1, NA_KH, GRID_W, NA_KH * GRID_W), lambda i, r: (i, 0, 0, 0)),
        ),
        out_shape=jax.ShapeDtypeStruct((h, NA_KH, GRID_W, NA_KH * GRID_W), F32),
        compiler_params=_cparams(("arbitrary",)),
        name="na_bias",
    )(rpb.reshape(-1))


def _split_heads(q):
    lane = lax.broadcasted_iota(I32, q.shape, 1)
    zero = jnp.zeros_like(q)
    return jnp.concatenate([jnp.where(lane < HEAD_DIM, q, zero), jnp.where(lane < HEAD_DIM, zero, q)], axis=0)


def _merge_heads(o):
    n = o.shape[0] // 2
    lane = lax.broadcasted_iota(I32, (n, LANES), 1)
    return jnp.where(lane < HEAD_DIM, o[:n], o[n:])


def _na_body(q_ref, kp_ref, kc_ref, kn_ref, vp_ref, vc_ref, vn_ref, km_ref, vm_ref, tab_ref, mb_ref,
             o_ref, kbuf, vbuf, *, rows):
    rb = pl.program_id(2)
    blk = NA_KH * GRID_W
    kbuf[0:blk] = kp_ref[0]
    kbuf[blk:2 * blk] = kc_ref[0]
    kbuf[2 * blk:3 * blk] = kn_ref[0]
    vbuf[0:blk] = vp_ref[0]
    vbuf[blk:2 * blk] = vc_ref[0]
    vbuf[2 * blk:3 * blk] = vn_ref[0]
    km = km_ref[...]
    vm = vm_ref[...]
    mb = mb_ref[0]
    for i in range(NA_KH):
        r = rb * NA_KH + i
        rs = jnp.clip(r - NA_KH // 2, 0, rows - NA_KH)
        d = r - rs
        start = pl.multiple_of((rs - (rb - 1) * NA_KH) * GRID_W, GRID_W)
        qq = _split_heads(q_ref[0, i * GRID_W:(i + 1) * GRID_W, :])
        kwin = kbuf[pl.ds(start, blk), :]
        vwin = vbuf[pl.ds(start, blk), :]
        bias = jnp.concatenate([tab_ref[0, d], tab_ref[1, d]], axis=0)
        s = _dot_nt(qq, kwin) + bias
        sm = _dot_nt(qq, km) + mb
        m = jnp.maximum(jnp.max(s, axis=1, keepdims=True), jnp.max(sm, axis=1, keepdims=True))
        p = jnp.exp(s - m)
        pm = jnp.exp(sm - m)
        l = jnp.sum(p, axis=1, keepdims=True) + jnp.sum(pm, axis=1, keepdims=True)
        o = (_dot(p.astype(BF16), vwin) + _dot(pm.astype(BF16), vm)) / l
        o_ref[0, i * GRID_W:(i + 1) * GRID_W, :] = _merge_heads(o).astype(BF16)


def _na_attention(proj, proj_meta, table, mb_stack):
    b, t, _ = proj.shape
    rows = t // GRID_W
    nrb = rows // NA_KH
    blk = NA_KH * GRID_W
    hp = NA_HEADS // 2

    def kv_spec(col, shift):
        return pl.BlockSpec((1, blk, LANES),
                            lambda h, bi, rb: (bi, jnp.clip(rb + shift, 0, nrb - 1), col + h))

    return pl.pallas_call(
        functools.partial(_na_body, rows=rows),
        grid=(hp, b, nrb),
        in_specs=[
            pl.BlockSpec((1, blk, LANES), lambda h, bi, rb: (bi, rb, NAQ_BLK + h)),
            kv_spec(NAK_BLK, -1), kv_spec(NAK_BLK, 0), kv_spec(NAK_BLK, 1),
            kv_spec(NAV_BLK, -1), kv_spec(NAV_BLK, 0), kv_spec(NAV_BLK, 1),
            pl.BlockSpec((N_META, LANES), lambda h, bi, rb: (0, NAK_BLK + h)),
            pl.BlockSpec((N_META, LANES), lambda h, bi, rb: (0, NAV_BLK + h)),
            pl.BlockSpec((2, NA_KH, GRID_W, blk), lambda h, bi, rb: (h, 0, 0, 0)),
            pl.BlockSpec((1, LANES, N_META), lambda h, bi, rb: (h, 0, 0)),
        ],
        out_specs=pl.BlockSpec((1, blk, LANES), lambda h, bi, rb: (bi, rb, h)),
        out_shape=jax.ShapeDtypeStruct((b, t, NA_WIDTH), BF16),
        scratch_shapes=[pltpu.VMEM((3 * blk, LANES), BF16), pltpu.VMEM((3 * blk, LANES), BF16)],
        compiler_params=_cparams(("arbitrary", "arbitrary", "arbitrary")),
        name="na_attention",
    )(proj, proj, proj, proj, proj, proj, proj, proj_meta, proj_meta, table, mb_stack)


def _na_meta_body(p_ref, mb_ref, o_ref):
    for h in range(NA_HEADS // 2):
        qq = _split_heads(p_ref[:, (NAQ_BLK + h) * LANES:(NAQ_BLK + h + 1) * LANES])
        km = p_ref[:, (NAK_BLK + h) * LANES:(NAK_BLK + h + 1) * LANES]
        vm = p_ref[:, (NAV_BLK + h) * LANES:(NAV_BLK + h + 1) * LANES]
        s = _dot_nt(qq, km) + mb_ref[h, 0:2 * N_META, :]
        m = jnp.max(s, axis=1, keepdims=True)
        p = jnp.exp(s - m)
        l = jnp.sum(p, axis=1, keepdims=True)
        o = _dot(p.astype(BF16), vm) / l
        o_ref[:, h * LANES:(h + 1) * LANES] = _merge_heads(o).astype(BF16)


def _na_meta(proj_meta, mb_meta):
    return pl.pallas_call(
        _na_meta_body,
        out_shape=jax.ShapeDtypeStruct((N_META, NA_WIDTH), BF16),
        name="na_meta",
    )(proj_meta, mb_meta)


def _split_maps(q):
    lane = lax.broadcasted_iota(I32, q.shape, 1)
    is0 = ((lane >> 5) & 1) == 0
    zero = jnp.zeros_like(q)
    return jnp.concatenate([jnp.where(is0, q, zero), jnp.where(is0, zero, q)], axis=0)


def _diff_body(q_ref, k_ref, v_ref, km_ref, vm_ref, lq1_ref, lk1_ref, lq2_ref, lk2_ref, sub_ref,
               o_ref, qq_scr, m_scr, l_scr, acc_scr, *, tq):
    ki = pl.program_id(3)

    @pl.when(ki == 0)
    def _():
        qq = _split_maps(q_ref[0])
        qq_scr[...] = qq
        s = _dot_nt(qq, km_ref[...])
        m = jnp.max(s, axis=1, keepdims=True)
        p = jnp.exp(s - m)
        m_scr[...] = m
        l_scr[...] = jnp.sum(p, axis=1, keepdims=True)
        acc_scr[...] = _dot(p.astype(BF16), vm_ref[...])

    s = _dot_nt(qq_scr[...], k_ref[0])
    m_prev = m_scr[...]
    m_new = jnp.maximum(m_prev, jnp.max(s, axis=1, keepdims=True))
    alpha = jnp.exp(m_prev - m_new)
    p = jnp.exp(s - m_new)
    l_scr[...] = alpha * l_scr[...] + jnp.sum(p, axis=1, keepdims=True)
    acc_scr[...] = alpha * acc_scr[...] + _dot(p.astype(BF16), v_ref[0])
    m_scr[...] = m_new

    @pl.when(ki == pl.num_programs(3) - 1)
    def _():
        lam = (jnp.exp(jnp.sum(lq1_ref[...] * lk1_ref[...], axis=1, keepdims=True))
               - jnp.exp(jnp.sum(lq2_ref[...] * lk2_ref[...], axis=1, keepdims=True)) + LAMBDA_INIT)
        o_all = acc_scr[...] / l_scr[...]
        o = o_all[:tq] - lam * o_all[tq:]
        ms = jnp.mean(o * o, axis=-1, keepdims=True)
        o = o * lax.rsqrt(ms + SUBLN_EPS) * sub_ref[...] * (1.0 - LAMBDA_INIT)
        o_ref[0] = o.astype(BF16)


def _diff_attention(q_src, proj, proj_meta, lq1, lk1, lq2, lk2, subln, *, tq, tk, q_shared):
    b, t, _ = proj.shape
    tq_total = q_src.shape[1]
    nq = tq_total // tq
    nk = t // tk
    if q_shared:
        q_map = lambda bi, h, qi, ki: (0, qi, DQ_BLK + h)
    else:
        q_map = lambda bi, h, qi, ki: (bi, qi, DQ_BLK + h)
    vec = lambda n: pl.BlockSpec((1, n), lambda bi, h, qi, ki: (0, 0))
    return pl.pallas_call(
        functools.partial(_diff_body, tq=tq),
        grid=(b, DIFF_HEADS, nq, nk),
        in_specs=[
            pl.BlockSpec((1, tq, LANES), q_map),
            pl.BlockSpec((1, tk, LANES), lambda bi, h, qi, ki: (bi, ki, DK_BLK + h)),
            pl.BlockSpec((1, tk, LANES), lambda bi, h, qi, ki: (bi, ki, DV_BLK + h)),
            pl.BlockSpec((N_META, LANES), lambda bi, h, qi, ki: (0, DK_BLK + h)),
            pl.BlockSpec((N_META, LANES), lambda bi, h, qi, ki: (0, DV_BLK + h)),
            vec(HEAD_DIM), vec(HEAD_DIM), vec(HEAD_DIM), vec(HEAD_DIM), vec(2 * HEAD_DIM),
        ],
        out_specs=pl.BlockSpec((1, tq, LANES), lambda bi, h, qi, ki: (bi, qi, h)),
        out_shape=jax.ShapeDtypeStruct((b, tq_total, DIFF_WIDTH), BF16),
        scratch_shapes=[
            pltpu.VMEM((2 * tq, LANES), BF16),
            pltpu.VMEM((2 * tq, 1), F32),
            pltpu.VMEM((2 * tq, 1), F32),
            pltpu.VMEM((2 * tq, LANES), F32),
        ],
        compiler_params=_cparams(("parallel", "parallel", "parallel", "arbitrary")),
        name="diff_attention",
    )(q_src, proj, proj, proj_meta, proj_meta, lq1, lk1, lq2, lk2, subln)


def _outproj_body(x_ref, na_ref, df_ref, wo_ref, g_ref, wrh_ref, wrl_ref, x2_ref, h2_ref, afft_ref, *aff_ref):
    x2 = x_ref[...] + _dot(na_ref[...], wo_ref[0:NA_WIDTH, :]) + _dot(df_ref[...], wo_ref[NA_WIDTH:, :])
    x2_ref[...] = x2
    ms = jnp.mean(x2 * x2, axis=-1, keepdims=True)
    hf = x2 * lax.rsqrt(ms + RMS_EPS) * g_ref[...]
    hi = hf.astype(BF16)
    h2_ref[...] = hi
    lo = (hf - hi.astype(F32)).astype(BF16)
    logits = _dot_nt(wrh_ref[...], hi) + _dot_nt(wrh_ref[...], lo) + _dot_nt(wrl_ref[...], hi)
    row = lax.broadcasted_iota(I32, logits.shape, 0)
    logits = jnp.where(row < N_EXPERTS, logits, NEG_BIG)
    m = jnp.max(logits, axis=0, keepdims=True)
    e = jnp.exp(logits - m)
    aff = e / jnp.sum(e, axis=0, keepdims=True)
    afft_ref[...] = aff[0:N_EXPERTS]
    if aff_ref:
        aff_ref[0][...] = aff.T


def _outproj(x, na, df, wo, g, wrh, wrl, *, tm, token_major):
    rows, d = x.shape
    out_shape = [jax.ShapeDtypeStruct((rows, d), F32), jax.ShapeDtypeStruct((rows, d), BF16),
                 jax.ShapeDtypeStruct((N_EXPERTS, rows), F32)]
    out_specs = [pl.BlockSpec((tm, d), lambda i: (i, 0)), pl.BlockSpec((tm, d), lambda i: (i, 0)),
                 pl.BlockSpec((N_EXPERTS, tm), lambda i: (0, i))]
    if token_major:
        out_shape.append(jax.ShapeDtypeStruct((rows, LANES), F32))
        out_specs.append(pl.BlockSpec((tm, LANES), lambda i: (i, 0)))
    return pl.pallas_call(
        _outproj_body,
        grid=(rows // tm,),
        in_specs=[
            pl.BlockSpec((tm, d), lambda i: (i, 0)),
            pl.BlockSpec((tm, NA_WIDTH), lambda i: (i, 0)),
            pl.BlockSpec((tm, DIFF_WIDTH), lambda i: (i, 0)),
            pl.BlockSpec((NA_WIDTH + DIFF_WIDTH, d), lambda i: (0, 0)),
            pl.BlockSpec((1, d), lambda i: (0, 0)),
            pl.BlockSpec((LANES, d), lambda i: (0, 0)),
            pl.BlockSpec((LANES, d), lambda i: (0, 0)),
        ],
        out_specs=out_specs,
        out_shape=out_shape,
        compiler_params=_cparams(("parallel",)),
        name="outproj_router",
    )(x, na, df, wo, g, wrh, wrl)


def _select_body(ar_ref, am_ref, gr_ref, gm_ref, tau_ref, cut_ref, offs_ref, cnt_ref, sel_scr, *, cap, nb, idx_bits):
    br = pltpu.bitcast(ar_ref[...], I32)
    bm = pltpu.bitcast(am_ref[...], I32)
    gr = gr_ref[...]
    gm = gm_ref[...]

    def count(mr, mm):
        return (jnp.sum(jnp.where(mr, 1.0, 0.0), axis=1, keepdims=True)
                + jnp.sum(jnp.where(mm, 1.0, 0.0), axis=1, keepdims=True))

    def value_bit(i, ans):
        cand = ans | jnp.left_shift(jnp.int32(1), 30 - i)
        return jnp.where(count(br >= cand, bm >= cand) >= cap, cand, ans)

    tau = lax.fori_loop(0, 31, value_bit, jnp.zeros((N_EXPERTS, 1), I32))
    need = cap - count(br > tau, bm > tau)
    eq_r = br == tau
    eq_m = bm == tau

    def index_bit(i, ans):
        cand = ans | jnp.left_shift(jnp.int32(1), idx_bits - 1 - i)
        return jnp.where(count(eq_r & (gr < cand), eq_m & (gm < cand)) < need, cand, ans)

    cut = lax.fori_loop(0, idx_bits, index_bit, jnp.zeros((N_EXPERTS, 1), I32))
    tau_ref[...] = jnp.broadcast_to(tau, tau_ref.shape)
    cut_ref[...] = jnp.broadcast_to(cut, cut_ref.shape)

    sel_scr[...] = jnp.where((br > tau) | (eq_r & (gr <= cut)), 1.0, 0.0)
    lane = lax.broadcasted_iota(I32, (N_EXPERTS, LANES), 1)

    def block_count(j, acc):
        start = pl.multiple_of(j * TOK_BLK, TOK_BLK)
        c = jnp.sum(sel_scr[:, pl.ds(start, TOK_BLK)], axis=1, keepdims=True)
        return jnp.where(lane == j, c, acc)

    counts = lax.fori_loop(0, nb, block_count, jnp.zeros((N_EXPERTS, LANES), F32))
    incl = counts
    shift = 1
    while shift < LANES:
        incl = incl + jnp.where(lane >= shift, pltpu.roll(incl, shift, 1), 0.0)
        shift *= 2
    offs_ref[...] = (incl - counts).astype(I32)
    cnt_ref[...] = counts.astype(I32)


def _select(aff_t, aff_t_meta, gidx, gidx_meta, *, cap, n_total):
    e, nr = aff_t.shape
    nb = nr // TOK_BLK
    assert nb <= LANES and nr % TOK_BLK == 0
    idx_bits = max(1, (n_total - 1).bit_length())
    shp = jax.ShapeDtypeStruct((e, LANES), I32)
    return pl.pallas_call(
        functools.partial(_select_body, cap=cap, nb=nb, idx_bits=idx_bits),
        out_shape=[shp, shp, shp, shp],
        scratch_shapes=[pltpu.VMEM((e, nr), F32)],
        compiler_params=pltpu.CompilerParams(vmem_limit_bytes=VMEM_LIMIT),
        name="ec_select",
    )(aff_t, aff_t_meta, gidx, gidx_meta)


def _gather_body(offs_ref, cnts_ref, aff_ref, tau_ref, cut_ref, h_ref, tri_ref, xe_ref, *, nb, t, l_total):
    e = pl.program_id(0)
    j = pl.program_id(1)

    @pl.when(j == 0)
    def _():
        xe_ref[...] = jnp.zeros_like(xe_ref)

    tau = tau_ref[0][:, 0:1]
    cut = cut_ref[0][:, 0:1]
    bits = pltpu.bitcast(aff_ref[0], I32)
    row0 = j * TOK_BLK
    gidx = row0 + (row0 // t) * (l_total - t) + (l_total - t) + lax.broadcasted_iota(I32, bits.shape, 1)
    sel = (bits > tau) | ((bits == tau) & (gidx <= cut))
    self32 = jnp.where(sel, 1.0, 0.0)
    incl = _dot(jnp.broadcast_to(self32, (8, TOK_BLK)).astype(BF16), tri_ref[...])[0:1]
    cnt = cnts_ref[e * nb + j]
    off = offs_ref[e * nb + j]
    off_al = (off // 16) * 16
    rel = jnp.where(sel, incl.astype(I32) - 1 + (off - off_al), -1)
    nchunk = (off - off_al + cnt + ROW_WIN - 1) // ROW_WIN
    hblk = h_ref[...]
    rowid = lax.broadcasted_iota(I32, (ROW_WIN, TOK_BLK), 0)

    def chunk(c, carry):
        onehot = jnp.where(rowid == rel - c * ROW_WIN, 1.0, 0.0).astype(BF16)
        g = _dot(onehot, hblk).astype(BF16)
        start = pl.multiple_of(off_al + c * ROW_WIN, 16)
        xe_ref[0, pl.ds(start, ROW_WIN), :] = xe_ref[0, pl.ds(start, ROW_WIN), :] + g
        return carry

    lax.fori_loop(0, nchunk, chunk, 0)


def _gather(offs_flat, cnts_flat, aff_t3, tau, cut, h2, tri_u, *, cap_pad, t, l_total):
    e = aff_t3.shape[0]
    nr, d = h2.shape
    nb = nr // TOK_BLK
    return pl.pallas_call(
        functools.partial(_gather_body, nb=nb, t=t, l_total=l_total),
        grid_spec=pltpu.PrefetchScalarGridSpec(
            num_scalar_prefetch=2,
            grid=(e, nb),
            in_specs=[
                pl.BlockSpec((1, 1, TOK_BLK), lambda ei, j, o, c: (ei, 0, j)),
                pl.BlockSpec((1, 1, LANES), lambda ei, j, o, c: (ei, 0, 0)),
                pl.BlockSpec((1, 1, LANES), lambda ei, j, o, c: (ei, 0, 0)),
                pl.BlockSpec((TOK_BLK, d), lambda ei, j, o, c: (j, 0)),
                pl.BlockSpec((TOK_BLK, TOK_BLK), lambda ei, j, o, c: (0, 0)),
            ],
            out_specs=pl.BlockSpec((1, cap_pad, d), lambda ei, j, o, c: (ei, 0, 0)),
        ),
        out_shape=jax.ShapeDtypeStruct((e, cap_pad, d), BF16),
        compiler_params=_cparams(("parallel", "arbitrary")),
        name="ec_gather",
    )(offs_flat, cnts_flat, aff_t3, tau, cut, h2, tri_u)


def _ffn_body(x_ref, wg_ref, wu_ref, wd_ref, o_ref, *, fc):
    x = x_ref[0]
    f = wg_ref.shape[2]
    acc = jnp.zeros((x.shape[0], wd_ref.shape[2]), F32)
    for c in range(f // fc):
        a = _dot(x, wg_ref[0, :, c * fc:(c + 1) * fc])
        b = _dot(x, wu_ref[0, :, c * fc:(c + 1) * fc])
        hmid = (a * jax.nn.sigmoid(a) * b).astype(BF16)
        acc = acc + _dot(hmid, wd_ref[0, c * fc:(c + 1) * fc, :])
    o_ref[0] = acc.astype(BF16)


def _ffn(xe, wg, wu, wd, *, tm):
    e, cap_pad, d = xe.shape
    f = wg.shape[2]
    return pl.pallas_call(
        functools.partial(_ffn_body, fc=min(512, f)),
        grid=(e, cap_pad // tm),
        in_specs=[
            pl.BlockSpec((1, tm, d), lambda ei, i: (ei, i, 0)),
            pl.BlockSpec((1, d, f), lambda ei, i: (ei, 0, 0)),
            pl.BlockSpec((1, d, f), lambda ei, i: (ei, 0, 0)),
            pl.BlockSpec((1, f, d), lambda ei, i: (ei, 0, 0)),
        ],
        out_specs=pl.BlockSpec((1, tm, d), lambda ei, i: (ei, i, 0)),
        out_shape=jax.ShapeDtypeStruct((e, cap_pad, d), BF16),
        compiler_params=_cparams(("parallel", "arbitrary")),
        name="ec_ffn",
    )(xe, wg, wu, wd)


def _combine_body(offs_ref, cnts_ref, aff_ref, tau_ref, cut_ref, x2_ref, tri_ref, fn_ref, ywin_ref, ye_hbm,
                  o_ref, rank_scr, gate_scr, acc_scr, ybuf, sem, *, nb, t, l_total):
    j = pl.program_id(0)
    e = pl.program_id(1)

    @pl.when(e == 0)
    def _():
        aff = aff_ref[...]
        bits = pltpu.bitcast(aff, I32)
        tau = tau_ref[0:1, :]
        cut = cut_ref[0:1, :]
        row0 = j * TOK_BLK
        gidx = row0 + (row0 // t) * (l_total - t) + (l_total - t) + lax.broadcasted_iota(I32, bits.shape, 0)
        sel = (bits > tau) | ((bits == tau) & (gidx <= cut))
        sel = sel & (lax.broadcasted_iota(I32, bits.shape, 1) < N_EXPERTS)
        self32 = jnp.where(sel, 1.0, 0.0)
        incl = _dot(tri_ref[...], self32.astype(BF16))
        rank_scr[...] = jnp.where(sel, incl - 1.0, -1.0)
        gate_scr[...] = jnp.where(sel, aff, 0.0)
        acc_scr[...] = x2_ref[...]

    lane = lax.broadcasted_iota(I32, (TOK_BLK, LANES), 1)
    pick = lane == e
    rank = jnp.sum(jnp.where(pick, rank_scr[...], 0.0), axis=1, keepdims=True).astype(I32)
    gate = jnp.sum(jnp.where(pick, gate_scr[...], 0.0), axis=1, keepdims=True)
    cnt = cnts_ref[e * nb + j]
    off = offs_ref[e * nb + j]
    off_al = (off // 16) * 16
    rel = jnp.where(rank >= 0, rank + (off - off_al), -1)
    nchunk = (off - off_al + cnt + ROW_WIN - 1) // ROW_WIN
    colid = lax.broadcasted_iota(I32, (TOK_BLK, ROW_WIN), 1)

    onehot = jnp.where(colid == rel, 1.0, 0.0).astype(BF16)
    acc_scr[...] += _dot(onehot, ywin_ref[...]) * gate

    def chunk(c, carry):
        start = pl.multiple_of(off_al + c * ROW_WIN, 16)
        cp = pltpu.make_async_copy(ye_hbm.at[e, pl.ds(start, ROW_WIN), :], ybuf, sem)
        cp.start()
        cp.wait()
        oh = jnp.where(colid == rel - c * ROW_WIN, 1.0, 0.0).astype(BF16)
        acc_scr[...] += _dot(oh, ybuf[...]) * gate
        return carry

    lax.fori_loop(1, nchunk, chunk, 0)

    @pl.when(e == pl.num_programs(1) - 1)
    def _():
        y = acc_scr[...]
        ms = jnp.mean(y * y, axis=-1, keepdims=True)
        o_ref[...] = y * lax.rsqrt(ms + RMS_EPS) * fn_ref[...]


def _combine(offs_flat, cnts_flat, aff, tau_row, cut_row, x2, tri_l, fnorm, ye, *, t, l_total):
    nr, d = x2.shape
    nb = nr // TOK_BLK
    e = ye.shape[0]

    def win_map(j, ei, o, c):
        return (ei, (o[ei * nb + j] // 16) * 16, 0)

    return pl.pallas_call(
        functools.partial(_combine_body, nb=nb, t=t, l_total=l_total),
        grid_spec=pltpu.PrefetchScalarGridSpec(
            num_scalar_prefetch=2,
            grid=(nb, e),
            in_specs=[
                pl.BlockSpec((TOK_BLK, LANES), lambda j, ei, o, c: (j, 0)),
                pl.BlockSpec((8, LANES), lambda j, ei, o, c: (0, 0)),
                pl.BlockSpec((8, LANES), lambda j, ei, o, c: (0, 0)),
                pl.BlockSpec((TOK_BLK, d), lambda j, ei, o, c: (j, 0)),
                pl.BlockSpec((TOK_BLK, TOK_BLK), lambda j, ei, o, c: (0, 0)),
                pl.BlockSpec((1, d), lambda j, ei, o, c: (0, 0)),
                pl.BlockSpec((pl.Squeezed(), pl.Element(ROW_WIN), pl.Element(d)), win_map),
                pl.BlockSpec(memory_space=pl.ANY),
            ],
            out_specs=pl.BlockSpec((TOK_BLK, d), lambda j, ei, o, c: (j, 0)),
            scratch_shapes=[
                pltpu.VMEM((TOK_BLK, LANES), F32),
                pltpu.VMEM((TOK_BLK, LANES), F32),
                pltpu.VMEM((TOK_BLK, d), F32),
                pltpu.VMEM((ROW_WIN, d), BF16),
                pltpu.SemaphoreType.DMA(()),
            ],
        ),
        out_shape=jax.ShapeDtypeStruct((nr, d), F32),
        compiler_params=_cparams(("parallel", "arbitrary")),
        name="ec_combine",
    )(offs_flat, cnts_flat, aff, tau_row, cut_row, x2, tri_l, fnorm, ye, ye)


def _rope_tables(positions):
    inv_freq = 1.0 / (ROPE_THETA ** (jnp.arange(0, HEAD_DIM, 2, dtype=F32) / HEAD_DIM))
    ang = positions.astype(F32)[:, None] * inv_freq[None, :]
    cos, sin = jnp.cos(ang), jnp.sin(ang)
    return (jnp.concatenate([cos, cos, cos, cos], axis=1),
            jnp.concatenate([-sin, -sin, sin, sin], axis=1))


def _permute_in_weights(w_in):
    l = jnp.arange(LANES)
    g = l // 32
    within = (g % 2) * HEAD_DIM + (g // 2) * 32 + l % 32
    cols = jnp.arange(PROJ_WIDTH)
    blk = cols // LANES
    permuted = blk * LANES + within[cols % LANES]
    cols = jnp.where((blk >= DQ_BLK) & (blk < DV_BLK), permuted, cols)
    return w_in[:, cols]


def _encode_group(x, shared):
    b, t, d = x.shape
    l_total = N_META + t
    n_total = b * l_total
    cap = EC_CAPACITY_FACTOR * n_total // N_EXPERTS
    rows = b * t
    xf = x.reshape(rows, d)

    cos, sin = _rope_tables(N_META + jnp.arange(t))
    proj = _inproj(xf, shared["norm_mix"], shared["w_in"], cos, sin, tm=512).reshape(b, t, PROJ_WIDTH)
    proj_meta = shared["proj_meta"]

    na = _na_attention(proj, proj_meta, shared["na_table"], shared["mb_stack"])
    lam = shared["lambda"]
    tk = min(1024, t)
    df = _diff_attention(proj, proj, proj_meta, *lam, shared["subln"], tq=min(512, t), tk=tk, q_shared=False)
    df_meta = _diff_attention(proj_meta[None], proj, proj_meta, *lam, shared["subln"],
                              tq=N_META, tk=tk, q_shared=True)

    wr = (shared["wr_hi"], shared["wr_lo"])
    x2, h2, aff_t, aff = _outproj(xf, na.reshape(rows, NA_WIDTH), df.reshape(rows, DIFF_WIDTH), shared["w_out"],
                                  shared["norm_ffn"], *wr, tm=512, token_major=True)
    xm = jnp.tile(shared["meta_tokens"], (b, 1))
    nam = jnp.tile(shared["na_meta"], (b, 1))
    _, _, aff_t_meta = _outproj(xm, nam, df_meta.reshape(b * N_META, DIFF_WIDTH), shared["w_out"],
                                shared["norm_ffn"], *wr, tm=b * N_META, token_major=False)

    r = jnp.arange(rows, dtype=I32)
    gidx = (r + (r // t + 1) * N_META)[None, :]
    rm = jnp.arange(b * N_META, dtype=I32)
    gidx_meta = ((rm // N_META) * l_total + rm % N_META)[None, :]
    tau, cut, offs, cnts = _select(aff_t, aff_t_meta, gidx, gidx_meta, cap=cap, n_total=n_total)

    nb = rows // TOK_BLK
    offs_flat = offs[:, :nb].reshape(-1)
    cnts_flat = cnts[:, :nb].reshape(-1)
    ffn_tm = 768
    cap_pad = -(-(cap + ROW_WIN) // ffn_tm) * ffn_tm
    xe = _gather(offs_flat, cnts_flat, aff_t.reshape(N_EXPERTS, 1, rows), tau.reshape(N_EXPERTS, 1, LANES),
                 cut.reshape(N_EXPERTS, 1, LANES), h2, shared["tri_u"], cap_pad=cap_pad, t=t, l_total=l_total)
    ye = _ffn(xe, shared["w_gate"], shared["w_up"], shared["w_down"], tm=ffn_tm)

    pad = jnp.zeros((8, LANES - N_EXPERTS), I32)
    tau_row = jnp.concatenate([jnp.broadcast_to(tau[:, 0][None, :], (8, N_EXPERTS)), pad], axis=1)
    cut_row = jnp.concatenate([jnp.broadcast_to(cut[:, 0][None, :], (8, N_EXPERTS)), pad], axis=1)
    y = _combine(offs_flat, cnts_flat, aff, tau_row, cut_row, x2, shared["tri_l"], shared["final_norm"], ye,
                 t=t, l_total=l_total)
    return y.reshape(b, t, d)


def kernel(x_prompt, x_sample, meta_tokens, norm_mix, w_in, na_rpb, na_meta_bias, lambda_q1, lambda_k1,
           lambda_q2, lambda_k2, diff_subln, w_out, norm_ffn, w_router, w_gate, w_up, w_down, final_norm):
    d = x_prompt.shape[-1]
    wr = jnp.zeros((LANES, d), F32).at[:N_EXPERTS].set(w_router[0].T)
    wr_hi = wr.astype(BF16)
    mb = na_meta_bias[0].astype(F32)
    idx = jnp.arange(TOK_BLK)
    shared = {
        "meta_tokens": meta_tokens,
        "norm_mix": norm_mix[0][None, :],
        "w_in": _permute_in_weights(w_in[0]).astype(BF16),
        "lambda": (lambda_q1, lambda_k1, lambda_q2, lambda_k2),
        "subln": diff_subln,
        "w_out": w_out[0].astype(BF16),
        "norm_ffn": norm_ffn[0][None, :],
        "wr_hi": wr_hi,
        "wr_lo": (wr - wr_hi.astype(F32)).astype(BF16),
        "w_gate": w_gate[0].astype(BF16),
        "w_up": w_up[0].astype(BF16),
        "w_down": w_down[0].astype(BF16),
        "final_norm": final_norm[None, :],
        "mb_stack": jnp.repeat(mb, GRID_W, axis=0).reshape(NA_HEADS // 2, 2 * GRID_W, N_META),
        "tri_u": (idx[:, None] <= idx[None, :]).astype(BF16),
        "tri_l": (idx[:, None] >= idx[None, :]).astype(BF16),
    }
    cos_m, sin_m = _rope_tables(jnp.arange(N_META))
    shared["proj_meta"] = _inproj(meta_tokens, shared["norm_mix"], shared["w_in"], cos_m, sin_m, tm=N_META)
    mb_meta = jnp.repeat(mb, N_META, axis=0).reshape(NA_HEADS // 2, 2 * N_META, N_META)
    shared["na_meta"] = _na_meta(shared["proj_meta"], mb_meta)
    shared["na_table"] = _na_bias(na_rpb[0].astype(F32))
    return (_encode_group(x_prompt, shared), _encode_group(x_sample, shared))
```

```python
import functools
import math

import jax
import jax.numpy as jnp
from jax import lax
from jax.experimental import pallas as pl
from jax.experimental.pallas import tpu as pltpu

BF16 = jnp.bfloat16
F32 = jnp.float32
I32 = jnp.int32

N_META = 16
GRID_W = 64
HEAD_DIM = 64
NA_HEADS = 8
NA_WIDTH = NA_HEADS * HEAD_DIM
NA_KH = 8
NA_KW = 16
DIFF_HEADS = 4
DIFF_WIDTH = DIFF_HEADS * 2 * HEAD_DIM
PROJ_WIDTH = 3 * NA_WIDTH + 3 * DIFF_WIDTH
ROPE_THETA = 10000.0
N_EXPERTS = 16
EC_CAPACITY_FACTOR = 2
RMS_EPS = 1e-6
SUBLN_EPS = 1e-5
LAMBDA_INIT = 0.8 - 0.6 * math.exp(-0.3 * 0)

LANES = 128
NEG_BIG = -1e30
VMEM_LIMIT = 56 * 1024 * 1024

NAQ_BLK, NAK_BLK, NAV_BLK = 0, 4, 8
DQ_BLK, DK_BLK, DV_BLK = 12, 16, 20

ROW_WIN = 256
TOK_BLK = 1024


def _cparams(sem, vmem=VMEM_LIMIT):
    return pltpu.CompilerParams(dimension_semantics=sem, vmem_limit_bytes=vmem)


def _dot(a, b):
    return jnp.dot(a, b, preferred_element_type=F32)


def _dot_nt(a, b):
    return lax.dot_general(a, b, (((1,), (1,)), ((), ())), preferred_element_type=F32)


def _inproj_body(x_ref, g_ref, w_ref, cos_ref, sin_ref, o_ref, *, tn):
    x = x_ref[...]
    ms = jnp.mean(x * x, axis=-1, keepdims=True)
    h = (x * lax.rsqrt(ms + RMS_EPS) * g_ref[...]).astype(BF16)
    cos = cos_ref[...]
    sin = sin_ref[...]
    for c in range(PROJ_WIDTH // tn):
        lo = c * tn
        acc = _dot(h, w_ref[:, lo:lo + tn])
        for s in range(tn // LANES):
            blk = (lo + s * LANES) // LANES
            a = acc[:, s * LANES:(s + 1) * LANES]
            if DQ_BLK <= blk < DV_BLK:
                a = a * cos + pltpu.roll(a, 64, 1) * sin
            if blk < NAK_BLK:
                a = a * (HEAD_DIM ** -0.5)
            if DQ_BLK <= blk < DK_BLK:
                a = a * (HEAD_DIM ** -0.5 * math.log2(math.e))
            o_ref[:, lo + s * LANES:lo + (s + 1) * LANES] = a.astype(BF16)


def _inproj(x, g, w, cos, sin, *, tm):
    rows, d = x.shape
    nt = cos.shape[0] // tm
    return pl.pallas_call(
        functools.partial(_inproj_body, tn=512),
        grid=(rows // tm,),
        in_specs=[
            pl.BlockSpec((tm, d), lambda i: (i, 0)),
            pl.BlockSpec((1, d), lambda i: (0, 0)),
            pl.BlockSpec((d, PROJ_WIDTH), lambda i: (0, 0)),
            pl.BlockSpec((tm, LANES), lambda i: (i % nt, 0)),
            pl.BlockSpec((tm, LANES), lambda i: (i % nt, 0)),
        ],
        out_specs=pl.BlockSpec((tm, PROJ_WIDTH), lambda i: (i, 0)),
        out_shape=jax.ShapeDtypeStruct((rows, PROJ_WIDTH), BF16),
        compiler_params=_cparams(("parallel",)),
        name="inproj",
    )(x, g, w, cos, sin)


def _na_bias_body(rpb_ref, o_ref):
    h = pl.program_id(0)
    c = lax.broadcasted_iota(I32, (GRID_W, GRID_W), 0)
    kc = lax.broadcasted_iota(I32, (GRID_W, GRID_W), 1)
    cs = jnp.clip(c - NA_KW // 2, 0, GRID_W - NA_KW)
    valid = (kc >= cs) & (kc < cs + NA_KW)
    jm = kc - c + NA_KW - 1
    n_dr = 2 * NA_KH - 1
    n_j = 2 * NA_KW - 1
    tiles = []
    for dr in range(n_dr):
        base = (h * n_dr + dr) * n_j

        def body(j, acc, base=base):
            return jnp.where(jm == j, rpb_ref[base + j], acc)

        t = lax.fori_loop(0, n_j, body, jnp.zeros((GRID_W, GRID_W), F32))
        tiles.append(jnp.where(valid, t, NEG_BIG))
    for d in range(NA_KH):
        o_ref[0, d] = jnp.concatenate([tiles[kr - d + NA_KH - 1] for kr in range(NA_KH)], axis=1)


def _na_bias(rpb):
    h = rpb.shape[0]
    return pl.pallas_call(
        _na_bias_body,
        grid_spec=pltpu.PrefetchScalarGridSpec(
            num_scalar_prefetch=1,
            grid=(h,),
            in_specs=[],
            out_specs=pl.BlockSpec((1, NA_KH, GRID_W, NA_KH * GRID_W), lambda i, r: (i, 0, 0, 0)),
        ),
        out_shape=jax.ShapeDtypeStruct((h, NA_KH, GRID_W, NA_KH * GRID_W), F32),
        compiler_params=_cparams(("arbitrary",)),
        name="na_bias",
    )(rpb.reshape(-1))


def _split_heads(q):
    lane = lax.broadcasted_iota(I32, q.shape, 1)
    zero = jnp.zeros_like(q)
    return jnp.concatenate([jnp.where(lane < HEAD_DIM, q, zero), jnp.where(lane < HEAD_DIM, zero, q)], axis=0)


def _merge_heads(o):
    n = o.shape[0] // 2
    lane = lax.broadcasted_iota(I32, (n, LANES), 1)
    return jnp.where(lane < HEAD_DIM, o[:n], o[n:])


def _na_body(q_ref, kp_ref, kc_ref, kn_ref, vp_ref, vc_ref, vn_ref, km_ref, vm_ref, tab_ref, mb_ref,
             o_ref, kbuf, vbuf, *, rows):
    rb = pl.program_id(2)
    blk = NA_KH * GRID_W
    kbuf[0:blk] = kp_ref[0]
    kbuf[blk:2 * blk] = kc_ref[0]
    kbuf[2 * blk:3 * blk] = kn_ref[0]
    vbuf[0:blk] = vp_ref[0]
    vbuf[blk:2 * blk] = vc_ref[0]
    vbuf[2 * blk:3 * blk] = vn_ref[0]
    km = km_ref[...]
    vm = vm_ref[...]
    mb = mb_ref[0]
    for i in range(NA_KH):
        r = rb * NA_KH + i
        rs = jnp.clip(r - NA_KH // 2, 0, rows - NA_KH)
        d = r - rs
        start = pl.multiple_of((rs - (rb - 1) * NA_KH) * GRID_W, GRID_W)
        qq = _split_heads(q_ref[0, i * GRID_W:(i + 1) * GRID_W, :])
        kwin = kbuf[pl.ds(start, blk), :]
        vwin = vbuf[pl.ds(start, blk), :]
        bias = jnp.concatenate([tab_ref[0, d], tab_ref[1, d]], axis=0)
        s = _dot_nt(qq, kwin) + bias
        sm = _dot_nt(qq, km) + mb
        m = jnp.maximum(jnp.max(s, axis=1, keepdims=True), jnp.max(sm, axis=1, keepdims=True))
        p = jnp.exp(s - m)
        pm = jnp.exp(sm - m)
        l = jnp.sum(p, axis=1, keepdims=True) + jnp.sum(pm, axis=1, keepdims=True)
        o = (_dot(p.astype(BF16), vwin) + _dot(pm.astype(BF16), vm)) / l
        o_ref[0, i * GRID_W:(i + 1) * GRID_W, :] = _merge_heads(o).astype(BF16)


def _na_attention(proj, proj_meta, table, mb_stack):
    b, t, _ = proj.shape
    rows = t // GRID_W
    nrb = rows // NA_KH
    blk = NA_KH * GRID_W
    hp = NA_HEADS // 2

    def kv_spec(col, shift):
        return pl.BlockSpec((1, blk, LANES),
                            lambda h, bi, rb: (bi, jnp.clip(rb + shift, 0, nrb - 1), col + h))

    return pl.pallas_call(
        functools.partial(_na_body, rows=rows),
        grid=(hp, b, nrb),
        in_specs=[
            pl.BlockSpec((1, blk, LANES), lambda h, bi, rb: (bi, rb, NAQ_BLK + h)),
            kv_spec(NAK_BLK, -1), kv_spec(NAK_BLK, 0), kv_spec(NAK_BLK, 1),
            kv_spec(NAV_BLK, -1), kv_spec(NAV_BLK, 0), kv_spec(NAV_BLK, 1),
            pl.BlockSpec((N_META, LANES), lambda h, bi, rb: (0, NAK_BLK + h)),
            pl.BlockSpec((N_META, LANES), lambda h, bi, rb: (0, NAV_BLK + h)),
            pl.BlockSpec((2, NA_KH, GRID_W, blk), lambda h, bi, rb: (h, 0, 0, 0)),
            pl.BlockSpec((1, LANES, N_META), lambda h, bi, rb: (h, 0, 0)),
        ],
        out_specs=pl.BlockSpec((1, blk, LANES), lambda h, bi, rb: (bi, rb, h)),
        out_shape=jax.ShapeDtypeStruct((b, t, NA_WIDTH), BF16),
        scratch_shapes=[pltpu.VMEM((3 * blk, LANES), BF16), pltpu.VMEM((3 * blk, LANES), BF16)],
        compiler_params=_cparams(("arbitrary", "arbitrary", "arbitrary")),
        name="na_attention",
    )(proj, proj, proj, proj, proj, proj, proj, proj_meta, proj_meta, table, mb_stack)


def _na_meta_body(p_ref, mb_ref, o_ref):
    for h in range(NA_HEADS // 2):
        qq = _split_heads(p_ref[:, (NAQ_BLK + h) * LANES:(NAQ_BLK + h + 1) * LANES])
        km = p_ref[:, (NAK_BLK + h) * LANES:(NAK_BLK + h + 1) * LANES]
        vm = p_ref[:, (NAV_BLK + h) * LANES:(NAV_BLK + h + 1) * LANES]
        s = _dot_nt(qq, km) + mb_ref[h, 0:2 * N_META, :]
        m = jnp.max(s, axis=1, keepdims=True)
        p = jnp.exp(s - m)
        l = jnp.sum(p, axis=1, keepdims=True)
        o = _dot(p.astype(BF16), vm) / l
        o_ref[:, h * LANES:(h + 1) * LANES] = _merge_heads(o).astype(BF16)


def _na_meta(proj_meta, mb_meta):
    return pl.pallas_call(
        _na_meta_body,
        out_shape=jax.ShapeDtypeStruct((N_META, NA_WIDTH), BF16),
        name="na_meta",
    )(proj_meta, mb_meta)


def _split_maps(q):
    lane = lax.broadcasted_iota(I32, q.shape, 1)
    is0 = ((lane >> 5) & 1) == 0
    zero = jnp.zeros_like(q)
    return jnp.concatenate([jnp.where(is0, q, zero), jnp.where(is0, zero, q)], axis=0)


def _online_softmax(s_ref, p_ref, a_ref, m_scr, l_scr, rows):
    s = s_ref[rows, :]
    m_prev = m_scr[rows]
    m_new = jnp.maximum(m_prev, jnp.max(s, axis=1, keepdims=True))
    alpha = jnp.exp2(m_prev - m_new)
    p = jnp.exp2(s - m_new)
    l_scr[rows] = alpha * l_scr[rows] + jnp.sum(p, axis=1, keepdims=True)
    m_scr[rows] = m_new
    a_ref[rows] = alpha
    p_ref[rows, :] = p.astype(BF16)


def _diff_body(q_ref, k_ref, kn_ref, vp_ref, v_ref, km_ref, vm_ref, lq1_ref, lk1_ref, lq2_ref, lk2_ref, sub_ref,
               o_ref, qq_scr, s0, s1, p0, p1, a0, a1, m_scr, l_scr, acc_scr, *, tq, rc, nk):
    ki = pl.program_id(3)

    @pl.when(ki == 0)
    def _():
        qq = _split_maps(q_ref[0])
        qq_scr[...] = qq
        s = _dot_nt(qq, km_ref[...])
        m = jnp.max(s, axis=1, keepdims=True)
        p = jnp.exp2(s - m)
        m_scr[...] = m
        l_scr[...] = jnp.sum(p, axis=1, keepdims=True)
        acc_scr[...] = _dot(p.astype(BF16), vm_ref[...])
        s0[...] = _dot_nt(qq, k_ref[0])
        p1[...] = jnp.zeros_like(p1)
        a1[...] = jnp.ones_like(a1)

    def step(s_cur, s_nxt, p_cur, p_prv, a_cur, a_prv):
        for g in range(2 * tq // rc):
            rows = slice(g * rc, (g + 1) * rc)
            s_nxt[rows, :] = _dot_nt(qq_scr[rows, :], kn_ref[0])
            acc_scr[rows, :] = a_prv[rows] * acc_scr[rows, :] + _dot(p_prv[rows, :], vp_ref[0])
            _online_softmax(s_cur, p_cur, a_cur, m_scr, l_scr, rows)

    @pl.when(ki % 2 == 0)
    def _():
        step(s0, s1, p0, p1, a0, a1)

    @pl.when(ki % 2 == 1)
    def _():
        step(s1, s0, p1, p0, a1, a0)

    @pl.when(ki == nk - 1)
    def _():
        p_last, a_last = (p0, a0) if (nk - 1) % 2 == 0 else (p1, a1)
        acc = a_last[...] * acc_scr[...] + _dot(p_last[...], v_ref[0])
        lam = (jnp.exp(jnp.sum(lq1_ref[...] * lk1_ref[...], axis=1, keepdims=True))
               - jnp.exp(jnp.sum(lq2_ref[...] * lk2_ref[...], axis=1, keepdims=True)) + LAMBDA_INIT)
        o_all = acc / l_scr[...]
        o = o_all[:tq] - lam * o_all[tq:]
        ms = jnp.mean(o * o, axis=-1, keepdims=True)
        o = o * lax.rsqrt(ms + SUBLN_EPS) * sub_ref[...] * (1.0 - LAMBDA_INIT)
        o_ref[0] = o.astype(BF16)


def _diff_attention(q_src, proj, proj_meta, lq1, lk1, lq2, lk2, subln, *, tq, tk, q_shared):
    b, t, _ = proj.shape
    tq_total = q_src.shape[1]
    nq = tq_total // tq
    nk = t // tk
    if q_shared:
        q_map = lambda bi, h, qi, ki: (0, qi, DQ_BLK + h)
    else:
        q_map = lambda bi, h, qi, ki: (bi, qi, DQ_BLK + h)
    vec = lambda n: pl.BlockSpec((1, n), lambda bi, h, qi, ki: (0, 0))
    return pl.pallas_call(
        functools.partial(_diff_body, tq=tq, rc=min(512, 2 * tq), nk=nk),
        grid=(b, DIFF_HEADS, nq, nk),
        in_specs=[
            pl.BlockSpec((1, tq, LANES), q_map),
            pl.BlockSpec((1, tk, LANES), lambda bi, h, qi, ki: (bi, ki, DK_BLK + h)),
            pl.BlockSpec((1, tk, LANES), lambda bi, h, qi, ki: (bi, jnp.minimum(ki + 1, nk - 1), DK_BLK + h)),
            pl.BlockSpec((1, tk, LANES), lambda bi, h, qi, ki: (bi, jnp.maximum(ki - 1, 0), DV_BLK + h)),
            pl.BlockSpec((1, tk, LANES), lambda bi, h, qi, ki: (bi, ki, DV_BLK + h)),
            pl.BlockSpec((N_META, LANES), lambda bi, h, qi, ki: (0, DK_BLK + h)),
            pl.BlockSpec((N_META, LANES), lambda bi, h, qi, ki: (0, DV_BLK + h)),
            vec(HEAD_DIM), vec(HEAD_DIM), vec(HEAD_DIM), vec(HEAD_DIM), vec(2 * HEAD_DIM),
        ],
        out_specs=pl.BlockSpec((1, tq, LANES), lambda bi, h, qi, ki: (bi, qi, h)),
        out_shape=jax.ShapeDtypeStruct((b, tq_total, DIFF_WIDTH), BF16),
        scratch_shapes=[
            pltpu.VMEM((2 * tq, LANES), BF16),
            pltpu.VMEM((2 * tq, tk), F32), pltpu.VMEM((2 * tq, tk), F32),
            pltpu.VMEM((2 * tq, tk), BF16), pltpu.VMEM((2 * tq, tk), BF16),
            pltpu.VMEM((2 * tq, 1), F32), pltpu.VMEM((2 * tq, 1), F32),
            pltpu.VMEM((2 * tq, 1), F32),
            pltpu.VMEM((2 * tq, 1), F32),
            pltpu.VMEM((2 * tq, LANES), F32),
        ],
        compiler_params=_cparams(("parallel", "parallel", "parallel", "arbitrary")),
        name="diff_attention",
    )(q_src, proj, proj, proj, proj, proj_meta, proj_meta, lq1, lk1, lq2, lk2, subln)


def _outproj_body(x_ref, na_ref, df_ref, wo_ref, g_ref, wrh_ref, wrl_ref, x2_ref, h2_ref, afft_ref, *aff_ref):
    x2 = x_ref[...] + _dot(na_ref[...], wo_ref[0:NA_WIDTH, :]) + _dot(df_ref[...], wo_ref[NA_WIDTH:, :])
    x2_ref[...] = x2
    ms = jnp.mean(x2 * x2, axis=-1, keepdims=True)
    hf = x2 * lax.rsqrt(ms + RMS_EPS) * g_ref[...]
    hi = hf.astype(BF16)
    h2_ref[...] = hi
    lo = (hf - hi.astype(F32)).astype(BF16)
    logits = _dot_nt(wrh_ref[...], hi) + _dot_nt(wrh_ref[...], lo) + _dot_nt(wrl_ref[...], hi)
    row = lax.broadcasted_iota(I32, logits.shape, 0)
    logits = jnp.where(row < N_EXPERTS, logits, NEG_BIG)
    m = jnp.max(logits, axis=0, keepdims=True)
    e = jnp.exp(logits - m)
    aff = e / jnp.sum(e, axis=0, keepdims=True)
    afft_ref[...] = aff[0:N_EXPERTS]
    if aff_ref:
        aff_ref[0][...] = aff.T


def _outproj(x, na, df, wo, g, wrh, wrl, *, tm, token_major):
    rows, d = x.shape
    out_shape = [jax.ShapeDtypeStruct((rows, d), F32), jax.ShapeDtypeStruct((rows, d), BF16),
                 jax.ShapeDtypeStruct((N_EXPERTS, rows), F32)]
    out_specs = [pl.BlockSpec((tm, d), lambda i: (i, 0)), pl.BlockSpec((tm, d), lambda i: (i, 0)),
                 pl.BlockSpec((N_EXPERTS, tm), lambda i: (0, i))]
    if token_major:
        out_shape.append(jax.ShapeDtypeStruct((rows, LANES), F32))
        out_specs.append(pl.BlockSpec((tm, LANES), lambda i: (i, 0)))
    return pl.pallas_call(
        _outproj_body,
        grid=(rows // tm,),
        in_specs=[
            pl.BlockSpec((tm, d), lambda i: (i, 0)),
            pl.BlockSpec((tm, NA_WIDTH), lambda i: (i, 0)),
            pl.BlockSpec((tm, DIFF_WIDTH), lambda i: (i, 0)),
            pl.BlockSpec((NA_WIDTH + DIFF_WIDTH, d), lambda i: (0, 0)),
            pl.BlockSpec((1, d), lambda i: (0, 0)),
            pl.BlockSpec((LANES, d), lambda i: (0, 0)),
            pl.BlockSpec((LANES, d), lambda i: (0, 0)),
        ],
        out_specs=out_specs,
        out_shape=out_shape,
        compiler_params=_cparams(("parallel",)),
        name="outproj_router",
    )(x, na, df, wo, g, wrh, wrl)


def _select_body(ar_ref, am_ref, gr_ref, gm_ref, tau_ref, cut_ref, offs_ref, cnt_ref, sel_scr, *, cap, nb, idx_bits):
    br = pltpu.bitcast(ar_ref[...], I32)
    bm = pltpu.bitcast(am_ref[...], I32)
    gr = gr_ref[...]
    gm = gm_ref[...]

    def count(mr, mm):
        return (jnp.sum(jnp.where(mr, 1.0, 0.0), axis=1, keepdims=True)
                + jnp.sum(jnp.where(mm, 1.0, 0.0), axis=1, keepdims=True))

    def value_bit(i, ans):
        cand = ans | jnp.left_shift(jnp.int32(1), 30 - i)
        return jnp.where(count(br >= cand, bm >= cand) >= cap, cand, ans)

    tau = lax.fori_loop(0, 31, value_bit, jnp.zeros((N_EXPERTS, 1), I32))
    need = cap - count(br > tau, bm > tau)
    eq_r = br == tau
    eq_m = bm == tau

    def index_bit(i, ans):
        cand = ans | jnp.left_shift(jnp.int32(1), idx_bits - 1 - i)
        return jnp.where(count(eq_r & (gr < cand), eq_m & (gm < cand)) < need, cand, ans)

    cut = lax.fori_loop(0, idx_bits, index_bit, jnp.zeros((N_EXPERTS, 1), I32))
    tau_ref[...] = jnp.broadcast_to(tau, tau_ref.shape)
    cut_ref[...] = jnp.broadcast_to(cut, cut_ref.shape)

    sel_scr[...] = jnp.where((br > tau) | (eq_r & (gr <= cut)), 1.0, 0.0)
    lane = lax.broadcasted_iota(I32, (N_EXPERTS, LANES), 1)

    def block_count(j, acc):
        start = pl.multiple_of(j * TOK_BLK, TOK_BLK)
        c = jnp.sum(sel_scr[:, pl.ds(start, TOK_BLK)], axis=1, keepdims=True)
        return jnp.where(lane == j, c, acc)

    counts = lax.fori_loop(0, nb, block_count, jnp.zeros((N_EXPERTS, LANES), F32))
    incl = counts
    shift = 1
    while shift < LANES:
        incl = incl + jnp.where(lane >= shift, pltpu.roll(incl, shift, 1), 0.0)
        shift *= 2
    offs_ref[...] = (incl - counts).astype(I32)
    cnt_ref[...] = counts.astype(I32)


def _select(aff_t, aff_t_meta, gidx, gidx_meta, *, cap, n_total):
    e, nr = aff_t.shape
    nb = nr // TOK_BLK
    assert nb <= LANES and nr % TOK_BLK == 0
    idx_bits = max(1, (n_total - 1).bit_length())
    shp = jax.ShapeDtypeStruct((e, LANES), I32)
    return pl.pallas_call(
        functools.partial(_select_body, cap=cap, nb=nb, idx_bits=idx_bits),
        out_shape=[shp, shp, shp, shp],
        scratch_shapes=[pltpu.VMEM((e, nr), F32)],
        compiler_params=pltpu.CompilerParams(vmem_limit_bytes=VMEM_LIMIT),
        name="ec_select",
    )(aff_t, aff_t_meta, gidx, gidx_meta)


def _gather_body(offs_ref, cnts_ref, aff_ref, tau_ref, cut_ref, h_ref, tri_ref, xe_ref, *, nb, t, l_total):
    e = pl.program_id(0)
    j = pl.program_id(1)

    @pl.when(j == 0)
    def _():
        xe_ref[...] = jnp.zeros_like(xe_ref)

    tau = tau_ref[0][:, 0:1]
    cut = cut_ref[0][:, 0:1]
    bits = pltpu.bitcast(aff_ref[0], I32)
    row0 = j * TOK_BLK
    gidx = row0 + (row0 // t) * (l_total - t) + (l_total - t) + lax.broadcasted_iota(I32, bits.shape, 1)
    sel = (bits > tau) | ((bits == tau) & (gidx <= cut))
    self32 = jnp.where(sel, 1.0, 0.0)
    incl = _dot(jnp.broadcast_to(self32, (8, TOK_BLK)).astype(BF16), tri_ref[...])[0:1]
    cnt = cnts_ref[e * nb + j]
    off = offs_ref[e * nb + j]
    off_al = (off // 16) * 16
    rel = jnp.where(sel, incl.astype(I32) - 1 + (off - off_al), -1)
    nchunk = (off - off_al + cnt + ROW_WIN - 1) // ROW_WIN
    hblk = h_ref[...]
    rowid = lax.broadcasted_iota(I32, (ROW_WIN, TOK_BLK), 0)

    def chunk(c, carry):
        onehot = jnp.where(rowid == rel - c * ROW_WIN, 1.0, 0.0).astype(BF16)
        g = _dot(onehot, hblk).astype(BF16)
        start = pl.multiple_of(off_al + c * ROW_WIN, 16)
        xe_ref[0, pl.ds(start, ROW_WIN), :] = xe_ref[0, pl.ds(start, ROW_WIN), :] + g
        return carry

    lax.fori_loop(0, nchunk, chunk, 0)


def _gather(offs_flat, cnts_flat, aff_t3, tau, cut, h2, tri_u, *, cap_pad, t, l_total):
    e = aff_t3.shape[0]
    nr, d = h2.shape
    nb = nr // TOK_BLK
    return pl.pallas_call(
        functools.partial(_gather_body, nb=nb, t=t, l_total=l_total),
        grid_spec=pltpu.PrefetchScalarGridSpec(
            num_scalar_prefetch=2,
            grid=(e, nb),
            in_specs=[
                pl.BlockSpec((1, 1, TOK_BLK), lambda ei, j, o, c: (ei, 0, j)),
                pl.BlockSpec((1, 1, LANES), lambda ei, j, o, c: (ei, 0, 0)),
                pl.BlockSpec((1, 1, LANES), lambda ei, j, o, c: (ei, 0, 0)),
                pl.BlockSpec((TOK_BLK, d), lambda ei, j, o, c: (j, 0)),
                pl.BlockSpec((TOK_BLK, TOK_BLK), lambda ei, j, o, c: (0, 0)),
            ],
            out_specs=pl.BlockSpec((1, cap_pad, d), lambda ei, j, o, c: (ei, 0, 0)),
        ),
        out_shape=jax.ShapeDtypeStruct((e, cap_pad, d), BF16),
        compiler_params=_cparams(("parallel", "arbitrary")),
        name="ec_gather",
    )(offs_flat, cnts_flat, aff_t3, tau, cut, h2, tri_u)


def _ffn_body(x_ref, wg_ref, wu_ref, wd_ref, o_ref, *, fc):
    x = x_ref[0]
    f = wg_ref.shape[2]
    acc = jnp.zeros((x.shape[0], wd_ref.shape[2]), F32)
    for c in range(f // fc):
        a = _dot(x, wg_ref[0, :, c * fc:(c + 1) * fc])
        b = _dot(x, wu_ref[0, :, c * fc:(c + 1) * fc])
        hmid = (a * jax.nn.sigmoid(a) * b).astype(BF16)
        acc = acc + _dot(hmid, wd_ref[0, c * fc:(c + 1) * fc, :])
    o_ref[0] = acc.astype(BF16)


def _ffn(xe, wg, wu, wd, *, tm):
    e, cap_pad, d = xe.shape
    f = wg.shape[2]
    return pl.pallas_call(
        functools.partial(_ffn_body, fc=min(512, f)),
        grid=(e, cap_pad // tm),
        in_specs=[
            pl.BlockSpec((1, tm, d), lambda ei, i: (ei, i, 0)),
            pl.BlockSpec((1, d, f), lambda ei, i: (ei, 0, 0)),
            pl.BlockSpec((1, d, f), lambda ei, i: (ei, 0, 0)),
            pl.BlockSpec((1, f, d), lambda ei, i: (ei, 0, 0)),
        ],
        out_specs=pl.BlockSpec((1, tm, d), lambda ei, i: (ei, i, 0)),
        out_shape=jax.ShapeDtypeStruct((e, cap_pad, d), BF16),
        compiler_params=_cparams(("parallel", "arbitrary")),
        name="ec_ffn",
    )(xe, wg, wu, wd)


def _combine_body(offs_ref, cnts_ref, aff_ref, tau_ref, cut_ref, x2_ref, tri_ref, fn_ref, ywin_ref, ye_hbm,
                  o_ref, rank_scr, gate_scr, acc_scr, ybuf, sem, *, nb, t, l_total):
    j = pl.program_id(0)
    e = pl.program_id(1)

    @pl.when(e == 0)
    def _():
        aff = aff_ref[...]
        bits = pltpu.bitcast(aff, I32)
        tau = tau_ref[0:1, :]
        cut = cut_ref[0:1, :]
        row0 = j * TOK_BLK
        gidx = row0 + (row0 // t) * (l_total - t) + (l_total - t) + lax.broadcasted_iota(I32, bits.shape, 0)
        sel = (bits > tau) | ((bits == tau) & (gidx <= cut))
        sel = sel & (lax.broadcasted_iota(I32, bits.shape, 1) < N_EXPERTS)
        self32 = jnp.where(sel, 1.0, 0.0)
        incl = _dot(tri_ref[...], self32.astype(BF16))
        rank_scr[...] = jnp.where(sel, incl - 1.0, -1.0)
        gate_scr[...] = jnp.where(sel, aff, 0.0)
        acc_scr[...] = x2_ref[...]

    lane = lax.broadcasted_iota(I32, (TOK_BLK, LANES), 1)
    pick = lane == e
    rank = jnp.sum(jnp.where(pick, rank_scr[...], 0.0), axis=1, keepdims=True).astype(I32)
    gate = jnp.sum(jnp.where(pick, gate_scr[...], 0.0), axis=1, keepdims=True)
    cnt = cnts_ref[e * nb + j]
    off = offs_ref[e * nb + j]
    off_al = (off // 16) * 16
    rel = jnp.where(rank >= 0, rank + (off - off_al), -1)
    nchunk = (off - off_al + cnt + ROW_WIN - 1) // ROW_WIN
    colid = lax.broadcasted_iota(I32, (TOK_BLK, ROW_WIN), 1)

    onehot = jnp.where(colid == rel, 1.0, 0.0).astype(BF16)
    acc_scr[...] += _dot(onehot, ywin_ref[...]) * gate

    def chunk(c, carry):
        start = pl.multiple_of(off_al + c * ROW_WIN, 16)
        cp = pltpu.make_async_copy(ye_hbm.at[e, pl.ds(start, ROW_WIN), :], ybuf, sem)
        cp.start()
        cp.wait()
        oh = jnp.where(colid == rel - c * ROW_WIN, 1.0, 0.0).astype(BF16)
        acc_scr[...] += _dot(oh, ybuf[...]) * gate
        return carry

    lax.fori_loop(1, nchunk, chunk, 0)

    @pl.when(e == pl.num_programs(1) - 1)
    def _():
        y = acc_scr[...]
        ms = jnp.mean(y * y, axis=-1, keepdims=True)
        o_ref[...] = y * lax.rsqrt(ms + RMS_EPS) * fn_ref[...]


def _combine(offs_flat, cnts_flat, aff, tau_row, cut_row, x2, tri_l, fnorm, ye, *, t, l_total):
    nr, d = x2.shape
    nb = nr // TOK_BLK
    e = ye.shape[0]

    def win_map(j, ei, o, c):
        return (ei, (o[ei * nb + j] // 16) * 16, 0)

    return pl.pallas_call(
        functools.partial(_combine_body, nb=nb, t=t, l_total=l_total),
        grid_spec=pltpu.PrefetchScalarGridSpec(
            num_scalar_prefetch=2,
            grid=(nb, e),
            in_specs=[
                pl.BlockSpec((TOK_BLK, LANES), lambda j, ei, o, c: (j, 0)),
                pl.BlockSpec((8, LANES), lambda j, ei, o, c: (0, 0)),
                pl.BlockSpec((8, LANES), lambda j, ei, o, c: (0, 0)),
                pl.BlockSpec((TOK_BLK, d), lambda j, ei, o, c: (j, 0)),
                pl.BlockSpec((TOK_BLK, TOK_BLK), lambda j, ei, o, c: (0, 0)),
                pl.BlockSpec((1, d), lambda j, ei, o, c: (0, 0)),
                pl.BlockSpec((pl.Squeezed(), pl.Element(ROW_WIN), pl.Element(d)), win_map),
                pl.BlockSpec(memory_space=pl.ANY),
            ],
            out_specs=pl.BlockSpec((TOK_BLK, d), lambda j, ei, o, c: (j, 0)),
            scratch_shapes=[
                pltpu.VMEM((TOK_BLK, LANES), F32),
                pltpu.VMEM((TOK_BLK, LANES), F32),
                pltpu.VMEM((TOK_BLK, d), F32),
                pltpu.VMEM((ROW_WIN, d), BF16),
                pltpu.SemaphoreType.DMA(()),
            ],
        ),
        out_shape=jax.ShapeDtypeStruct((nr, d), F32),
        compiler_params=_cparams(("parallel", "arbitrary")),
        name="ec_combine",
    )(offs_flat, cnts_flat, aff, tau_row, cut_row, x2, tri_l, fnorm, ye, ye)


def _rope_tables(positions):
    inv_freq = 1.0 / (ROPE_THETA ** (jnp.arange(0, HEAD_DIM, 2, dtype=F32) / HEAD_DIM))
    ang = positions.astype(F32)[:, None] * inv_freq[None, :]
    cos, sin = jnp.cos(ang), jnp.sin(ang)
    return (jnp.concatenate([cos, cos, cos, cos], axis=1),
            jnp.concatenate([-sin, -sin, sin, sin], axis=1))


def _permute_in_weights(w_in):
    l = jnp.arange(LANES)
    g = l // 32
    within = (g % 2) * HEAD_DIM + (g // 2) * 32 + l % 32
    cols = jnp.arange(PROJ_WIDTH)
    blk = cols // LANES
    permuted = blk * LANES + within[cols % LANES]
    cols = jnp.where((blk >= DQ_BLK) & (blk < DV_BLK), permuted, cols)
    return w_in[:, cols]


def _encode_group(x, shared):
    b, t, d = x.shape
    l_total = N_META + t
    n_total = b * l_total
    cap = EC_CAPACITY_FACTOR * n_total // N_EXPERTS
    rows = b * t
    xf = x.reshape(rows, d)

    cos, sin = _rope_tables(N_META + jnp.arange(t))
    proj = _inproj(xf, shared["norm_mix"], shared["w_in"], cos, sin, tm=512).reshape(b, t, PROJ_WIDTH)
    proj_meta = shared["proj_meta"]

    na = _na_attention(proj, proj_meta, shared["na_table"], shared["mb_stack"])
    lam = shared["lambda"]
    tk = min(1024, t)
    df = _diff_attention(proj, proj, proj_meta, *lam, shared["subln"], tq=min(1024, t), tk=tk, q_shared=False)
    df_meta = _diff_attention(proj_meta[None], proj, proj_meta, *lam, shared["subln"],
                              tq=N_META, tk=tk, q_shared=True)

    wr = (shared["wr_hi"], shared["wr_lo"])
    x2, h2, aff_t, aff = _outproj(xf, na.reshape(rows, NA_WIDTH), df.reshape(rows, DIFF_WIDTH), shared["w_out"],
                                  shared["norm_ffn"], *wr, tm=512, token_major=True)
    xm = jnp.tile(shared["meta_tokens"], (b, 1))
    nam = jnp.tile(shared["na_meta"], (b, 1))
    _, _, aff_t_meta = _outproj(xm, nam, df_meta.reshape(b * N_META, DIFF_WIDTH), shared["w_out"],
                                shared["norm_ffn"], *wr, tm=b * N_META, token_major=False)

    r = jnp.arange(rows, dtype=I32)
    gidx = (r + (r // t + 1) * N_META)[None, :]
    rm = jnp.arange(b * N_META, dtype=I32)
    gidx_meta = ((rm // N_META) * l_total + rm % N_META)[None, :]
    tau, cut, offs, cnts = _select(aff_t, aff_t_meta, gidx, gidx_meta, cap=cap, n_total=n_total)

    nb = rows // TOK_BLK
    offs_flat = offs[:, :nb].reshape(-1)
    cnts_flat = cnts[:, :nb].reshape(-1)
    cap_pad = -(-(cap + ROW_WIN) // 256) * 256
    ffn_tm = max(tm for tm in range(16, 1025, 16) if cap_pad % tm == 0)
    xe = _gather(offs_flat, cnts_flat, aff_t.reshape(N_EXPERTS, 1, rows), tau.reshape(N_EXPERTS, 1, LANES),
                 cut.reshape(N_EXPERTS, 1, LANES), h2, shared["tri_u"], cap_pad=cap_pad, t=t, l_total=l_total)
    ye = _ffn(xe, shared["w_gate"], shared["w_up"], shared["w_down"], tm=ffn_tm)

    pad = jnp.zeros((8, LANES - N_EXPERTS), I32)
    tau_row = jnp.concatenate([jnp.broadcast_to(tau[:, 0][None, :], (8, N_EXPERTS)), pad], axis=1)
    cut_row = jnp.concatenate([jnp.broadcast_to(cut[:, 0][None, :], (8, N_EXPERTS)), pad], axis=1)
    y = _combine(offs_flat, cnts_flat, aff, tau_row, cut_row, x2, shared["tri_l"], shared["final_norm"], ye,
                 t=t, l_total=l_total)
    return y.reshape(b, t, d)


def kernel(x_prompt, x_sample, meta_tokens, norm_mix, w_in, na_rpb, na_meta_bias, lambda_q1, lambda_k1,
           lambda_q2, lambda_k2, diff_subln, w_out, norm_ffn, w_router, w_gate, w_up, w_down, final_norm):
    d = x_prompt.shape[-1]
    wr = jnp.zeros((LANES, d), F32).at[:N_EXPERTS].set(w_router[0].T)
    wr_hi = wr.astype(BF16)
    mb = na_meta_bias[0].astype(F32)
    idx = jnp.arange(TOK_BLK)
    shared = {
        "meta_tokens": meta_tokens,
        "norm_mix": norm_mix[0][None, :],
        "w_in": _permute_in_weights(w_in[0]).astype(BF16),
        "lambda": (lambda_q1, lambda_k1, lambda_q2, lambda_k2),
        "subln": diff_subln,
        "w_out": w_out[0].astype(BF16),
        "norm_ffn": norm_ffn[0][None, :],
        "wr_hi": wr_hi,
        "wr_lo": (wr - wr_hi.astype(F32)).astype(BF16),
        "w_gate": w_gate[0].astype(BF16),
        "w_up": w_up[0].astype(BF16),
        "w_down": w_down[0].astype(BF16),
        "final_norm": final_norm[None, :],
        "mb_stack": jnp.repeat(mb, GRID_W, axis=0).reshape(NA_HEADS // 2, 2 * GRID_W, N_META),
        "tri_u": (idx[:, None] <= idx[None, :]).astype(BF16),
        "tri_l": (idx[:, None] >= idx[None, :]).astype(BF16),
    }
    cos_m, sin_m = _rope_tables(jnp.arange(N_META))
    shared["proj_meta"] = _inproj(meta_tokens, shared["norm_mix"], shared["w_in"], cos_m, sin_m, tm=N_META)
    mb_meta = jnp.repeat(mb, N_META, axis=0).reshape(NA_HEADS // 2, 2 * N_META, N_META)
    shared["na_meta"] = _na_meta(shared["proj_meta"], mb_meta)
    shared["na_table"] = _na_bias(na_rpb[0].astype(F32))
    return (_encode_group(x_prompt, shared), _encode_group(x_sample, shared))
```

```python
import functools
import math

import jax
import jax.numpy as jnp
from jax import lax
from jax.experimental import pallas as pl
from jax.experimental.pallas import tpu as pltpu

BF16 = jnp.bfloat16
F32 = jnp.float32
I32 = jnp.int32

N_META = 16
GRID_W = 64
HEAD_DIM = 64
NA_HEADS = 8
NA_WIDTH = NA_HEADS * HEAD_DIM
NA_KH = 8
NA_KW = 16
DIFF_HEADS = 4
DIFF_WIDTH = DIFF_HEADS * 2 * HEAD_DIM
PROJ_WIDTH = 3 * NA_WIDTH + 3 * DIFF_WIDTH
ROPE_THETA = 10000.0
N_EXPERTS = 16
EC_CAPACITY_FACTOR = 2
RMS_EPS = 1e-6
SUBLN_EPS = 1e-5
LAMBDA_INIT = 0.8 - 0.6 * math.exp(-0.3 * 0)

LANES = 128
NEG_BIG = -1e30
VMEM_LIMIT = 56 * 1024 * 1024

NAQ_BLK, NAK_BLK, NAV_BLK = 0, 4, 8
DQ_BLK, DK_BLK, DV_BLK = 12, 16, 20

ROW_WIN = 256
TOK_BLK = 1024


def _cparams(sem, vmem=VMEM_LIMIT):
    return pltpu.CompilerParams(dimension_semantics=sem, vmem_limit_bytes=vmem)


def _dot(a, b):
    return jnp.dot(a, b, preferred_element_type=F32)


def _dot_nt(a, b):
    return lax.dot_general(a, b, (((1,), (1,)), ((), ())), preferred_element_type=F32)


def _inproj_body(x_ref, g_ref, w_ref, cos_ref, sin_ref, o_ref, *, tn):
    x = x_ref[...]
    ms = jnp.mean(x * x, axis=-1, keepdims=True)
    h = (x * lax.rsqrt(ms + RMS_EPS) * g_ref[...]).astype(BF16)
    cos = cos_ref[...]
    sin = sin_ref[...]
    for c in range(PROJ_WIDTH // tn):
        lo = c * tn
        acc = _dot(h, w_ref[:, lo:lo + tn])
        for s in range(tn // LANES):
            blk = (lo + s * LANES) // LANES
            a = acc[:, s * LANES:(s + 1) * LANES]
            if DQ_BLK <= blk < DV_BLK:
                a = a * cos + pltpu.roll(a, 64, 1) * sin
            if blk < NAK_BLK:
                a = a * (HEAD_DIM ** -0.5)
            if DQ_BLK <= blk < DK_BLK:
                a = a * (HEAD_DIM ** -0.5 * math.log2(math.e))
            o_ref[:, lo + s * LANES:lo + (s + 1) * LANES] = a.astype(BF16)


def _inproj(x, g, w, cos, sin, *, tm):
    rows, d = x.shape
    nt = cos.shape[0] // tm
    return pl.pallas_call(
        functools.partial(_inproj_body, tn=512),
        grid=(rows // tm,),
        in_specs=[
            pl.BlockSpec((tm, d), lambda i: (i, 0)),
            pl.BlockSpec((1, d), lambda i: (0, 0)),
            pl.BlockSpec((d, PROJ_WIDTH), lambda i: (0, 0)),
            pl.BlockSpec((tm, LANES), lambda i: (i % nt, 0)),
            pl.BlockSpec((tm, LANES), lambda i: (i % nt, 0)),
        ],
        out_specs=pl.BlockSpec((tm, PROJ_WIDTH), lambda i: (i, 0)),
        out_shape=jax.ShapeDtypeStruct((rows, PROJ_WIDTH), BF16),
        compiler_params=_cparams(("parallel",)),
        name="inproj",
    )(x, g, w, cos, sin)


def _na_bias_body(rpb_ref, o_ref):
    h = pl.program_id(0)
    c = lax.broadcasted_iota(I32, (GRID_W, GRID_W), 0)
    kc = lax.broadcasted_iota(I32, (GRID_W, GRID_W), 1)
    cs = jnp.clip(c - NA_KW // 2, 0, GRID_W - NA_KW)
    valid = (kc >= cs) & (kc < cs + NA_KW)
    jm = kc - c + NA_KW - 1
    n_dr = 2 * NA_KH - 1
    n_j = 2 * NA_KW - 1
    tiles = []
    for dr in range(n_dr):
        base = (h * n_dr + dr) * n_j

        def body(j, acc, base=base):
            return jnp.where(jm == j, rpb_ref[base + j], acc)

        t = lax.fori_loop(0, n_j, body, jnp.zeros((GRID_W, GRID_W), F32))
        tiles.append(jnp.where(valid, t, NEG_BIG))
    for d in range(NA_KH):
        o_ref[0, d] = jnp.concatenate([tiles[kr - d + NA_KH - 1] for kr in range(NA_KH)], axis=1)


def _na_bias(rpb):
    h = rpb.shape[0]
    return pl.pallas_call(
        _na_bias_body,
        grid_spec=pltpu.PrefetchScalarGridSpec(
            num_scalar_prefetch=1,
            grid=(h,),
            in_specs=[],
            out_specs=pl.BlockSpec((1, NA_KH, GRID_W, NA_KH * GRID_W), lambda i, r: (i, 0, 0, 0)),
        ),
        out_shape=jax.ShapeDtypeStruct((h, NA_KH, GRID_W, NA_KH * GRID_W), F32),
        compiler_params=_cparams(("arbitrary",)),
        name="na_bias",
    )(rpb.reshape(-1))


def _split_heads(q):
    lane = lax.broadcasted_iota(I32, q.shape, 1)
    zero = jnp.zeros_like(q)
    return jnp.concatenate([jnp.where(lane < HEAD_DIM, q, zero), jnp.where(lane < HEAD_DIM, zero, q)], axis=0)


def _merge_heads(o):
    n = o.shape[0] // 2
    lane = lax.broadcasted_iota(I32, (n, LANES), 1)
    return jnp.where(lane < HEAD_DIM, o[:n], o[n:])


def _na_body(q_ref, kp_ref, kc_ref, kn_ref, vp_ref, vc_ref, vn_ref, km_ref, vm_ref, tab_ref, mb_ref,
             o_ref, kbuf, vbuf, qq_scr, s_scr, p_scr, *, rows):
    rb = pl.program_id(2)
    blk = NA_KH * GRID_W
    kbuf[0:blk] = kp_ref[0]
    kbuf[blk:2 * blk] = kc_ref[0]
    kbuf[2 * blk:3 * blk] = kn_ref[0]
    vbuf[0:blk] = vp_ref[0]
    vbuf[blk:2 * blk] = vc_ref[0]
    vbuf[2 * blk:3 * blk] = vn_ref[0]
    two = 2 * GRID_W
    starts = []
    for i in range(NA_KH):
        r = rb * NA_KH + i
        rs = jnp.clip(r - NA_KH // 2, 0, rows - NA_KH)
        d = r - rs
        start = pl.multiple_of((rs - (rb - 1) * NA_KH) * GRID_W, GRID_W)
        starts.append(start)
        qq = _split_heads(q_ref[0, i * GRID_W:(i + 1) * GRID_W, :])
        qq_scr[i * two:(i + 1) * two, :] = qq
        bias = jnp.concatenate([tab_ref[0, d], tab_ref[1, d]], axis=0)
        s_scr[i * two:(i + 1) * two, :] = _dot_nt(qq, kbuf[pl.ds(start, blk), :]) + bias
    s = s_scr[...]
    sm = (_dot_nt(qq_scr[...], km_ref[...]).reshape(NA_KH, two, N_META) + mb_ref[0][None]).reshape(NA_KH * two, N_META)
    m = jnp.maximum(jnp.max(s, axis=1, keepdims=True), jnp.max(sm, axis=1, keepdims=True))
    p = jnp.exp(s - m)
    pm = jnp.exp(sm - m)
    l = jnp.sum(p, axis=1, keepdims=True) + jnp.sum(pm, axis=1, keepdims=True)
    p_scr[...] = p.astype(BF16)
    om = _dot(pm.astype(BF16), vm_ref[...])
    inv_l = 1.0 / l
    for i in range(NA_KH):
        rsl = slice(i * two, (i + 1) * two)
        o = (_dot(p_scr[rsl, :], vbuf[pl.ds(starts[i], blk), :]) + om[rsl]) * inv_l[rsl]
        o_ref[0, i * GRID_W:(i + 1) * GRID_W, :] = _merge_heads(o).astype(BF16)


def _na_attention(proj, proj_meta, table, mb_stack):
    b, t, _ = proj.shape
    rows = t // GRID_W
    nrb = rows // NA_KH
    blk = NA_KH * GRID_W
    hp = NA_HEADS // 2

    def kv_spec(col, shift):
        return pl.BlockSpec((1, blk, LANES),
                            lambda h, bi, rb: (bi, jnp.clip(rb + shift, 0, nrb - 1), col + h))

    return pl.pallas_call(
        functools.partial(_na_body, rows=rows),
        grid=(hp, b, nrb),
        in_specs=[
            pl.BlockSpec((1, blk, LANES), lambda h, bi, rb: (bi, rb, NAQ_BLK + h)),
            kv_spec(NAK_BLK, -1), kv_spec(NAK_BLK, 0), kv_spec(NAK_BLK, 1),
            kv_spec(NAV_BLK, -1), kv_spec(NAV_BLK, 0), kv_spec(NAV_BLK, 1),
            pl.BlockSpec((N_META, LANES), lambda h, bi, rb: (0, NAK_BLK + h)),
            pl.BlockSpec((N_META, LANES), lambda h, bi, rb: (0, NAV_BLK + h)),
            pl.BlockSpec((2, NA_KH, GRID_W, blk), lambda h, bi, rb: (h, 0, 0, 0)),
            pl.BlockSpec((1, LANES, N_META), lambda h, bi, rb: (h, 0, 0)),
        ],
        out_specs=pl.BlockSpec((1, blk, LANES), lambda h, bi, rb: (bi, rb, h)),
        out_shape=jax.ShapeDtypeStruct((b, t, NA_WIDTH), BF16),
        scratch_shapes=[pltpu.VMEM((3 * blk, LANES), BF16), pltpu.VMEM((3 * blk, LANES), BF16),
                        pltpu.VMEM((2 * blk, LANES), BF16), pltpu.VMEM((2 * blk, blk), F32),
                        pltpu.VMEM((2 * blk, blk), BF16)],
        compiler_params=_cparams(("arbitrary", "arbitrary", "arbitrary")),
        name="na_attention",
    )(proj, proj, proj, proj, proj, proj, proj, proj_meta, proj_meta, table, mb_stack)


def _na_meta_body(p_ref, mb_ref, o_ref):
    for h in range(NA_HEADS // 2):
        qq = _split_heads(p_ref[:, (NAQ_BLK + h) * LANES:(NAQ_BLK + h + 1) * LANES])
        km = p_ref[:, (NAK_BLK + h) * LANES:(NAK_BLK + h + 1) * LANES]
        vm = p_ref[:, (NAV_BLK + h) * LANES:(NAV_BLK + h + 1) * LANES]
        s = _dot_nt(qq, km) + mb_ref[h, 0:2 * N_META, :]
        m = jnp.max(s, axis=1, keepdims=True)
        p = jnp.exp(s - m)
        l = jnp.sum(p, axis=1, keepdims=True)
        o = _dot(p.astype(BF16), vm) / l
        o_ref[:, h * LANES:(h + 1) * LANES] = _merge_heads(o).astype(BF16)


def _na_meta(proj_meta, mb_meta):
    return pl.pallas_call(
        _na_meta_body,
        out_shape=jax.ShapeDtypeStruct((N_META, NA_WIDTH), BF16),
        name="na_meta",
    )(proj_meta, mb_meta)


def _split_maps(q):
    lane = lax.broadcasted_iota(I32, q.shape, 1)
    is0 = ((lane >> 5) & 1) == 0
    zero = jnp.zeros_like(q)
    return jnp.concatenate([jnp.where(is0, q, zero), jnp.where(is0, zero, q)], axis=0)


def _online_softmax(s_ref, p_ref, a_ref, m_scr, rows):
    s = s_ref[rows, :]
    m_prev = m_scr[rows]
    m_new = jnp.maximum(m_prev, jnp.max(s, axis=1, keepdims=True))
    a_ref[rows] = jnp.exp2(m_prev - m_new)
    m_scr[rows] = m_new
    p_ref[rows, :] = jnp.exp2((s - m_new).astype(BF16))


def _diff_body(q_ref, k_ref, kn_ref, vp_ref, v_ref, km_ref, vm_ref, lq1_ref, lk1_ref, lq2_ref, lk2_ref, sub_ref,
               o_ref, qq_scr, vx_scr, s0, s1, p0, p1, a0, a1, m_scr, acc_scr, *, tq, rc, nk):
    ki = pl.program_id(3)

    @pl.when(ki == 0)
    def _():
        qq = _split_maps(q_ref[0])
        qq_scr[...] = qq
        s = _dot_nt(qq, km_ref[...])
        m = jnp.max(s, axis=1, keepdims=True)
        p = jnp.exp2((s - m).astype(BF16))
        m_scr[...] = m
        vx_scr[:, LANES:] = jnp.ones((vx_scr.shape[0], LANES), BF16)
        vx_scr[0:N_META, 0:LANES] = vm_ref[...]
        acc_scr[...] = _dot(p, vx_scr[0:N_META, :])
        s0[...] = _dot_nt(qq, k_ref[0])
        p1[...] = jnp.zeros_like(p1)
        a1[...] = jnp.ones_like(a1)

    def step(s_cur, s_nxt, p_cur, p_prv, a_cur, a_prv):
        vx_scr[:, 0:LANES] = vp_ref[0]
        for g in range(2 * tq // rc):
            rows = slice(g * rc, (g + 1) * rc)
            s_nxt[rows, :] = _dot_nt(qq_scr[rows, :], kn_ref[0])
            acc_scr[rows, :] = a_prv[rows] * acc_scr[rows, :] + _dot(p_prv[rows, :], vx_scr[...])
            _online_softmax(s_cur, p_cur, a_cur, m_scr, rows)

    @pl.when(ki % 2 == 0)
    def _():
        step(s0, s1, p0, p1, a0, a1)

    @pl.when(ki % 2 == 1)
    def _():
        step(s1, s0, p1, p0, a1, a0)

    @pl.when(ki == nk - 1)
    def _():
        p_last, a_last = (p0, a0) if (nk - 1) % 2 == 0 else (p1, a1)
        vx_scr[:, 0:LANES] = v_ref[0]
        acc = a_last[...] * acc_scr[...] + _dot(p_last[...], vx_scr[...])
        lam = (jnp.exp(jnp.sum(lq1_ref[...] * lk1_ref[...], axis=1, keepdims=True))
               - jnp.exp(jnp.sum(lq2_ref[...] * lk2_ref[...], axis=1, keepdims=True)) + LAMBDA_INIT)
        o_all = acc[:, 0:LANES] / acc[:, LANES:]
        o = o_all[:tq] - lam * o_all[tq:]
        ms = jnp.mean(o * o, axis=-1, keepdims=True)
        o = o * lax.rsqrt(ms + SUBLN_EPS) * sub_ref[...] * (1.0 - LAMBDA_INIT)
        o_ref[0] = o.astype(BF16)


def _diff_attention(q_src, proj, proj_meta, lq1, lk1, lq2, lk2, subln, *, tq, tk, q_shared):
    b, t, _ = proj.shape
    tq_total = q_src.shape[1]
    nq = tq_total // tq
    nk = t // tk
    if q_shared:
        q_map = lambda bi, h, qi, ki: (0, qi, DQ_BLK + h)
    else:
        q_map = lambda bi, h, qi, ki: (bi, qi, DQ_BLK + h)
    vec = lambda n: pl.BlockSpec((1, n), lambda bi, h, qi, ki: (0, 0))
    return pl.pallas_call(
        functools.partial(_diff_body, tq=tq, rc=min(512, 2 * tq), nk=nk),
        grid=(b, DIFF_HEADS, nq, nk),
        in_specs=[
            pl.BlockSpec((1, tq, LANES), q_map),
            pl.BlockSpec((1, tk, LANES), lambda bi, h, qi, ki: (bi, ki, DK_BLK + h)),
            pl.BlockSpec((1, tk, LANES), lambda bi, h, qi, ki: (bi, jnp.minimum(ki + 1, nk - 1), DK_BLK + h)),
            pl.BlockSpec((1, tk, LANES), lambda bi, h, qi, ki: (bi, jnp.maximum(ki - 1, 0), DV_BLK + h)),
            pl.BlockSpec((1, tk, LANES), lambda bi, h, qi, ki: (bi, ki, DV_BLK + h)),
            pl.BlockSpec((N_META, LANES), lambda bi, h, qi, ki: (0, DK_BLK + h)),
            pl.BlockSpec((N_META, LANES), lambda bi, h, qi, ki: (0, DV_BLK + h)),
            vec(HEAD_DIM), vec(HEAD_DIM), vec(HEAD_DIM), vec(HEAD_DIM), vec(2 * HEAD_DIM),
        ],
        out_specs=pl.BlockSpec((1, tq, LANES), lambda bi, h, qi, ki: (bi, qi, h)),
        out_shape=jax.ShapeDtypeStruct((b, tq_total, DIFF_WIDTH), BF16),
        scratch_shapes=[
            pltpu.VMEM((2 * tq, LANES), BF16),
            pltpu.VMEM((tk, 2 * LANES), BF16),
            pltpu.VMEM((2 * tq, tk), F32), pltpu.VMEM((2 * tq, tk), F32),
            pltpu.VMEM((2 * tq, tk), BF16), pltpu.VMEM((2 * tq, tk), BF16),
            pltpu.VMEM((2 * tq, 1), F32), pltpu.VMEM((2 * tq, 1), F32),
            pltpu.VMEM((2 * tq, 1), F32),
            pltpu.VMEM((2 * tq, 2 * LANES), F32),
        ],
        compiler_params=_cparams(("parallel", "parallel", "parallel", "arbitrary")),
        name="diff_attention",
    )(q_src, proj, proj, proj, proj, proj_meta, proj_meta, lq1, lk1, lq2, lk2, subln)


def _outproj_body(x_ref, na_ref, df_ref, wo_ref, g_ref, wrh_ref, wrl_ref, x2_ref, h2_ref, afft_ref, *aff_ref):
    x2 = x_ref[...] + _dot(na_ref[...], wo_ref[0:NA_WIDTH, :]) + _dot(df_ref[...], wo_ref[NA_WIDTH:, :])
    x2_ref[...] = x2
    ms = jnp.mean(x2 * x2, axis=-1, keepdims=True)
    hf = x2 * lax.rsqrt(ms + RMS_EPS) * g_ref[...]
    hi = hf.astype(BF16)
    h2_ref[...] = hi
    lo = (hf - hi.astype(F32)).astype(BF16)
    logits = _dot_nt(wrh_ref[...], hi) + _dot_nt(wrh_ref[...], lo) + _dot_nt(wrl_ref[...], hi)
    row = lax.broadcasted_iota(I32, logits.shape, 0)
    logits = jnp.where(row < N_EXPERTS, logits, NEG_BIG)
    m = jnp.max(logits, axis=0, keepdims=True)
    e = jnp.exp(logits - m)
    aff = e / jnp.sum(e, axis=0, keepdims=True)
    afft_ref[...] = aff[0:N_EXPERTS]
    if aff_ref:
        aff_ref[0][...] = aff.T


def _outproj(x, na, df, wo, g, wrh, wrl, *, tm, token_major):
    rows, d = x.shape
    out_shape = [jax.ShapeDtypeStruct((rows, d), F32), jax.ShapeDtypeStruct((rows, d), BF16),
                 jax.ShapeDtypeStruct((N_EXPERTS, rows), F32)]
    out_specs = [pl.BlockSpec((tm, d), lambda i: (i, 0)), pl.BlockSpec((tm, d), lambda i: (i, 0)),
                 pl.BlockSpec((N_EXPERTS, tm), lambda i: (0, i))]
    if token_major:
        out_shape.append(jax.ShapeDtypeStruct((rows, LANES), F32))
        out_specs.append(pl.BlockSpec((tm, LANES), lambda i: (i, 0)))
    return pl.pallas_call(
        _outproj_body,
        grid=(rows // tm,),
        in_specs=[
            pl.BlockSpec((tm, d), lambda i: (i, 0)),
            pl.BlockSpec((tm, NA_WIDTH), lambda i: (i, 0)),
            pl.BlockSpec((tm, DIFF_WIDTH), lambda i: (i, 0)),
            pl.BlockSpec((NA_WIDTH + DIFF_WIDTH, d), lambda i: (0, 0)),
            pl.BlockSpec((1, d), lambda i: (0, 0)),
            pl.BlockSpec((LANES, d), lambda i: (0, 0)),
            pl.BlockSpec((LANES, d), lambda i: (0, 0)),
        ],
        out_specs=out_specs,
        out_shape=out_shape,
        compiler_params=_cparams(("parallel",)),
        name="outproj_router",
    )(x, na, df, wo, g, wrh, wrl)


def _select_body(ar_ref, am_ref, gr_ref, gm_ref, tri_ref, tau_ref, cut_ref, offs_ref, cnt_ref, rank_ref, sel_scr,
                 *, cap, nb, idx_bits):
    br = pltpu.bitcast(ar_ref[...], I32)
    bm = pltpu.bitcast(am_ref[...], I32)
    gr = gr_ref[...]
    gm = gm_ref[...]

    def count(mr, mm):
        return (jnp.sum(jnp.where(mr, 1.0, 0.0), axis=1, keepdims=True)
                + jnp.sum(jnp.where(mm, 1.0, 0.0), axis=1, keepdims=True))

    def value_bit(i, ans):
        cand = ans | jnp.left_shift(jnp.int32(1), 30 - i)
        return jnp.where(count(br >= cand, bm >= cand) >= cap, cand, ans)

    tau = lax.fori_loop(0, 31, value_bit, jnp.zeros((N_EXPERTS, 1), I32))
    need = cap - count(br > tau, bm > tau)
    eq_r = br == tau
    eq_m = bm == tau

    def index_bit(i, ans):
        cand = ans | jnp.left_shift(jnp.int32(1), idx_bits - 1 - i)
        return jnp.where(count(eq_r & (gr < cand), eq_m & (gm < cand)) < need, cand, ans)

    cut = lax.fori_loop(0, idx_bits, index_bit, jnp.zeros((N_EXPERTS, 1), I32))
    tau_ref[...] = jnp.broadcast_to(tau, tau_ref.shape)
    cut_ref[...] = jnp.broadcast_to(cut, cut_ref.shape)

    sel_scr[...] = jnp.where((br > tau) | (eq_r & (gr <= cut)), 1.0, 0.0)
    lane = lax.broadcasted_iota(I32, (N_EXPERTS, LANES), 1)

    def block_count(j, acc):
        start = pl.multiple_of(j * TOK_BLK, TOK_BLK)
        sel = sel_scr[:, pl.ds(start, TOK_BLK)]
        rank_ref[:, pl.ds(start, TOK_BLK)] = sel * _dot(sel.astype(BF16), tri_ref[...])
        c = jnp.sum(sel, axis=1, keepdims=True)
        return jnp.where(lane == j, c, acc)

    counts = lax.fori_loop(0, nb, block_count, jnp.zeros((N_EXPERTS, LANES), F32))
    incl = counts
    shift = 1
    while shift < LANES:
        incl = incl + jnp.where(lane >= shift, pltpu.roll(incl, shift, 1), 0.0)
        shift *= 2
    offs_ref[...] = (incl - counts).astype(I32)
    cnt_ref[...] = counts.astype(I32)


def _select(aff_t, aff_t_meta, gidx, gidx_meta, tri_u, *, cap, n_total):
    e, nr = aff_t.shape
    nb = nr // TOK_BLK
    assert nb <= LANES and nr % TOK_BLK == 0
    idx_bits = max(1, (n_total - 1).bit_length())
    shp = jax.ShapeDtypeStruct((e, LANES), I32)
    return pl.pallas_call(
        functools.partial(_select_body, cap=cap, nb=nb, idx_bits=idx_bits),
        out_shape=[shp, shp, shp, shp, jax.ShapeDtypeStruct((e, nr), F32)],
        scratch_shapes=[pltpu.VMEM((e, nr), F32)],
        compiler_params=pltpu.CompilerParams(vmem_limit_bytes=VMEM_LIMIT),
        name="ec_select",
    )(aff_t, aff_t_meta, gidx, gidx_meta, tri_u)


def _gather_body(offs_ref, cnts_ref, rank_ref, h_ref, xe_ref, *, nb):
    e = pl.program_id(0)
    j = pl.program_id(1)

    @pl.when(j == 0)
    def _():
        xe_ref[...] = jnp.zeros_like(xe_ref)

    rank = rank_ref[0].astype(I32)
    cnt = cnts_ref[e * nb + j]
    off = offs_ref[e * nb + j]
    off_al = (off // 16) * 16
    rel = jnp.where(rank > 0, rank - 1 + (off - off_al), -1)
    nchunk = (off - off_al + cnt + ROW_WIN - 1) // ROW_WIN
    hblk = h_ref[...]
    rowid = lax.broadcasted_iota(I32, (ROW_WIN, TOK_BLK), 0)

    def chunk(c, carry):
        onehot = jnp.where(rowid == rel - c * ROW_WIN, 1.0, 0.0).astype(BF16)
        g = _dot(onehot, hblk).astype(BF16)
        start = pl.multiple_of(off_al + c * ROW_WIN, 16)
        xe_ref[0, pl.ds(start, ROW_WIN), :] = xe_ref[0, pl.ds(start, ROW_WIN), :] + g
        return carry

    lax.fori_loop(0, nchunk, chunk, 0)


def _gather(offs_flat, cnts_flat, rank3, h2, *, cap_pad):
    e = rank3.shape[0]
    nr, d = h2.shape
    nb = nr // TOK_BLK
    return pl.pallas_call(
        functools.partial(_gather_body, nb=nb),
        grid_spec=pltpu.PrefetchScalarGridSpec(
            num_scalar_prefetch=2,
            grid=(e, nb),
            in_specs=[
                pl.BlockSpec((1, 1, TOK_BLK), lambda ei, j, o, c: (ei, 0, j)),
                pl.BlockSpec((TOK_BLK, d), lambda ei, j, o, c: (j, 0)),
            ],
            out_specs=pl.BlockSpec((1, cap_pad, d), lambda ei, j, o, c: (ei, 0, 0)),
        ),
        out_shape=jax.ShapeDtypeStruct((e, cap_pad, d), BF16),
        compiler_params=_cparams(("parallel", "arbitrary")),
        name="ec_gather",
    )(offs_flat, cnts_flat, rank3, h2)


def _ffn_body(x_ref, wg_ref, wu_ref, wd_ref, o_ref, *, fc):
    x = x_ref[0]
    f = wg_ref.shape[2]
    acc = jnp.zeros((x.shape[0], wd_ref.shape[2]), F32)
    for c in range(f // fc):
        a = _dot(x, wg_ref[0, :, c * fc:(c + 1) * fc])
        b = _dot(x, wu_ref[0, :, c * fc:(c + 1) * fc])
        hmid = (a * jax.nn.sigmoid(a) * b).astype(BF16)
        acc = acc + _dot(hmid, wd_ref[0, c * fc:(c + 1) * fc, :])
    o_ref[0] = acc.astype(BF16)


def _ffn(xe, wg, wu, wd, *, tm):
    e, cap_pad, d = xe.shape
    f = wg.shape[2]
    return pl.pallas_call(
        functools.partial(_ffn_body, fc=min(512, f)),
        grid=(e, cap_pad // tm),
        in_specs=[
            pl.BlockSpec((1, tm, d), lambda ei, i: (ei, i, 0)),
            pl.BlockSpec((1, d, f), lambda ei, i: (ei, 0, 0)),
            pl.BlockSpec((1, d, f), lambda ei, i: (ei, 0, 0)),
            pl.BlockSpec((1, f, d), lambda ei, i: (ei, 0, 0)),
        ],
        out_specs=pl.BlockSpec((1, tm, d), lambda ei, i: (ei, i, 0)),
        out_shape=jax.ShapeDtypeStruct((e, cap_pad, d), BF16),
        compiler_params=_cparams(("parallel", "arbitrary")),
        name="ec_ffn",
    )(xe, wg, wu, wd)


def _combine_body(offs_ref, cnts_ref, aff_ref, tau_ref, cut_ref, x2_ref, tri_ref, fn_ref, ywin_ref, ye_hbm,
                  o_ref, rank_scr, gate_scr, acc_scr, ybuf, sem, *, nb, t, l_total):
    j = pl.program_id(0)
    e = pl.program_id(1)

    @pl.when(e == 0)
    def _():
        aff = aff_ref[...]
        bits = pltpu.bitcast(aff, I32)
        tau = tau_ref[0:1, :]
        cut = cut_ref[0:1, :]
        row0 = j * TOK_BLK
        gidx = row0 + (row0 // t) * (l_total - t) + (l_total - t) + lax.broadcasted_iota(I32, bits.shape, 0)
        sel = (bits > tau) | ((bits == tau) & (gidx <= cut))
        sel = sel & (lax.broadcasted_iota(I32, bits.shape, 1) < N_EXPERTS)
        self32 = jnp.where(sel, 1.0, 0.0)
        incl = _dot(tri_ref[...], self32.astype(BF16))
        rank_scr[...] = jnp.where(sel, incl - 1.0, -1.0)
        gate_scr[...] = jnp.where(sel, aff, 0.0)
        acc_scr[...] = x2_ref[...]

    lane = lax.broadcasted_iota(I32, (TOK_BLK, LANES), 1)
    pick = lane == e
    rank = jnp.sum(jnp.where(pick, rank_scr[...], 0.0), axis=1, keepdims=True).astype(I32)
    gate = jnp.sum(jnp.where(pick, gate_scr[...], 0.0), axis=1, keepdims=True)
    cnt = cnts_ref[e * nb + j]
    off = offs_ref[e * nb + j]
    off_al = (off // 16) * 16
    rel = jnp.where(rank >= 0, rank + (off - off_al), -1)
    nchunk = (off - off_al + cnt + ROW_WIN - 1) // ROW_WIN
    colid = lax.broadcasted_iota(I32, (TOK_BLK, ROW_WIN), 1)

    onehot = jnp.where(colid == rel, 1.0, 0.0).astype(BF16)
    acc_scr[...] += _dot(onehot, ywin_ref[...]) * gate

    def chunk(c, carry):
        start = pl.multiple_of(off_al + c * ROW_WIN, 16)
        cp = pltpu.make_async_copy(ye_hbm.at[e, pl.ds(start, ROW_WIN), :], ybuf, sem)
        cp.start()
        cp.wait()
        oh = jnp.where(colid == rel - c * ROW_WIN, 1.0, 0.0).astype(BF16)
        acc_scr[...] += _dot(oh, ybuf[...]) * gate
        return carry

    lax.fori_loop(1, nchunk, chunk, 0)

    @pl.when(e == pl.num_programs(1) - 1)
    def _():
        y = acc_scr[...]
        ms = jnp.mean(y * y, axis=-1, keepdims=True)
        o_ref[...] = y * lax.rsqrt(ms + RMS_EPS) * fn_ref[...]


def _combine(offs_flat, cnts_flat, aff, tau_row, cut_row, x2, tri_l, fnorm, ye, *, t, l_total):
    nr, d = x2.shape
    nb = nr // TOK_BLK
    e = ye.shape[0]

    def win_map(j, ei, o, c):
        return (ei, (o[ei * nb + j] // 16) * 16, 0)

    return pl.pallas_call(
        functools.partial(_combine_body, nb=nb, t=t, l_total=l_total),
        grid_spec=pltpu.PrefetchScalarGridSpec(
            num_scalar_prefetch=2,
            grid=(nb, e),
            in_specs=[
                pl.BlockSpec((TOK_BLK, LANES), lambda j, ei, o, c: (j, 0)),
                pl.BlockSpec((8, LANES), lambda j, ei, o, c: (0, 0)),
                pl.BlockSpec((8, LANES), lambda j, ei, o, c: (0, 0)),
                pl.BlockSpec((TOK_BLK, d), lambda j, ei, o, c: (j, 0)),
                pl.BlockSpec((TOK_BLK, TOK_BLK), lambda j, ei, o, c: (0, 0)),
                pl.BlockSpec((1, d), lambda j, ei, o, c: (0, 0)),
                pl.BlockSpec((pl.Squeezed(), pl.Element(ROW_WIN), pl.Element(d)), win_map),
                pl.BlockSpec(memory_space=pl.ANY),
            ],
            out_specs=pl.BlockSpec((TOK_BLK, d), lambda j, ei, o, c: (j, 0)),
            scratch_shapes=[
                pltpu.VMEM((TOK_BLK, LANES), F32),
                pltpu.VMEM((TOK_BLK, LANES), F32),
                pltpu.VMEM((TOK_BLK, d), F32),
                pltpu.VMEM((ROW_WIN, d), BF16),
                pltpu.SemaphoreType.DMA(()),
            ],
        ),
        out_shape=jax.ShapeDtypeStruct((nr, d), F32),
        compiler_params=_cparams(("parallel", "arbitrary")),
        name="ec_combine",
    )(offs_flat, cnts_flat, aff, tau_row, cut_row, x2, tri_l, fnorm, ye, ye)


def _rope_tables(positions):
    inv_freq = 1.0 / (ROPE_THETA ** (jnp.arange(0, HEAD_DIM, 2, dtype=F32) / HEAD_DIM))
    ang = positions.astype(F32)[:, None] * inv_freq[None, :]
    cos, sin = jnp.cos(ang), jnp.sin(ang)
    return (jnp.concatenate([cos, cos, cos, cos], axis=1),
            jnp.concatenate([-sin, -sin, sin, sin], axis=1))


def _permute_in_weights(w_in):
    l = jnp.arange(LANES)
    g = l // 32
    within = (g % 2) * HEAD_DIM + (g // 2) * 32 + l % 32
    cols = jnp.arange(PROJ_WIDTH)
    blk = cols // LANES
    permuted = blk * LANES + within[cols % LANES]
    cols = jnp.where((blk >= DQ_BLK) & (blk < DV_BLK), permuted, cols)
    return w_in[:, cols]


def _encode_group(x, shared):
    b, t, d = x.shape
    l_total = N_META + t
    n_total = b * l_total
    cap = EC_CAPACITY_FACTOR * n_total // N_EXPERTS
    rows = b * t
    xf = x.reshape(rows, d)

    cos, sin = _rope_tables(N_META + jnp.arange(t))
    proj = _inproj(xf, shared["norm_mix"], shared["w_in"], cos, sin, tm=512).reshape(b, t, PROJ_WIDTH)
    proj_meta = shared["proj_meta"]

    na = _na_attention(proj, proj_meta, shared["na_table"], shared["mb_stack"])
    lam = shared["lambda"]
    tk = min(1024, t)
    df = _diff_attention(proj, proj, proj_meta, *lam, shared["subln"], tq=min(1024, t), tk=tk, q_shared=False)
    df_meta = _diff_attention(proj_meta[None], proj, proj_meta, *lam, shared["subln"],
                              tq=N_META, tk=tk, q_shared=True)

    wr = (shared["wr_hi"], shared["wr_lo"])
    x2, h2, aff_t, aff = _outproj(xf, na.reshape(rows, NA_WIDTH), df.reshape(rows, DIFF_WIDTH), shared["w_out"],
                                  shared["norm_ffn"], *wr, tm=512, token_major=True)
    xm = jnp.tile(shared["meta_tokens"], (b, 1))
    nam = jnp.tile(shared["na_meta"], (b, 1))
    _, _, aff_t_meta = _outproj(xm, nam, df_meta.reshape(b * N_META, DIFF_WIDTH), shared["w_out"],
                                shared["norm_ffn"], *wr, tm=b * N_META, token_major=False)

    r = jnp.arange(rows, dtype=I32)
    gidx = (r + (r // t + 1) * N_META)[None, :]
    rm = jnp.arange(b * N_META, dtype=I32)
    gidx_meta = ((rm // N_META) * l_total + rm % N_META)[None, :]
    tau, cut, offs, cnts, rank_t = _select(aff_t, aff_t_meta, gidx, gidx_meta, shared["tri_u"],
                                           cap=cap, n_total=n_total)

    nb = rows // TOK_BLK
    offs_flat = offs[:, :nb].reshape(-1)
    cnts_flat = cnts[:, :nb].reshape(-1)
    cap_pad = -(-(cap + ROW_WIN) // 256) * 256
    ffn_tm = max(tm for tm in range(16, 1025, 16) if cap_pad % tm == 0)
    xe = _gather(offs_flat, cnts_flat, rank_t.reshape(N_EXPERTS, 1, rows), h2, cap_pad=cap_pad)
    ye = _ffn(xe, shared["w_gate"], shared["w_up"], shared["w_down"], tm=ffn_tm)

    pad = jnp.zeros((8, LANES - N_EXPERTS), I32)
    tau_row = jnp.concatenate([jnp.broadcast_to(tau[:, 0][None, :], (8, N_EXPERTS)), pad], axis=1)
    cut_row = jnp.concatenate([jnp.broadcast_to(cut[:, 0][None, :], (8, N_EXPERTS)), pad], axis=1)
    y = _combine(offs_flat, cnts_flat, aff, tau_row, cut_row, x2, shared["tri_l"], shared["final_norm"], ye,
                 t=t, l_total=l_total)
    return y.reshape(b, t, d)


def kernel(x_prompt, x_sample, meta_tokens, norm_mix, w_in, na_rpb, na_meta_bias, lambda_q1, lambda_k1,
           lambda_q2, lambda_k2, diff_subln, w_out, norm_ffn, w_router, w_gate, w_up, w_down, final_norm):
    d = x_prompt.shape[-1]
    wr = jnp.zeros((LANES, d), F32).at[:N_EXPERTS].set(w_router[0].T)
    wr_hi = wr.astype(BF16)
    mb = na_meta_bias[0].astype(F32)
    idx = jnp.arange(TOK_BLK)
    shared = {
        "meta_tokens": meta_tokens,
        "norm_mix": norm_mix[0][None, :],
        "w_in": _permute_in_weights(w_in[0]).astype(BF16),
        "lambda": (lambda_q1, lambda_k1, lambda_q2, lambda_k2),
        "subln": diff_subln,
        "w_out": w_out[0].astype(BF16),
        "norm_ffn": norm_ffn[0][None, :],
        "wr_hi": wr_hi,
        "wr_lo": (wr - wr_hi.astype(F32)).astype(BF16),
        "w_gate": w_gate[0].astype(BF16),
        "w_up": w_up[0].astype(BF16),
        "w_down": w_down[0].astype(BF16),
        "final_norm": final_norm[None, :],
        "mb_stack": jnp.repeat(mb, GRID_W, axis=0).reshape(NA_HEADS // 2, 2 * GRID_W, N_META),
        "tri_u": (idx[:, None] <= idx[None, :]).astype(BF16),
        "tri_l": (idx[:, None] >= idx[None, :]).astype(BF16),
    }
    cos_m, sin_m = _rope_tables(jnp.arange(N_META))
    shared["proj_meta"] = _inproj(meta_tokens, shared["norm_mix"], shared["w_in"], cos_m, sin_m, tm=N_META)
    mb_meta = jnp.repeat(mb, N_META, axis=0).reshape(NA_HEADS // 2, 2 * N_META, N_META)
    shared["na_meta"] = _na_meta(shared["proj_meta"], mb_meta)
    shared["na_table"] = _na_bias(na_rpb[0].astype(F32))
    return (_encode_group(x_prompt, shared), _encode_group(x_sample, shared))
```

```python
import functools
import math

import jax
import jax.numpy as jnp
from jax import lax
from jax.experimental import pallas as pl
from jax.experimental.pallas import tpu as pltpu

BF16 = jnp.bfloat16
F32 = jnp.float32
I32 = jnp.int32

N_META = 16
GRID_W = 64
HEAD_DIM = 64
NA_HEADS = 8
NA_WIDTH = NA_HEADS * HEAD_DIM
NA_KH = 8
NA_KW = 16
DIFF_HEADS = 4
DIFF_WIDTH = DIFF_HEADS * 2 * HEAD_DIM
PROJ_WIDTH = 3 * NA_WIDTH + 3 * DIFF_WIDTH
ROPE_THETA = 10000.0
N_EXPERTS = 16
EC_CAPACITY_FACTOR = 2
RMS_EPS = 1e-6
SUBLN_EPS = 1e-5
LAMBDA_INIT = 0.8 - 0.6 * math.exp(-0.3 * 0)

LANES = 128
NEG_BIG = -1e30
VMEM_LIMIT = 56 * 1024 * 1024
DIFF_VMEM_LIMIT = 60 * 1024 * 1024

NAQ_BLK, NAK_BLK, NAV_BLK = 0, 4, 8
DQ_BLK, DK_BLK, DV_BLK = 12, 16, 20

ROW_WIN = 256
TOK_BLK = 1024
ONES_ROWS = 16


def _cparams(sem, vmem=VMEM_LIMIT):
    return pltpu.CompilerParams(dimension_semantics=sem, vmem_limit_bytes=vmem)


def _dot(a, b):
    return jnp.dot(a, b, preferred_element_type=F32)


def _dot_nt(a, b):
    return lax.dot_general(a, b, (((1,), (1,)), ((), ())), preferred_element_type=F32)


def _inproj_body(x_ref, g_ref, w_ref, wvt_ref, cos_ref, sin_ref, o_ref, vt_ref, *, tn):
    x = x_ref[...]
    ms = jnp.mean(x * x, axis=-1, keepdims=True)
    h = (x * lax.rsqrt(ms + RMS_EPS) * g_ref[...]).astype(BF16)
    cos = cos_ref[...]
    sin = sin_ref[...]
    vt_ref[...] = _dot_nt(wvt_ref[...], h).astype(BF16)
    for c in range(PROJ_WIDTH // tn):
        lo = c * tn
        acc = _dot(h, w_ref[:, lo:lo + tn])
        for s in range(tn // LANES):
            blk = (lo + s * LANES) // LANES
            a = acc[:, s * LANES:(s + 1) * LANES]
            if DQ_BLK <= blk < DV_BLK:
                a = a * cos + pltpu.roll(a, 64, 1) * sin
            if blk < NAK_BLK:
                a = a * (HEAD_DIM ** -0.5)
            if DQ_BLK <= blk < DK_BLK:
                a = a * (HEAD_DIM ** -0.5 * math.log2(math.e))
            o_ref[:, lo + s * LANES:lo + (s + 1) * LANES] = a.astype(BF16)


def _inproj(x, g, w, wvt, cos, sin, *, tm):
    rows, d = x.shape
    nt = cos.shape[0] // tm
    return pl.pallas_call(
        functools.partial(_inproj_body, tn=512),
        grid=(rows // tm,),
        in_specs=[
            pl.BlockSpec((tm, d), lambda i: (i, 0)),
            pl.BlockSpec((1, d), lambda i: (0, 0)),
            pl.BlockSpec((d, PROJ_WIDTH), lambda i: (0, 0)),
            pl.BlockSpec((DIFF_WIDTH, d), lambda i: (0, 0)),
            pl.BlockSpec((tm, LANES), lambda i: (i % nt, 0)),
            pl.BlockSpec((tm, LANES), lambda i: (i % nt, 0)),
        ],
        out_specs=[pl.BlockSpec((tm, PROJ_WIDTH), lambda i: (i, 0)),
                   pl.BlockSpec((DIFF_WIDTH, tm), lambda i: (0, i))],
        out_shape=[jax.ShapeDtypeStruct((rows, PROJ_WIDTH), BF16),
                   jax.ShapeDtypeStruct((DIFF_WIDTH, rows), BF16)],
        compiler_params=_cparams(("parallel",)),
        name="inproj",
    )(x, g, w, wvt, cos, sin)


def _na_bias_body(rpb_ref, o_ref):
    h = pl.program_id(0)
    c = lax.broadcasted_iota(I32, (GRID_W, GRID_W), 0)
    kc = lax.broadcasted_iota(I32, (GRID_W, GRID_W), 1)
    cs = jnp.clip(c - NA_KW // 2, 0, GRID_W - NA_KW)
    valid = (kc >= cs) & (kc < cs + NA_KW)
    jm = kc - c + NA_KW - 1
    n_dr = 2 * NA_KH - 1
    n_j = 2 * NA_KW - 1
    tiles = []
    for dr in range(n_dr):
        base = (h * n_dr + dr) * n_j

        def body(j, acc, base=base):
            return jnp.where(jm == j, rpb_ref[base + j], acc)

        t = lax.fori_loop(0, n_j, body, jnp.zeros((GRID_W, GRID_W), F32))
        tiles.append(jnp.where(valid, t, NEG_BIG))
    for d in range(NA_KH):
        o_ref[0, d] = jnp.concatenate([tiles[kr - d + NA_KH - 1] for kr in range(NA_KH)], axis=1)


def _na_bias(rpb):
    h = rpb.shape[0]
    return pl.pallas_call(
        _na_bias_body,
        grid_spec=pltpu.PrefetchScalarGridSpec(
            num_scalar_prefetch=1,
            grid=(h,),
            in_specs=[],
            out_specs=pl.BlockSpec((1, NA_KH, GRID_W, NA_KH * GRID_W), lambda i, r: (i, 0, 0, 0)),
        ),
        out_shape=jax.ShapeDtypeStruct((h, NA_KH, GRID_W, NA_KH * GRID_W), F32),
        compiler_params=_cparams(("arbitrary",)),
        name="na_bias",
    )(rpb.reshape(-1))


def _split_heads(q):
    lane = lax.broadcasted_iota(I32, q.shape, 1)
    zero = jnp.zeros_like(q)
    return jnp.concatenate([jnp.where(lane < HEAD_DIM, q, zero), jnp.where(lane < HEAD_DIM, zero, q)], axis=0)


def _merge_heads(o):
    n = o.shape[0] // 2
    lane = lax.broadcasted_iota(I32, (n, LANES), 1)
    return jnp.where(lane < HEAD_DIM, o[:n], o[n:])


def _na_body(q_ref, kp_ref, kc_ref, kn_ref, vp_ref, vc_ref, vn_ref, km_ref, vm_ref, tab_ref, mb_ref,
             o_ref, kbuf, vbuf, qq_scr, s_scr, p_scr, *, rows):
    rb = pl.program_id(2)
    blk = NA_KH * GRID_W
    kbuf[0:blk] = kp_ref[0]
    kbuf[blk:2 * blk] = kc_ref[0]
    kbuf[2 * blk:3 * blk] = kn_ref[0]
    vbuf[0:blk] = vp_ref[0]
    vbuf[blk:2 * blk] = vc_ref[0]
    vbuf[2 * blk:3 * blk] = vn_ref[0]
    two = 2 * GRID_W
    starts = []
    for i in range(NA_KH):
        r = rb * NA_KH + i
        rs = jnp.clip(r - NA_KH // 2, 0, rows - NA_KH)
        d = r - rs
        start = pl.multiple_of((rs - (rb - 1) * NA_KH) * GRID_W, GRID_W)
        starts.append(start)
        qq = _split_heads(q_ref[0, i * GRID_W:(i + 1) * GRID_W, :])
        qq_scr[i * two:(i + 1) * two, :] = qq
        bias = jnp.concatenate([tab_ref[0, d], tab_ref[1, d]], axis=0)
        s_scr[i * two:(i + 1) * two, :] = _dot_nt(qq, kbuf[pl.ds(start, blk), :]) + bias
    s = s_scr[...]
    sm = (_dot_nt(qq_scr[...], km_ref[...]).reshape(NA_KH, two, N_META) + mb_ref[0][None]).reshape(NA_KH * two, N_META)
    m = jnp.maximum(jnp.max(s, axis=1, keepdims=True), jnp.max(sm, axis=1, keepdims=True))
    p = jnp.exp(s - m)
    pm = jnp.exp(sm - m)
    l = jnp.sum(p, axis=1, keepdims=True) + jnp.sum(pm, axis=1, keepdims=True)
    p_scr[...] = p.astype(BF16)
    om = _dot(pm.astype(BF16), vm_ref[...])
    inv_l = 1.0 / l
    for i in range(NA_KH):
        rsl = slice(i * two, (i + 1) * two)
        o = (_dot(p_scr[rsl, :], vbuf[pl.ds(starts[i], blk), :]) + om[rsl]) * inv_l[rsl]
        o_ref[0, i * GRID_W:(i + 1) * GRID_W, :] = _merge_heads(o).astype(BF16)


def _na_attention(proj, proj_meta, table, mb_stack):
    b, t, _ = proj.shape
    rows = t // GRID_W
    nrb = rows // NA_KH
    blk = NA_KH * GRID_W
    hp = NA_HEADS // 2

    def kv_spec(col, shift):
        return pl.BlockSpec((1, blk, LANES),
                            lambda h, bi, rb: (bi, jnp.clip(rb + shift, 0, nrb - 1), col + h))

    return pl.pallas_call(
        functools.partial(_na_body, rows=rows),
        grid=(hp, b, nrb),
        in_specs=[
            pl.BlockSpec((1, blk, LANES), lambda h, bi, rb: (bi, rb, NAQ_BLK + h)),
            kv_spec(NAK_BLK, -1), kv_spec(NAK_BLK, 0), kv_spec(NAK_BLK, 1),
            kv_spec(NAV_BLK, -1), kv_spec(NAV_BLK, 0), kv_spec(NAV_BLK, 1),
            pl.BlockSpec((N_META, LANES), lambda h, bi, rb: (0, NAK_BLK + h)),
            pl.BlockSpec((N_META, LANES), lambda h, bi, rb: (0, NAV_BLK + h)),
            pl.BlockSpec((2, NA_KH, GRID_W, blk), lambda h, bi, rb: (h, 0, 0, 0)),
            pl.BlockSpec((1, LANES, N_META), lambda h, bi, rb: (h, 0, 0)),
        ],
        out_specs=pl.BlockSpec((1, blk, LANES), lambda h, bi, rb: (bi, rb, h)),
        out_shape=jax.ShapeDtypeStruct((b, t, NA_WIDTH), BF16),
        scratch_shapes=[pltpu.VMEM((3 * blk, LANES), BF16), pltpu.VMEM((3 * blk, LANES), BF16),
                        pltpu.VMEM((2 * blk, LANES), BF16), pltpu.VMEM((2 * blk, blk), F32),
                        pltpu.VMEM((2 * blk, blk), BF16)],
        compiler_params=_cparams(("arbitrary", "arbitrary", "arbitrary")),
        name="na_attention",
    )(proj, proj, proj, proj, proj, proj, proj, proj_meta, proj_meta, table, mb_stack)


def _na_meta_body(p_ref, mb_ref, o_ref):
    for h in range(NA_HEADS // 2):
        qq = _split_heads(p_ref[:, (NAQ_BLK + h) * LANES:(NAQ_BLK + h + 1) * LANES])
        km = p_ref[:, (NAK_BLK + h) * LANES:(NAK_BLK + h + 1) * LANES]
        vm = p_ref[:, (NAV_BLK + h) * LANES:(NAV_BLK + h + 1) * LANES]
        s = _dot_nt(qq, km) + mb_ref[h, 0:2 * N_META, :]
        m = jnp.max(s, axis=1, keepdims=True)
        p = jnp.exp(s - m)
        l = jnp.sum(p, axis=1, keepdims=True)
        o = _dot(p.astype(BF16), vm) / l
        o_ref[:, h * LANES:(h + 1) * LANES] = _merge_heads(o).astype(BF16)


def _na_meta(proj_meta, mb_meta):
    return pl.pallas_call(
        _na_meta_body,
        out_shape=jax.ShapeDtypeStruct((N_META, NA_WIDTH), BF16),
        name="na_meta",
    )(proj_meta, mb_meta)


def _split_maps(q):
    lane = lax.broadcasted_iota(I32, q.shape, 1)
    is0 = ((lane >> 5) & 1) == 0
    zero = jnp.zeros_like(q)
    return jnp.concatenate([jnp.where(is0, q, zero), jnp.where(is0, zero, q)], axis=0)


def _diff_body(q_ref, k_ref, kn_ref, vtp_ref, vt_ref, km_ref, vtm_ref, lq1_ref, lk1_ref, lq2_ref, lk2_ref, sub_ref,
               o_ref, qq_scr, vx_scr, s0, s1, b0, b1, p0, p1, a0, a1, m_scr, acc_scr, *, tq, cw, nk):
    ki = pl.program_id(3)
    ng = 2 * tq // cw
    ones = jnp.ones((ONES_ROWS, vx_scr.shape[1]), BF16)

    @pl.when(ki == 0)
    def _():
        qq = _split_maps(q_ref[0])
        vx_scr[LANES:, :] = ones
        vmx = jnp.concatenate([vtm_ref[...], ones[:, 0:N_META]], axis=0)
        for g in range(ng):
            qg = qq[g * cw:(g + 1) * cw]
            qq_scr[g] = qg
            s = _dot_nt(km_ref[...], qg)
            m = jnp.max(s, axis=0, keepdims=True)
            m_scr[g] = m
            acc_scr[g] = _dot(vmx, jnp.exp2((s - m).astype(BF16)))
            sn = _dot_nt(k_ref[0], qg)
            s0[g] = sn
            b0[g] = jnp.max(sn, axis=0, keepdims=True)
        p1[...] = jnp.zeros_like(p1)
        a1[...] = jnp.ones_like(a1)

    def step(s_cur, s_nxt, b_cur, b_nxt, p_cur, p_prv, a_cur, a_prv):
        vx_scr[0:LANES, :] = vtp_ref[...]
        for g in range(ng):
            sn = _dot_nt(kn_ref[0], qq_scr[g])
            s_nxt[g] = sn
            b_nxt[g] = jnp.max(sn, axis=0, keepdims=True)
            acc_scr[g] = a_prv[g] * acc_scr[g] + _dot(vx_scr[...], p_prv[g])
            m_prev = m_scr[g]
            m_new = jnp.maximum(m_prev, b_cur[g])
            a_cur[g] = jnp.exp2(m_prev - m_new)
            m_scr[g] = m_new
            p_cur[g] = jnp.exp2((s_cur[g] - m_new).astype(BF16))

    @pl.when(ki % 2 == 0)
    def _():
        step(s0, s1, b0, b1, p0, p1, a0, a1)

    @pl.when(ki % 2 == 1)
    def _():
        step(s1, s0, b1, b0, p1, p0, a1, a0)

    @pl.when(ki == nk - 1)
    def _():
        p_last, a_last = (p0, a0) if (nk - 1) % 2 == 0 else (p1, a1)
        vx_scr[0:LANES, :] = vt_ref[...]
        acc = jnp.concatenate(
            [a_last[g] * acc_scr[g] + _dot(vx_scr[...], p_last[g]) for g in range(ng)], axis=1)
        lam = (jnp.exp(jnp.sum(lq1_ref[...] * lk1_ref[...], axis=1, keepdims=True))
               - jnp.exp(jnp.sum(lq2_ref[...] * lk2_ref[...], axis=1, keepdims=True)) + LAMBDA_INIT)
        o_all = acc[0:LANES] / acc[LANES:LANES + 1]
        o = o_all[:, :tq] - lam * o_all[:, tq:]
        ms = jnp.mean(o * o, axis=0, keepdims=True)
        o = o * lax.rsqrt(ms + SUBLN_EPS) * sub_ref[...] * (1.0 - LAMBDA_INIT)
        o_ref[0] = o.T.astype(BF16)


def _diff_attention(q_src, proj, vt, proj_meta, vt_meta, lq1, lk1, lq2, lk2, subln_col, *, tq, tk, q_shared):
    b, t, _ = proj.shape
    tq_total = q_src.shape[1]
    nq = tq_total // tq
    nk = t // tk
    if q_shared:
        q_map = lambda bi, h, qi, ki: (0, qi, DQ_BLK + h)
    else:
        q_map = lambda bi, h, qi, ki: (bi, qi, DQ_BLK + h)
    vec = lambda n: pl.BlockSpec((1, n), lambda bi, h, qi, ki: (0, 0))
    cw = min(256, 2 * tq)
    ng = 2 * tq // cw
    return pl.pallas_call(
        functools.partial(_diff_body, tq=tq, cw=cw, nk=nk),
        grid=(b, DIFF_HEADS, nq, nk),
        in_specs=[
            pl.BlockSpec((1, tq, LANES), q_map),
            pl.BlockSpec((1, tk, LANES), lambda bi, h, qi, ki: (bi, ki, DK_BLK + h)),
            pl.BlockSpec((1, tk, LANES), lambda bi, h, qi, ki: (bi, jnp.minimum(ki + 1, nk - 1), DK_BLK + h)),
            pl.BlockSpec((LANES, tk), lambda bi, h, qi, ki: (h, bi * nk + jnp.maximum(ki - 1, 0))),
            pl.BlockSpec((LANES, tk), lambda bi, h, qi, ki: (h, bi * nk + ki)),
            pl.BlockSpec((N_META, LANES), lambda bi, h, qi, ki: (0, DK_BLK + h)),
            pl.BlockSpec((LANES, N_META), lambda bi, h, qi, ki: (h, 0)),
            vec(HEAD_DIM), vec(HEAD_DIM), vec(HEAD_DIM), vec(HEAD_DIM),
            pl.BlockSpec((2 * HEAD_DIM, 1), lambda bi, h, qi, ki: (0, 0)),
        ],
        out_specs=pl.BlockSpec((1, tq, LANES), lambda bi, h, qi, ki: (bi, qi, h)),
        out_shape=jax.ShapeDtypeStruct((b, tq_total, DIFF_WIDTH), BF16),
        scratch_shapes=[
            pltpu.VMEM((ng, cw, LANES), BF16),
            pltpu.VMEM((LANES + ONES_ROWS, tk), BF16),
            pltpu.VMEM((ng, tk, cw), F32), pltpu.VMEM((ng, tk, cw), F32),
            pltpu.VMEM((ng, 1, cw), F32), pltpu.VMEM((ng, 1, cw), F32),
            pltpu.VMEM((ng, tk, cw), BF16), pltpu.VMEM((ng, tk, cw), BF16),
            pltpu.VMEM((ng, 1, cw), F32), pltpu.VMEM((ng, 1, cw), F32),
            pltpu.VMEM((ng, 1, cw), F32),
            pltpu.VMEM((ng, LANES + ONES_ROWS, cw), F32),
        ],
        compiler_params=_cparams(("parallel", "parallel", "parallel", "arbitrary"), vmem=DIFF_VMEM_LIMIT),
        name="diff_attention",
    )(q_src, proj, proj, vt, vt, proj_meta, vt_meta, lq1, lk1, lq2, lk2, subln_col)


def _outproj_body(x_ref, na_ref, df_ref, wo_ref, g_ref, wrh_ref, wrl_ref, x2_ref, h2_ref, afft_ref, *aff_ref):
    x2 = x_ref[...] + _dot(na_ref[...], wo_ref[0:NA_WIDTH, :]) + _dot(df_ref[...], wo_ref[NA_WIDTH:, :])
    x2_ref[...] = x2
    ms = jnp.mean(x2 * x2, axis=-1, keepdims=True)
    hf = x2 * lax.rsqrt(ms + RMS_EPS) * g_ref[...]
    hi = hf.astype(BF16)
    h2_ref[...] = hi
    lo = (hf - hi.astype(F32)).astype(BF16)
    logits = _dot_nt(wrh_ref[...], hi) + _dot_nt(wrh_ref[...], lo) + _dot_nt(wrl_ref[...], hi)
    row = lax.broadcasted_iota(I32, logits.shape, 0)
    logits = jnp.where(row < N_EXPERTS, logits, NEG_BIG)
    m = jnp.max(logits, axis=0, keepdims=True)
    e = jnp.exp(logits - m)
    aff = e / jnp.sum(e, axis=0, keepdims=True)
    afft_ref[...] = aff[0:N_EXPERTS]
    if aff_ref:
        aff_ref[0][...] = aff.T


def _outproj(x, na, df, wo, g, wrh, wrl, *, tm, token_major):
    rows, d = x.shape
    out_shape = [jax.ShapeDtypeStruct((rows, d), F32), jax.ShapeDtypeStruct((rows, d), BF16),
                 jax.ShapeDtypeStruct((N_EXPERTS, rows), F32)]
    out_specs = [pl.BlockSpec((tm, d), lambda i: (i, 0)), pl.BlockSpec((tm, d), lambda i: (i, 0)),
                 pl.BlockSpec((N_EXPERTS, tm), lambda i: (0, i))]
    if token_major:
        out_shape.append(jax.ShapeDtypeStruct((rows, LANES), F32))
        out_specs.append(pl.BlockSpec((tm, LANES), lambda i: (i, 0)))
    return pl.pallas_call(
        _outproj_body,
        grid=(rows // tm,),
        in_specs=[
            pl.BlockSpec((tm, d), lambda i: (i, 0)),
            pl.BlockSpec((tm, NA_WIDTH), lambda i: (i, 0)),
            pl.BlockSpec((tm, DIFF_WIDTH), lambda i: (i, 0)),
            pl.BlockSpec((NA_WIDTH + DIFF_WIDTH, d), lambda i: (0, 0)),
            pl.BlockSpec((1, d), lambda i: (0, 0)),
            pl.BlockSpec((LANES, d), lambda i: (0, 0)),
            pl.BlockSpec((LANES, d), lambda i: (0, 0)),
        ],
        out_specs=out_specs,
        out_shape=out_shape,
        compiler_params=_cparams(("parallel",)),
        name="outproj_router",
    )(x, na, df, wo, g, wrh, wrl)


def _select_body(ar_ref, am_ref, gr_ref, gm_ref, tri_ref, tau_ref, cut_ref, offs_ref, cnt_ref, rank_ref, sel_scr,
                 *, cap, nb, idx_bits):
    br = pltpu.bitcast(ar_ref[...], I32)
    bm = pltpu.bitcast(am_ref[...], I32)
    gr = gr_ref[...]
    gm = gm_ref[...]

    def count(mr, mm):
        return (jnp.sum(jnp.where(mr, 1.0, 0.0), axis=1, keepdims=True)
                + jnp.sum(jnp.where(mm, 1.0, 0.0), axis=1, keepdims=True))

    def value_bit(i, ans):
        cand = ans | jnp.left_shift(jnp.int32(1), 30 - i)
        return jnp.where(count(br >= cand, bm >= cand) >= cap, cand, ans)

    tau = lax.fori_loop(0, 31, value_bit, jnp.zeros((N_EXPERTS, 1), I32))
    need = cap - count(br > tau, bm > tau)
    eq_r = br == tau
    eq_m = bm == tau

    def index_bit(i, ans):
        cand = ans | jnp.left_shift(jnp.int32(1), idx_bits - 1 - i)
        return jnp.where(count(eq_r & (gr < cand), eq_m & (gm < cand)) < need, cand, ans)

    cut = lax.fori_loop(0, idx_bits, index_bit, jnp.zeros((N_EXPERTS, 1), I32))
    tau_ref[...] = jnp.broadcast_to(tau, tau_ref.shape)
    cut_ref[...] = jnp.broadcast_to(cut, cut_ref.shape)

    sel_scr[...] = jnp.where((br > tau) | (eq_r & (gr <= cut)), 1.0, 0.0)
    lane = lax.broadcasted_iota(I32, (N_EXPERTS, LANES), 1)

    def block_count(j, acc):
        start = pl.multiple_of(j * TOK_BLK, TOK_BLK)
        sel = sel_scr[:, pl.ds(start, TOK_BLK)]
        rank_ref[:, pl.ds(start, TOK_BLK)] = sel * _dot(sel.astype(BF16), tri_ref[...])
        c = jnp.sum(sel, axis=1, keepdims=True)
        return jnp.where(lane == j, c, acc)

    counts = lax.fori_loop(0, nb, block_count, jnp.zeros((N_EXPERTS, LANES), F32))
    incl = counts
    shift = 1
    while shift < LANES:
        incl = incl + jnp.where(lane >= shift, pltpu.roll(incl, shift, 1), 0.0)
        shift *= 2
    offs_ref[...] = (incl - counts).astype(I32)
    cnt_ref[...] = counts.astype(I32)


def _select(aff_t, aff_t_meta, gidx, gidx_meta, tri_u, *, cap, n_total):
    e, nr = aff_t.shape
    nb = nr // TOK_BLK
    assert nb <= LANES and nr % TOK_BLK == 0
    idx_bits = max(1, (n_total - 1).bit_length())
    shp = jax.ShapeDtypeStruct((e, LANES), I32)
    return pl.pallas_call(
        functools.partial(_select_body, cap=cap, nb=nb, idx_bits=idx_bits),
        out_shape=[shp, shp, shp, shp, jax.ShapeDtypeStruct((e, nr), F32)],
        scratch_shapes=[pltpu.VMEM((e, nr), F32)],
        compiler_params=pltpu.CompilerParams(vmem_limit_bytes=VMEM_LIMIT),
        name="ec_select",
    )(aff_t, aff_t_meta, gidx, gidx_meta, tri_u)


def _gather_body(offs_ref, cnts_ref, rank_ref, h_ref, xe_ref, *, nb):
    e = pl.program_id(0)
    j = pl.program_id(1)

    @pl.when(j == 0)
    def _():
        xe_ref[...] = jnp.zeros_like(xe_ref)

    rank = rank_ref[0].astype(I32)
    cnt = cnts_ref[e * nb + j]
    off = offs_ref[e * nb + j]
    off_al = (off // 16) * 16
    rel = jnp.where(rank > 0, rank - 1 + (off - off_al), -1)
    nchunk = (off - off_al + cnt + ROW_WIN - 1) // ROW_WIN
    hblk = h_ref[...]
    rowid = lax.broadcasted_iota(I32, (ROW_WIN, TOK_BLK), 0)

    def chunk(c, carry):
        onehot = jnp.where(rowid == rel - c * ROW_WIN, 1.0, 0.0).astype(BF16)
        g = _dot(onehot, hblk).astype(BF16)
        start = pl.multiple_of(off_al + c * ROW_WIN, 16)
        xe_ref[0, pl.ds(start, ROW_WIN), :] = xe_ref[0, pl.ds(start, ROW_WIN), :] + g
        return carry

    lax.fori_loop(0, nchunk, chunk, 0)


def _gather(offs_flat, cnts_flat, rank3, h2, *, cap_pad):
    e = rank3.shape[0]
    nr, d = h2.shape
    nb = nr // TOK_BLK
    return pl.pallas_call(
        functools.partial(_gather_body, nb=nb),
        grid_spec=pltpu.PrefetchScalarGridSpec(
            num_scalar_prefetch=2,
            grid=(e, nb),
            in_specs=[
                pl.BlockSpec((1, 1, TOK_BLK), lambda ei, j, o, c: (ei, 0, j)),
                pl.BlockSpec((TOK_BLK, d), lambda ei, j, o, c: (j, 0)),
            ],
            out_specs=pl.BlockSpec((1, cap_pad, d), lambda ei, j, o, c: (ei, 0, 0)),
        ),
        out_shape=jax.ShapeDtypeStruct((e, cap_pad, d), BF16),
        compiler_params=_cparams(("parallel", "arbitrary")),
        name="ec_gather",
    )(offs_flat, cnts_flat, rank3, h2)


def _ffn_body(x_ref, wg_ref, wu_ref, wd_ref, o_ref, *, fc):
    x = x_ref[0]
    f = wg_ref.shape[2]
    acc = jnp.zeros((x.shape[0], wd_ref.shape[2]), F32)
    for c in range(f // fc):
        a = _dot(x, wg_ref[0, :, c * fc:(c + 1) * fc])
        b = _dot(x, wu_ref[0, :, c * fc:(c + 1) * fc])
        hmid = (a * jax.nn.sigmoid(a) * b).astype(BF16)
        acc = acc + _dot(hmid, wd_ref[0, c * fc:(c + 1) * fc, :])
    o_ref[0] = acc.astype(BF16)


def _ffn(xe, wg, wu, wd, *, tm):
    e, cap_pad, d = xe.shape
    f = wg.shape[2]
    return pl.pallas_call(
        functools.partial(_ffn_body, fc=min(512, f)),
        grid=(e, cap_pad // tm),
        in_specs=[
            pl.BlockSpec((1, tm, d), lambda ei, i: (ei, i, 0)),
            pl.BlockSpec((1, d, f), lambda ei, i: (ei, 0, 0)),
            pl.BlockSpec((1, d, f), lambda ei, i: (ei, 0, 0)),
            pl.BlockSpec((1, f, d), lambda ei, i: (ei, 0, 0)),
        ],
        out_specs=pl.BlockSpec((1, tm, d), lambda ei, i: (ei, i, 0)),
        out_shape=jax.ShapeDtypeStruct((e, cap_pad, d), BF16),
        compiler_params=_cparams(("parallel", "arbitrary")),
        name="ec_ffn",
    )(xe, wg, wu, wd)


def _combine_body(offs_ref, cnts_ref, aff_ref, tau_ref, cut_ref, x2_ref, tri_ref, fn_ref, ywin_ref, ye_hbm,
                  o_ref, rank_scr, gate_scr, acc_scr, ybuf, sem, *, nb, t, l_total):
    j = pl.program_id(0)
    e = pl.program_id(1)

    @pl.when(e == 0)
    def _():
        aff = aff_ref[...]
        bits = pltpu.bitcast(aff, I32)
        tau = tau_ref[0:1, :]
        cut = cut_ref[0:1, :]
        row0 = j * TOK_BLK
        gidx = row0 + (row0 // t) * (l_total - t) + (l_total - t) + lax.broadcasted_iota(I32, bits.shape, 0)
        sel = (bits > tau) | ((bits == tau) & (gidx <= cut))
        sel = sel & (lax.broadcasted_iota(I32, bits.shape, 1) < N_EXPERTS)
        self32 = jnp.where(sel, 1.0, 0.0)
        incl = _dot(tri_ref[...], self32.astype(BF16))
        rank_scr[...] = jnp.where(sel, incl - 1.0, -1.0)
        gate_scr[...] = jnp.where(sel, aff, 0.0)
        acc_scr[...] = x2_ref[...]

    lane = lax.broadcasted_iota(I32, (TOK_BLK, LANES), 1)
    pick = lane == e
    rank = jnp.sum(jnp.where(pick, rank_scr[...], 0.0), axis=1, keepdims=True).astype(I32)
    gate = jnp.sum(jnp.where(pick, gate_scr[...], 0.0), axis=1, keepdims=True)
    cnt = cnts_ref[e * nb + j]
    off = offs_ref[e * nb + j]
    off_al = (off // 16) * 16
    rel = jnp.where(rank >= 0, rank + (off - off_al), -1)
    nchunk = (off - off_al + cnt + ROW_WIN - 1) // ROW_WIN
    colid = lax.broadcasted_iota(I32, (TOK_BLK, ROW_WIN), 1)

    onehot = jnp.where(colid == rel, 1.0, 0.0).astype(BF16)
    acc_scr[...] += _dot(onehot, ywin_ref[...]) * gate

    def chunk(c, carry):
        start = pl.multiple_of(off_al + c * ROW_WIN, 16)
        cp = pltpu.make_async_copy(ye_hbm.at[e, pl.ds(start, ROW_WIN), :], ybuf, sem)
        cp.start()
        cp.wait()
        oh = jnp.where(colid == rel - c * ROW_WIN, 1.0, 0.0).astype(BF16)
        acc_scr[...] += _dot(oh, ybuf[...]) * gate
        return carry

    lax.fori_loop(1, nchunk, chunk, 0)

    @pl.when(e == pl.num_programs(1) - 1)
    def _():
        y = acc_scr[...]
        ms = jnp.mean(y * y, axis=-1, keepdims=True)
        o_ref[...] = y * lax.rsqrt(ms + RMS_EPS) * fn_ref[...]


def _combine(offs_flat, cnts_flat, aff, tau_row, cut_row, x2, tri_l, fnorm, ye, *, t, l_total):
    nr, d = x2.shape
    nb = nr // TOK_BLK
    e = ye.shape[0]

    def win_map(j, ei, o, c):
        return (ei, (o[ei * nb + j] // 16) * 16, 0)

    return pl.pallas_call(
        functools.partial(_combine_body, nb=nb, t=t, l_total=l_total),
        grid_spec=pltpu.PrefetchScalarGridSpec(
            num_scalar_prefetch=2,
            grid=(nb, e),
            in_specs=[
                pl.BlockSpec((TOK_BLK, LANES), lambda j, ei, o, c: (j, 0)),
                pl.BlockSpec((8, LANES), lambda j, ei, o, c: (0, 0)),
                pl.BlockSpec((8, LANES), lambda j, ei, o, c: (0, 0)),
                pl.BlockSpec((TOK_BLK, d), lambda j, ei, o, c: (j, 0)),
                pl.BlockSpec((TOK_BLK, TOK_BLK), lambda j, ei, o, c: (0, 0)),
                pl.BlockSpec((1, d), lambda j, ei, o, c: (0, 0)),
                pl.BlockSpec((pl.Squeezed(), pl.Element(ROW_WIN), pl.Element(d)), win_map),
                pl.BlockSpec(memory_space=pl.ANY),
            ],
            out_specs=pl.BlockSpec((TOK_BLK, d), lambda j, ei, o, c: (j, 0)),
            scratch_shapes=[
                pltpu.VMEM((TOK_BLK, LANES), F32),
                pltpu.VMEM((TOK_BLK, LANES), F32),
                pltpu.VMEM((TOK_BLK, d), F32),
                pltpu.VMEM((ROW_WIN, d), BF16),
                pltpu.SemaphoreType.DMA(()),
            ],
        ),
        out_shape=jax.ShapeDtypeStruct((nr, d), F32),
        compiler_params=_cparams(("parallel", "arbitrary")),
        name="ec_combine",
    )(offs_flat, cnts_flat, aff, tau_row, cut_row, x2, tri_l, fnorm, ye, ye)


def _rope_tables(positions):
    inv_freq = 1.0 / (ROPE_THETA ** (jnp.arange(0, HEAD_DIM, 2, dtype=F32) / HEAD_DIM))
    ang = positions.astype(F32)[:, None] * inv_freq[None, :]
    cos, sin = jnp.cos(ang), jnp.sin(ang)
    return (jnp.concatenate([cos, cos, cos, cos], axis=1),
            jnp.concatenate([-sin, -sin, sin, sin], axis=1))


def _permute_in_weights(w_in):
    l = jnp.arange(LANES)
    g = l // 32
    within = (g % 2) * HEAD_DIM + (g // 2) * 32 + l % 32
    cols = jnp.arange(PROJ_WIDTH)
    blk = cols // LANES
    permuted = blk * LANES + within[cols % LANES]
    cols = jnp.where((blk >= DQ_BLK) & (blk < DV_BLK), permuted, cols)
    return w_in[:, cols]


def _encode_group(x, shared):
    b, t, d = x.shape
    l_total = N_META + t
    n_total = b * l_total
    cap = EC_CAPACITY_FACTOR * n_total // N_EXPERTS
    rows = b * t
    xf = x.reshape(rows, d)

    cos, sin = _rope_tables(N_META + jnp.arange(t))
    proj, vt = _inproj(xf, shared["norm_mix"], shared["w_in"], shared["w_vt"], cos, sin, tm=512)
    proj = proj.reshape(b, t, PROJ_WIDTH)
    proj_meta = shared["proj_meta"]
    vt_meta = shared["vt_meta"]

    na = _na_attention(proj, proj_meta, shared["na_table"], shared["mb_stack"])
    lam = shared["lambda"]
    tk = min(2048, t)
    df = _diff_attention(proj, proj, vt, proj_meta, vt_meta, *lam, shared["subln"],
                         tq=min(1024, t), tk=tk, q_shared=False)
    q_meta = jnp.pad(proj_meta, ((0, LANES - N_META), (0, 0)))[None]
    df_meta = _diff_attention(q_meta, proj, vt, proj_meta, vt_meta, *lam, shared["subln"],
                              tq=LANES, tk=tk, q_shared=True)[:, :N_META]

    wr = (shared["wr_hi"], shared["wr_lo"])
    x2, h2, aff_t, aff = _outproj(xf, na.reshape(rows, NA_WIDTH), df.reshape(rows, DIFF_WIDTH), shared["w_out"],
                                  shared["norm_ffn"], *wr, tm=512, token_major=True)
    xm = jnp.tile(shared["meta_tokens"], (b, 1))
    nam = jnp.tile(shared["na_meta"], (b, 1))
    _, _, aff_t_meta = _outproj(xm, nam, df_meta.reshape(b * N_META, DIFF_WIDTH), shared["w_out"],
                                shared["norm_ffn"], *wr, tm=b * N_META, token_major=False)

    r = jnp.arange(rows, dtype=I32)
    gidx = (r + (r // t + 1) * N_META)[None, :]
    rm = jnp.arange(b * N_META, dtype=I32)
    gidx_meta = ((rm // N_META) * l_total + rm % N_META)[None, :]
    tau, cut, offs, cnts, rank_t = _select(aff_t, aff_t_meta, gidx, gidx_meta, shared["tri_u"],
                                           cap=cap, n_total=n_total)

    nb = rows // TOK_BLK
    offs_flat = offs[:, :nb].reshape(-1)
    cnts_flat = cnts[:, :nb].reshape(-1)
    cap_pad = -(-(cap + ROW_WIN) // 256) * 256
    ffn_tm = max(tm for tm in range(16, 1025, 16) if cap_pad % tm == 0)
    xe = _gather(offs_flat, cnts_flat, rank_t.reshape(N_EXPERTS, 1, rows), h2, cap_pad=cap_pad)
    ye = _ffn(xe, shared["w_gate"], shared["w_up"], shared["w_down"], tm=ffn_tm)

    pad = jnp.zeros((8, LANES - N_EXPERTS), I32)
    tau_row = jnp.concatenate([jnp.broadcast_to(tau[:, 0][None, :], (8, N_EXPERTS)), pad], axis=1)
    cut_row = jnp.concatenate([jnp.broadcast_to(cut[:, 0][None, :], (8, N_EXPERTS)), pad], axis=1)
    y = _combine(offs_flat, cnts_flat, aff, tau_row, cut_row, x2, shared["tri_l"], shared["final_norm"], ye,
                 t=t, l_total=l_total)
    return y.reshape(b, t, d)


def kernel(x_prompt, x_sample, meta_tokens, norm_mix, w_in, na_rpb, na_meta_bias, lambda_q1, lambda_k1,
           lambda_q2, lambda_k2, diff_subln, w_out, norm_ffn, w_router, w_gate, w_up, w_down, final_norm):
    d = x_prompt.shape[-1]
    wr = jnp.zeros((LANES, d), F32).at[:N_EXPERTS].set(w_router[0].T)
    wr_hi = wr.astype(BF16)
    mb = na_meta_bias[0].astype(F32)
    idx = jnp.arange(TOK_BLK)
    shared = {
        "meta_tokens": meta_tokens,
        "norm_mix": norm_mix[0][None, :],
        "w_in": _permute_in_weights(w_in[0]).astype(BF16),
        "lambda": (lambda_q1, lambda_k1, lambda_q2, lambda_k2),
        "subln": diff_subln.reshape(2 * HEAD_DIM, 1),
        "w_vt": w_in[0][:, DV_BLK * LANES:].T.astype(BF16),
        "w_out": w_out[0].astype(BF16),
        "norm_ffn": norm_ffn[0][None, :],
        "wr_hi": wr_hi,
        "wr_lo": (wr - wr_hi.astype(F32)).astype(BF16),
        "w_gate": w_gate[0].astype(BF16),
        "w_up": w_up[0].astype(BF16),
        "w_down": w_down[0].astype(BF16),
        "final_norm": final_norm[None, :],
        "mb_stack": jnp.repeat(mb, GRID_W, axis=0).reshape(NA_HEADS // 2, 2 * GRID_W, N_META),
        "tri_u": (idx[:, None] <= idx[None, :]).astype(BF16),
        "tri_l": (idx[:, None] >= idx[None, :]).astype(BF16),
    }
    cos_m, sin_m = _rope_tables(jnp.arange(N_META))
    shared["proj_meta"], shared["vt_meta"] = _inproj(meta_tokens, shared["norm_mix"], shared["w_in"], shared["w_vt"],
                                                     cos_m, sin_m, tm=N_META)
    mb_meta = jnp.repeat(mb, N_META, axis=0).reshape(NA_HEADS // 2, 2 * N_META, N_META)
    shared["na_meta"] = _na_meta(shared["proj_meta"], mb_meta)
    shared["na_table"] = _na_bias(na_rpb[0].astype(F32))
    return (_encode_group(x_prompt, shared), _encode_group(x_sample, shared))
```

```python
import functools
import math

import jax
import jax.numpy as jnp
from jax import lax
from jax.experimental import pallas as pl
from jax.experimental.pallas import tpu as pltpu

BF16 = jnp.bfloat16
F32 = jnp.float32
I32 = jnp.int32

N_META = 16
GRID_W = 64
HEAD_DIM = 64
NA_HEADS = 8
NA_WIDTH = NA_HEADS * HEAD_DIM
NA_KH = 8
NA_KW = 16
DIFF_HEADS = 4
DIFF_WIDTH = DIFF_HEADS * 2 * HEAD_DIM
PROJ_WIDTH = 3 * NA_WIDTH + 3 * DIFF_WIDTH
ROPE_THETA = 10000.0
N_EXPERTS = 16
EC_CAPACITY_FACTOR = 2
RMS_EPS = 1e-6
SUBLN_EPS = 1e-5
LAMBDA_INIT = 0.8 - 0.6 * math.exp(-0.3 * 0)
LOG2E = math.log2(math.e)

LANES = 128
NEG_BIG = -1e30
VMEM_LIMIT = 56 * 1024 * 1024

NAQ_BLK, NAK_BLK, NAV_BLK = 0, 4, 8
DQ_BLK, DK_BLK, DV_BLK = 12, 16, 20

ROW_WIN = 256
GATHER_WIN = 192
TOK_BLK = 1024
COMBINE_EXPERTS = 4


def _cparams(sem, vmem=VMEM_LIMIT):
    return pltpu.CompilerParams(dimension_semantics=sem, vmem_limit_bytes=vmem)


def _dot(a, b):
    return jnp.dot(a, b, preferred_element_type=F32)


def _dot_nt(a, b):
    return lax.dot_general(a, b, (((1,), (1,)), ((), ())), preferred_element_type=F32)


def _inproj_body(x_ref, g_ref, w_ref, cos_ref, sin_ref, o_ref, *, tn):
    x = x_ref[...]
    ms = jnp.mean(x * x, axis=-1, keepdims=True)
    h = (x * lax.rsqrt(ms + RMS_EPS) * g_ref[...]).astype(BF16)
    cos = cos_ref[...]
    sin = sin_ref[...]
    for c in range(PROJ_WIDTH // tn):
        lo = c * tn
        acc = _dot(h, w_ref[:, lo:lo + tn])
        for s in range(tn // LANES):
            blk = (lo + s * LANES) // LANES
            a = acc[:, s * LANES:(s + 1) * LANES]
            if DQ_BLK <= blk < DV_BLK:
                a = a * cos + pltpu.roll(a, 64, 1) * sin
            if blk < NAK_BLK:
                a = a * (HEAD_DIM ** -0.5)
            if DQ_BLK <= blk < DK_BLK:
                a = a * (HEAD_DIM ** -0.5 * LOG2E)
            o_ref[:, lo + s * LANES:lo + (s + 1) * LANES] = a.astype(BF16)


def _inproj(x, g, w, cos, sin, *, tm):
    rows, d = x.shape
    nt = cos.shape[0] // tm
    return pl.pallas_call(
        functools.partial(_inproj_body, tn=512),
        grid=(rows // tm,),
        in_specs=[
            pl.BlockSpec((tm, d), lambda i: (i, 0)),
            pl.BlockSpec((1, d), lambda i: (0, 0)),
            pl.BlockSpec((d, PROJ_WIDTH), lambda i: (0, 0)),
            pl.BlockSpec((tm, LANES), lambda i: (i % nt, 0)),
            pl.BlockSpec((tm, LANES), lambda i: (i % nt, 0)),
        ],
        out_specs=pl.BlockSpec((tm, PROJ_WIDTH), lambda i: (i, 0)),
        out_shape=jax.ShapeDtypeStruct((rows, PROJ_WIDTH), BF16),
        compiler_params=_cparams(("parallel",)),
        name="inproj",
    )(x, g, w, cos, sin)


def _na_bias_body(rpb_ref, o_ref):
    h = pl.program_id(0)
    c = lax.broadcasted_iota(I32, (GRID_W, GRID_W), 0)
    kc = lax.broadcasted_iota(I32, (GRID_W, GRID_W), 1)
    cs = jnp.clip(c - NA_KW // 2, 0, GRID_W - NA_KW)
    valid = (kc >= cs) & (kc < cs + NA_KW)
    jm = kc - c + NA_KW - 1
    n_dr = 2 * NA_KH - 1
    n_j = 2 * NA_KW - 1
    tiles = []
    for dr in range(n_dr):
        base = (h * n_dr + dr) * n_j

        def body(j, acc, base=base):
            return jnp.where(jm == j, rpb_ref[base + j], acc)

        t = lax.fori_loop(0, n_j, body, jnp.zeros((GRID_W, GRID_W), F32))
        tiles.append(jnp.where(valid, t, NEG_BIG))
    for d in range(NA_KH):
        o_ref[0, d] = jnp.concatenate([tiles[kr - d + NA_KH - 1] for kr in range(NA_KH)], axis=1)


def _na_bias(rpb):
    h = rpb.shape[0]
    return pl.pallas_call(
        _na_bias_body,
        grid_spec=pltpu.PrefetchScalarGridSpec(
            num_scalar_prefetch=1,
            grid=(h,),
            in_specs=[],
            out_specs=pl.BlockSpec((1, NA_KH, GRID_W, NA_KH * GRID_W), lambda i, r: (i, 0, 0, 0)),
        ),
        out_shape=jax.ShapeDtypeStruct((h, NA_KH, GRID_W, NA_KH * GRID_W), F32),
        compiler_params=_cparams(("arbitrary",)),
        name="na_bias",
    )(rpb.reshape(-1))


def _split_heads(q):
    lane = lax.broadcasted_iota(I32, q.shape, 1)
    zero = jnp.zeros_like(q)
    return jnp.concatenate([jnp.where(lane < HEAD_DIM, q, zero), jnp.where(lane < HEAD_DIM, zero, q)], axis=0)


def _merge_heads(o):
    n = o.shape[0] // 2
    lane = lax.broadcasted_iota(I32, (n, LANES), 1)
    return jnp.where(lane < HEAD_DIM, o[:n], o[n:])


def _na_body(q_ref, kp_ref, kc_ref, kn_ref, vp_ref, vc_ref, vn_ref, km_ref, vm_ref, tab_ref, mb_ref,
             o_ref, kbuf, vbuf, qq_scr, s_scr, p_scr, *, rows):
    rb = pl.program_id(2)
    blk = NA_KH * GRID_W
    kbuf[0:blk] = kp_ref[0]
    kbuf[blk:2 * blk] = kc_ref[0]
    kbuf[2 * blk:3 * blk] = kn_ref[0]
    vbuf[0:blk] = vp_ref[0]
    vbuf[blk:2 * blk] = vc_ref[0]
    vbuf[2 * blk:3 * blk] = vn_ref[0]
    two = 2 * GRID_W
    starts = []
    for i in range(NA_KH):
        r = rb * NA_KH + i
        rs = jnp.clip(r - NA_KH // 2, 0, rows - NA_KH)
        d = r - rs
        start = pl.multiple_of((rs - (rb - 1) * NA_KH) * GRID_W, GRID_W)
        starts.append(start)
        qq = _split_heads(q_ref[0, i * GRID_W:(i + 1) * GRID_W, :])
        qq_scr[i * two:(i + 1) * two, :] = qq
        bias = jnp.concatenate([tab_ref[0, d], tab_ref[1, d]], axis=0)
        s_scr[i * two:(i + 1) * two, :] = _dot_nt(qq, kbuf[pl.ds(start, blk), :]) + bias
    s = s_scr[...]
    sm = (_dot_nt(qq_scr[...], km_ref[...]).reshape(NA_KH, two, N_META) + mb_ref[0][None]).reshape(NA_KH * two, N_META)
    m = jnp.maximum(jnp.max(s, axis=1, keepdims=True), jnp.max(sm, axis=1, keepdims=True))
    p = jnp.exp(s - m)
    pm = jnp.exp(sm - m)
    l = jnp.sum(p, axis=1, keepdims=True) + jnp.sum(pm, axis=1, keepdims=True)
    p_scr[...] = p.astype(BF16)
    om = _dot(pm.astype(BF16), vm_ref[...])
    inv_l = 1.0 / l
    for i in range(NA_KH):
        rsl = slice(i * two, (i + 1) * two)
        o = (_dot(p_scr[rsl, :], vbuf[pl.ds(starts[i], blk), :]) + om[rsl]) * inv_l[rsl]
        o_ref[0, i * GRID_W:(i + 1) * GRID_W, :] = _merge_heads(o).astype(BF16)


def _na_attention(proj, proj_meta, table, mb_stack):
    b, t, _ = proj.shape
    rows = t // GRID_W
    nrb = rows // NA_KH
    blk = NA_KH * GRID_W
    hp = NA_HEADS // 2

    def kv_spec(col, shift):
        return pl.BlockSpec((1, blk, LANES),
                            lambda h, bi, rb: (bi, jnp.clip(rb + shift, 0, nrb - 1), col + h))

    return pl.pallas_call(
        functools.partial(_na_body, rows=rows),
        grid=(hp, b, nrb),
        in_specs=[
            pl.BlockSpec((1, blk, LANES), lambda h, bi, rb: (bi, rb, NAQ_BLK + h)),
            kv_spec(NAK_BLK, -1), kv_spec(NAK_BLK, 0), kv_spec(NAK_BLK, 1),
            kv_spec(NAV_BLK, -1), kv_spec(NAV_BLK, 0), kv_spec(NAV_BLK, 1),
            pl.BlockSpec((N_META, LANES), lambda h, bi, rb: (0, NAK_BLK + h)),
            pl.BlockSpec((N_META, LANES), lambda h, bi, rb: (0, NAV_BLK + h)),
            pl.BlockSpec((2, NA_KH, GRID_W, blk), lambda h, bi, rb: (h, 0, 0, 0)),
            pl.BlockSpec((1, LANES, N_META), lambda h, bi, rb: (h, 0, 0)),
        ],
        out_specs=pl.BlockSpec((1, blk, LANES), lambda h, bi, rb: (bi, rb, h)),
        out_shape=jax.ShapeDtypeStruct((b, t, NA_WIDTH), BF16),
        scratch_shapes=[pltpu.VMEM((3 * blk, LANES), BF16), pltpu.VMEM((3 * blk, LANES), BF16),
                        pltpu.VMEM((2 * blk, LANES), BF16), pltpu.VMEM((2 * blk, blk), F32),
                        pltpu.VMEM((2 * blk, blk), BF16)],
        compiler_params=_cparams(("arbitrary", "arbitrary", "arbitrary")),
        name="na_attention",
    )(proj, proj, proj, proj, proj, proj, proj, proj_meta, proj_meta, table, mb_stack)


def _na_meta_body(p_ref, mb_ref, o_ref):
    for h in range(NA_HEADS // 2):
        qq = _split_heads(p_ref[:, (NAQ_BLK + h) * LANES:(NAQ_BLK + h + 1) * LANES])
        km = p_ref[:, (NAK_BLK + h) * LANES:(NAK_BLK + h + 1) * LANES]
        vm = p_ref[:, (NAV_BLK + h) * LANES:(NAV_BLK + h + 1) * LANES]
        s = _dot_nt(qq, km) + mb_ref[h, 0:2 * N_META, :]
        m = jnp.max(s, axis=1, keepdims=True)
        p = jnp.exp(s - m)
        l = jnp.sum(p, axis=1, keepdims=True)
        o = _dot(p.astype(BF16), vm) / l
        o_ref[:, h * LANES:(h + 1) * LANES] = _merge_heads(o).astype(BF16)


def _na_meta(proj_meta, mb_meta):
    return pl.pallas_call(
        _na_meta_body,
        out_shape=jax.ShapeDtypeStruct((N_META, NA_WIDTH), BF16),
        name="na_meta",
    )(proj_meta, mb_meta)


def _split_maps(q):
    lane = lax.broadcasted_iota(I32, q.shape, 1)
    is0 = ((lane >> 5) & 1) == 0
    zero = jnp.zeros_like(q)
    return jnp.concatenate([jnp.where(is0, q, zero), jnp.where(is0, zero, q)], axis=0)


def _online_softmax(s_ref, p_ref, a_ref, m_scr, rows):
    s = s_ref[rows, :]
    m_prev = m_scr[rows]
    m_new = jnp.maximum(m_prev, jnp.max(s, axis=1, keepdims=True))
    a_ref[rows] = jnp.exp2(m_prev - m_new)
    m_scr[rows] = m_new
    p_ref[rows, :] = jnp.exp2((s - m_new).astype(BF16))


def _diff_body(q_ref, k_ref, kn_ref, vp_ref, v_ref, km_ref, vm_ref, lq1_ref, lk1_ref, lq2_ref, lk2_ref, sub_ref,
               o_ref, qq_scr, vx_scr, s0, s1, p0, p1, a0, a1, m_scr, acc_scr, *, tq, rc, nk):
    ki = pl.program_id(3)

    @pl.when(ki == 0)
    def _():
        qq = _split_maps(q_ref[0])
        qq_scr[...] = qq
        s = _dot_nt(qq, km_ref[...])
        m = jnp.max(s, axis=1, keepdims=True)
        p = jnp.exp2((s - m).astype(BF16))
        m_scr[...] = m
        vx_scr[:, LANES:] = jnp.ones((vx_scr.shape[0], LANES), BF16)
        vx_scr[0:N_META, 0:LANES] = vm_ref[...]
        acc_scr[...] = _dot(p, vx_scr[0:N_META, :])
        s0[...] = _dot_nt(qq, k_ref[0])
        p1[...] = jnp.zeros_like(p1)
        a1[...] = jnp.ones_like(a1)

    def step(s_cur, s_nxt, p_cur, p_prv, a_cur, a_prv):
        vx_scr[:, 0:LANES] = vp_ref[0]
        for g in range(2 * tq // rc):
            rows = slice(g * rc, (g + 1) * rc)
            s_nxt[rows, :] = _dot_nt(qq_scr[rows, :], kn_ref[0])
            acc_scr[rows, :] = a_prv[rows] * acc_scr[rows, :] + _dot(p_prv[rows, :], vx_scr[...])
            _online_softmax(s_cur, p_cur, a_cur, m_scr, rows)

    @pl.when(ki % 2 == 0)
    def _():
        step(s0, s1, p0, p1, a0, a1)

    @pl.when(ki % 2 == 1)
    def _():
        step(s1, s0, p1, p0, a1, a0)

    @pl.when(ki == nk - 1)
    def _():
        p_last, a_last = (p0, a0) if (nk - 1) % 2 == 0 else (p1, a1)
        vx_scr[:, 0:LANES] = v_ref[0]
        acc = a_last[...] * acc_scr[...] + _dot(p_last[...], vx_scr[...])
        lam = (jnp.exp(jnp.sum(lq1_ref[...] * lk1_ref[...], axis=1, keepdims=True))
               - jnp.exp(jnp.sum(lq2_ref[...] * lk2_ref[...], axis=1, keepdims=True)) + LAMBDA_INIT)
        o_all = acc[:, 0:LANES] / acc[:, LANES:]
        o = o_all[:tq] - lam * o_all[tq:]
        ms = jnp.mean(o * o, axis=-1, keepdims=True)
        o = o * lax.rsqrt(ms + SUBLN_EPS) * sub_ref[...] * (1.0 - LAMBDA_INIT)
        o_ref[0] = o.astype(BF16)


def _diff_attention(q_src, proj, proj_meta, lq1, lk1, lq2, lk2, subln, *, tq, tk, q_shared):
    b, t, _ = proj.shape
    tq_total = q_src.shape[1]
    nq = tq_total // tq
    nk = t // tk
    if q_shared:
        q_map = lambda bi, h, qi, ki: (0, qi, DQ_BLK + h)
    else:
        q_map = lambda bi, h, qi, ki: (bi, qi, DQ_BLK + h)
    vec = lambda n: pl.BlockSpec((1, n), lambda bi, h, qi, ki: (0, 0))
    return pl.pallas_call(
        functools.partial(_diff_body, tq=tq, rc=min(512, 2 * tq), nk=nk),
        grid=(b, DIFF_HEADS, nq, nk),
        in_specs=[
            pl.BlockSpec((1, tq, LANES), q_map),
            pl.BlockSpec((1, tk, LANES), lambda bi, h, qi, ki: (bi, ki, DK_BLK + h)),
            pl.BlockSpec((1, tk, LANES), lambda bi, h, qi, ki: (bi, jnp.minimum(ki + 1, nk - 1), DK_BLK + h)),
            pl.BlockSpec((1, tk, LANES), lambda bi, h, qi, ki: (bi, jnp.maximum(ki - 1, 0), DV_BLK + h)),
            pl.BlockSpec((1, tk, LANES), lambda bi, h, qi, ki: (bi, ki, DV_BLK + h)),
            pl.BlockSpec((N_META, LANES), lambda bi, h, qi, ki: (0, DK_BLK + h)),
            pl.BlockSpec((N_META, LANES), lambda bi, h, qi, ki: (0, DV_BLK + h)),
            vec(HEAD_DIM), vec(HEAD_DIM), vec(HEAD_DIM), vec(HEAD_DIM), vec(2 * HEAD_DIM),
        ],
        out_specs=pl.BlockSpec((1, tq, LANES), lambda bi, h, qi, ki: (bi, qi, h)),
        out_shape=jax.ShapeDtypeStruct((b, tq_total, DIFF_WIDTH), BF16),
        scratch_shapes=[
            pltpu.VMEM((2 * tq, LANES), BF16),
            pltpu.VMEM((tk, 2 * LANES), BF16),
            pltpu.VMEM((2 * tq, tk), F32), pltpu.VMEM((2 * tq, tk), F32),
            pltpu.VMEM((2 * tq, tk), BF16), pltpu.VMEM((2 * tq, tk), BF16),
            pltpu.VMEM((2 * tq, 1), F32), pltpu.VMEM((2 * tq, 1), F32),
            pltpu.VMEM((2 * tq, 1), F32),
            pltpu.VMEM((2 * tq, 2 * LANES), F32),
        ],
        compiler_params=_cparams(("parallel", "parallel", "parallel", "arbitrary")),
        name="diff_attention",
    )(q_src, proj, proj, proj, proj, proj_meta, proj_meta, lq1, lk1, lq2, lk2, subln)


def _outproj_body(x_ref, na_ref, df_ref, wo_ref, g_ref, wrh_ref, wrl_ref, x2_ref, h2_ref, afft_ref, *aff_ref):
    x2 = x_ref[...] + _dot(na_ref[...], wo_ref[0:NA_WIDTH, :]) + _dot(df_ref[...], wo_ref[NA_WIDTH:, :])
    x2_ref[...] = x2
    ms = jnp.mean(x2 * x2, axis=-1, keepdims=True)
    hf = x2 * lax.rsqrt(ms + RMS_EPS) * g_ref[...]
    hi = hf.astype(BF16)
    h2_ref[...] = hi
    lo = (hf - hi.astype(F32)).astype(BF16)
    logits = _dot_nt(wrh_ref[...], hi) + _dot_nt(wrh_ref[...], lo) + _dot_nt(wrl_ref[...], hi)
    row = lax.broadcasted_iota(I32, logits.shape, 0)
    logits = jnp.where(row < N_EXPERTS, logits, NEG_BIG)
    m = jnp.max(logits, axis=0, keepdims=True)
    e = jnp.exp(logits - m)
    aff = e / jnp.sum(e, axis=0, keepdims=True)
    afft_ref[...] = aff[0:N_EXPERTS]
    if aff_ref:
        aff_ref[0][...] = aff.T


def _outproj(x, na, df, wo, g, wrh, wrl, *, tm, token_major):
    rows, d = x.shape
    out_shape = [jax.ShapeDtypeStruct((rows, d), F32), jax.ShapeDtypeStruct((rows, d), BF16),
                 jax.ShapeDtypeStruct((N_EXPERTS, rows), F32)]
    out_specs = [pl.BlockSpec((tm, d), lambda i: (i, 0)), pl.BlockSpec((tm, d), lambda i: (i, 0)),
                 pl.BlockSpec((N_EXPERTS, tm), lambda i: (0, i))]
    if token_major:
        out_shape.append(jax.ShapeDtypeStruct((rows, LANES), F32))
        out_specs.append(pl.BlockSpec((tm, LANES), lambda i: (i, 0)))
    return pl.pallas_call(
        _outproj_body,
        grid=(rows // tm,),
        in_specs=[
            pl.BlockSpec((tm, d), lambda i: (i, 0)),
            pl.BlockSpec((tm, NA_WIDTH), lambda i: (i, 0)),
            pl.BlockSpec((tm, DIFF_WIDTH), lambda i: (i, 0)),
            pl.BlockSpec((NA_WIDTH + DIFF_WIDTH, d), lambda i: (0, 0)),
            pl.BlockSpec((1, d), lambda i: (0, 0)),
            pl.BlockSpec((LANES, d), lambda i: (0, 0)),
            pl.BlockSpec((LANES, d), lambda i: (0, 0)),
        ],
        out_specs=out_specs,
        out_shape=out_shape,
        compiler_params=_cparams(("parallel",)),
        name="outproj_router",
    )(x, na, df, wo, g, wrh, wrl)


def _select_body(ar_ref, am_ref, gr_ref, gm_ref, tri_ref, tau_ref, cut_ref, offs_ref, cnt_ref, rank_ref, sel_scr,
                 *, cap, nb, idx_bits):
    br = pltpu.bitcast(ar_ref[...], I32)
    bm = pltpu.bitcast(am_ref[...], I32)
    gr = gr_ref[...]
    gm = gm_ref[...]

    def count(mr, mm):
        return (jnp.sum(jnp.where(mr, 1.0, 0.0), axis=1, keepdims=True)
                + jnp.sum(jnp.where(mm, 1.0, 0.0), axis=1, keepdims=True))

    def value_bit(i, ans):
        cand = ans | jnp.left_shift(jnp.int32(1), 30 - i)
        return jnp.where(count(br >= cand, bm >= cand) >= cap, cand, ans)

    tau = lax.fori_loop(0, 31, value_bit, jnp.zeros((N_EXPERTS, 1), I32))
    need = cap - count(br > tau, bm > tau)
    eq_r = br == tau
    eq_m = bm == tau

    def index_bit(i, ans):
        cand = ans | jnp.left_shift(jnp.int32(1), idx_bits - 1 - i)
        return jnp.where(count(eq_r & (gr < cand), eq_m & (gm < cand)) < need, cand, ans)

    cut = lax.fori_loop(0, idx_bits, index_bit, jnp.zeros((N_EXPERTS, 1), I32))
    tau_ref[...] = jnp.broadcast_to(tau, tau_ref.shape)
    cut_ref[...] = jnp.broadcast_to(cut, cut_ref.shape)

    sel_scr[...] = jnp.where((br > tau) | (eq_r & (gr <= cut)), 1.0, 0.0)
    lane = lax.broadcasted_iota(I32, (N_EXPERTS, LANES), 1)

    def block_count(j, acc):
        start = pl.multiple_of(j * TOK_BLK, TOK_BLK)
        sel = sel_scr[:, pl.ds(start, TOK_BLK)]
        rank_ref[:, pl.ds(start, TOK_BLK)] = sel * _dot(sel.astype(BF16), tri_ref[...])
        c = jnp.sum(sel, axis=1, keepdims=True)
        return jnp.where(lane == j, c, acc)

    counts = lax.fori_loop(0, nb, block_count, jnp.zeros((N_EXPERTS, LANES), F32))
    incl = counts
    shift = 1
    while shift < LANES:
        incl = incl + jnp.where(lane >= shift, pltpu.roll(incl, shift, 1), 0.0)
        shift *= 2
    offs_ref[...] = (incl - counts).astype(I32)
    cnt_ref[...] = counts.astype(I32)


def _select(aff_t, aff_t_meta, gidx, gidx_meta, tri_u, *, cap, n_total):
    e, nr = aff_t.shape
    nb = nr // TOK_BLK
    assert nb <= LANES and nr % TOK_BLK == 0
    idx_bits = max(1, (n_total - 1).bit_length())
    shp = jax.ShapeDtypeStruct((e, LANES), I32)
    return pl.pallas_call(
        functools.partial(_select_body, cap=cap, nb=nb, idx_bits=idx_bits),
        out_shape=[shp, shp, shp, shp, jax.ShapeDtypeStruct((e, nr), F32)],
        scratch_shapes=[pltpu.VMEM((e, nr), F32)],
        compiler_params=pltpu.CompilerParams(vmem_limit_bytes=VMEM_LIMIT),
        name="ec_select",
    )(aff_t, aff_t_meta, gidx, gidx_meta, tri_u)


def _gather_body(offs_ref, cnts_ref, rank_ref, h_ref, xe_ref, *, nb):
    e = pl.program_id(0)
    j = pl.program_id(1)

    @pl.when(j == 0)
    def _():
        xe_ref[...] = jnp.zeros_like(xe_ref)

    rank = rank_ref[0].astype(I32)
    cnt = cnts_ref[e * nb + j]
    off = offs_ref[e * nb + j]
    off_al = (off // 16) * 16
    rel = jnp.where(rank > 0, rank - 1 + (off - off_al), -1)
    nchunk = (off - off_al + cnt + GATHER_WIN - 1) // GATHER_WIN
    hblk = h_ref[...]
    rowid = lax.broadcasted_iota(I32, (GATHER_WIN, TOK_BLK), 0)

    def chunk(c, carry):
        onehot = jnp.where(rowid == rel - c * GATHER_WIN, 1.0, 0.0).astype(BF16)
        g = _dot(onehot, hblk).astype(BF16)
        start = pl.multiple_of(off_al + c * GATHER_WIN, 16)
        xe_ref[0, pl.ds(start, GATHER_WIN), :] = xe_ref[0, pl.ds(start, GATHER_WIN), :] + g
        return carry

    lax.fori_loop(0, nchunk, chunk, 0)


def _gather(offs_flat, cnts_flat, rank3, h2, *, cap_pad):
    e = rank3.shape[0]
    nr, d = h2.shape
    nb = nr // TOK_BLK
    return pl.pallas_call(
        functools.partial(_gather_body, nb=nb),
        grid_spec=pltpu.PrefetchScalarGridSpec(
            num_scalar_prefetch=2,
            grid=(e, nb),
            in_specs=[
                pl.BlockSpec((1, 1, TOK_BLK), lambda ei, j, o, c: (ei, 0, j)),
                pl.BlockSpec((TOK_BLK, d), lambda ei, j, o, c: (j, 0)),
            ],
            out_specs=pl.BlockSpec((1, cap_pad, d), lambda ei, j, o, c: (ei, 0, 0)),
        ),
        out_shape=jax.ShapeDtypeStruct((e, cap_pad, d), BF16),
        compiler_params=_cparams(("parallel", "arbitrary")),
        name="ec_gather",
    )(offs_flat, cnts_flat, rank3, h2)


def _ffn_body(x_ref, wg_ref, wu_ref, wd_ref, o_ref, *, fc):
    x = x_ref[0]
    f = wg_ref.shape[2]
    acc = jnp.zeros((x.shape[0], wd_ref.shape[2]), F32)
    for c in range(f // fc):
        a = _dot(x, wg_ref[0, :, c * fc:(c + 1) * fc])
        b = _dot(x, wu_ref[0, :, c * fc:(c + 1) * fc])
        hmid = (a * jax.nn.sigmoid(a) * b).astype(BF16)
        acc = acc + _dot(hmid, wd_ref[0, c * fc:(c + 1) * fc, :])
    o_ref[0] = acc.astype(BF16)


def _ffn(xe, wg, wu, wd, *, tm):
    e, cap_pad, d = xe.shape
    f = wg.shape[2]
    return pl.pallas_call(
        functools.partial(_ffn_body, fc=min(512, f)),
        grid=(e, cap_pad // tm),
        in_specs=[
            pl.BlockSpec((1, tm, d), lambda ei, i: (ei, i, 0)),
            pl.BlockSpec((1, d, f), lambda ei, i: (ei, 0, 0)),
            pl.BlockSpec((1, d, f), lambda ei, i: (ei, 0, 0)),
            pl.BlockSpec((1, f, d), lambda ei, i: (ei, 0, 0)),
        ],
        out_specs=pl.BlockSpec((1, tm, d), lambda ei, i: (ei, i, 0)),
        out_shape=jax.ShapeDtypeStruct((e, cap_pad, d), BF16),
        compiler_params=_cparams(("parallel", "arbitrary")),
        name="ec_ffn",
    )(xe, wg, wu, wd)


def _combine_body(offs_ref, cnts_ref, aff_ref, tau_ref, cut_ref, x2_ref, tri_ref, fn_ref, *rest, nb, t, l_total):
    ywin_refs = rest[:COMBINE_EXPERTS]
    ye_hbm, o_ref, rank_scr, gate_scr, acc_scr, ybuf, sem = rest[COMBINE_EXPERTS:]
    j = pl.program_id(0)
    eg = pl.program_id(1)

    @pl.when(eg == 0)
    def _():
        aff = aff_ref[...]
        bits = pltpu.bitcast(aff, I32)
        tau = tau_ref[0:1, :]
        cut = cut_ref[0:1, :]
        row0 = j * TOK_BLK
        gidx = row0 + (row0 // t) * (l_total - t) + (l_total - t) + lax.broadcasted_iota(I32, bits.shape, 0)
        sel = (bits > tau) | ((bits == tau) & (gidx <= cut))
        sel = sel & (lax.broadcasted_iota(I32, bits.shape, 1) < N_EXPERTS)
        self32 = jnp.where(sel, 1.0, 0.0)
        incl = _dot(tri_ref[...], self32.astype(BF16))
        rank_scr[...] = jnp.where(sel, incl - 1.0, -1.0)
        gate_scr[...] = jnp.where(sel, aff, 0.0)
        acc_scr[...] = x2_ref[...]

    lane = lax.broadcasted_iota(I32, (TOK_BLK, LANES), 1)
    colid = lax.broadcasted_iota(I32, (TOK_BLK, ROW_WIN), 1)

    def placement(e):
        pick = lane == e
        rank = jnp.sum(jnp.where(pick, rank_scr[...], 0.0), axis=1, keepdims=True).astype(I32)
        gate = jnp.sum(jnp.where(pick, gate_scr[...], 0.0), axis=1, keepdims=True)
        off = offs_ref[e * nb + j]
        off_al = (off // 16) * 16
        rel = jnp.where(rank >= 0, rank + (off - off_al), -1)
        return rel, gate, off_al, (off - off_al + cnts_ref[e * nb + j] + ROW_WIN - 1) // ROW_WIN

    acc = acc_scr[...]
    for u in range(COMBINE_EXPERTS):
        rel, gate, _, _ = placement(eg * COMBINE_EXPERTS + u)
        onehot = jnp.where(colid == rel, 1.0, 0.0).astype(BF16)
        acc = acc + _dot(onehot, ywin_refs[u][...]) * gate
    acc_scr[...] = acc

    for u in range(COMBINE_EXPERTS):
        e = eg * COMBINE_EXPERTS + u
        nchunk = placement(e)[3]

        def chunk(c, carry, e=e):
            rel, gate, off_al, _ = placement(e)
            start = pl.multiple_of(off_al + c * ROW_WIN, 16)
            cp = pltpu.make_async_copy(ye_hbm.at[e, pl.ds(start, ROW_WIN), :], ybuf, sem)
            cp.start()
            cp.wait()
            oh = jnp.where(colid == rel - c * ROW_WIN, 1.0, 0.0).astype(BF16)
            acc_scr[...] += _dot(oh, ybuf[...]) * gate
            return carry

        lax.fori_loop(1, nchunk, chunk, 0)

    @pl.when(eg == pl.num_programs(1) - 1)
    def _():
        y = acc_scr[...]
        ms = jnp.mean(y * y, axis=-1, keepdims=True)
        o_ref[...] = y * lax.rsqrt(ms + RMS_EPS) * fn_ref[...]


def _combine(offs_flat, cnts_flat, aff, tau_row, cut_row, x2, tri_l, fnorm, ye, *, t, l_total):
    nr, d = x2.shape
    nb = nr // TOK_BLK
    e = ye.shape[0]

    def win_spec(u):
        def win_map(j, eg, o, c):
            ei = eg * COMBINE_EXPERTS + u
            return (ei, (o[ei * nb + j] // 16) * 16, 0)
        return pl.BlockSpec((pl.Squeezed(), pl.Element(ROW_WIN), pl.Element(d)), win_map)

    return pl.pallas_call(
        functools.partial(_combine_body, nb=nb, t=t, l_total=l_total),
        grid_spec=pltpu.PrefetchScalarGridSpec(
            num_scalar_prefetch=2,
            grid=(nb, e // COMBINE_EXPERTS),
            in_specs=[
                pl.BlockSpec((TOK_BLK, LANES), lambda j, ei, o, c: (j, 0)),
                pl.BlockSpec((8, LANES), lambda j, ei, o, c: (0, 0)),
                pl.BlockSpec((8, LANES), lambda j, ei, o, c: (0, 0)),
                pl.BlockSpec((TOK_BLK, d), lambda j, ei, o, c: (j, 0)),
                pl.BlockSpec((TOK_BLK, TOK_BLK), lambda j, ei, o, c: (0, 0)),
                pl.BlockSpec((1, d), lambda j, ei, o, c: (0, 0)),
                *[win_spec(u) for u in range(COMBINE_EXPERTS)],
                pl.BlockSpec(memory_space=pl.ANY),
            ],
            out_specs=pl.BlockSpec((TOK_BLK, d), lambda j, ei, o, c: (j, 0)),
            scratch_shapes=[
                pltpu.VMEM((TOK_BLK, LANES), F32),
                pltpu.VMEM((TOK_BLK, LANES), F32),
                pltpu.VMEM((TOK_BLK, d), F32),
                pltpu.VMEM((ROW_WIN, d), BF16),
                pltpu.SemaphoreType.DMA(()),
            ],
        ),
        out_shape=jax.ShapeDtypeStruct((nr, d), F32),
        compiler_params=_cparams(("parallel", "arbitrary")),
        name="ec_combine",
    )(offs_flat, cnts_flat, aff, tau_row, cut_row, x2, tri_l, fnorm, *([ye] * COMBINE_EXPERTS), ye)


def _rope_tables(positions):
    inv_freq = 1.0 / (ROPE_THETA ** (jnp.arange(0, HEAD_DIM, 2, dtype=F32) / HEAD_DIM))
    ang = positions.astype(F32)[:, None] * inv_freq[None, :]
    cos, sin = jnp.cos(ang), jnp.sin(ang)
    return (jnp.concatenate([cos, cos, cos, cos], axis=1),
            jnp.concatenate([-sin, -sin, sin, sin], axis=1))


def _permute_in_weights(w_in):
    l = jnp.arange(LANES)
    g = l // 32
    within = (g % 2) * HEAD_DIM + (g // 2) * 32 + l % 32
    cols = jnp.arange(PROJ_WIDTH)
    blk = cols // LANES
    permuted = blk * LANES + within[cols % LANES]
    cols = jnp.where((blk >= DQ_BLK) & (blk < DV_BLK), permuted, cols)
    return w_in[:, cols]


def _encode_group(x, shared):
    b, t, d = x.shape
    l_total = N_META + t
    n_total = b * l_total
    cap = EC_CAPACITY_FACTOR * n_total // N_EXPERTS
    rows = b * t
    xf = x.reshape(rows, d)

    cos, sin = _rope_tables(N_META + jnp.arange(t))
    proj = _inproj(xf, shared["norm_mix"], shared["w_in"], cos, sin, tm=512).reshape(b, t, PROJ_WIDTH)
    proj_meta = shared["proj_meta"]

    na = _na_attention(proj, proj_meta, shared["na_table"], shared["mb_stack"])
    lam = shared["lambda"]
    tk = min(1024, t)
    df = _diff_attention(proj, proj, proj_meta, *lam, shared["subln"], tq=min(1024, t), tk=tk, q_shared=False)
    df_meta = _diff_attention(proj_meta[None], proj, proj_meta, *lam, shared["subln"],
                              tq=N_META, tk=tk, q_shared=True)

    wr = (shared["wr_hi"], shared["wr_lo"])
    x2, h2, aff_t, aff = _outproj(xf, na.reshape(rows, NA_WIDTH), df.reshape(rows, DIFF_WIDTH), shared["w_out"],
                                  shared["norm_ffn"], *wr, tm=512, token_major=True)
    xm = jnp.tile(shared["meta_tokens"], (b, 1))
    nam = jnp.tile(shared["na_meta"], (b, 1))
    _, _, aff_t_meta = _outproj(xm, nam, df_meta.reshape(b * N_META, DIFF_WIDTH), shared["w_out"],
                                shared["norm_ffn"], *wr, tm=b * N_META, token_major=False)

    r = jnp.arange(rows, dtype=I32)
    gidx = (r + (r // t + 1) * N_META)[None, :]
    rm = jnp.arange(b * N_META, dtype=I32)
    gidx_meta = ((rm // N_META) * l_total + rm % N_META)[None, :]
    tau, cut, offs, cnts, rank_t = _select(aff_t, aff_t_meta, gidx, gidx_meta, shared["tri_u"],
                                           cap=cap, n_total=n_total)

    nb = rows // TOK_BLK
    offs_flat = offs[:, :nb].reshape(-1)
    cnts_flat = cnts[:, :nb].reshape(-1)
    cap_pad = -(-(cap + ROW_WIN) // 256) * 256
    ffn_tm = max(tm for tm in range(16, 1025, 16) if cap_pad % tm == 0)
    xe = _gather(offs_flat, cnts_flat, rank_t.reshape(N_EXPERTS, 1, rows), h2, cap_pad=cap_pad)
    ye = _ffn(xe, shared["w_gate"], shared["w_up"], shared["w_down"], tm=ffn_tm)

    pad = jnp.zeros((8, LANES - N_EXPERTS), I32)
    tau_row = jnp.concatenate([jnp.broadcast_to(tau[:, 0][None, :], (8, N_EXPERTS)), pad], axis=1)
    cut_row = jnp.concatenate([jnp.broadcast_to(cut[:, 0][None, :], (8, N_EXPERTS)), pad], axis=1)
    y = _combine(offs_flat, cnts_flat, aff, tau_row, cut_row, x2, shared["tri_l"], shared["final_norm"], ye,
                 t=t, l_total=l_total)
    return y.reshape(b, t, d)


def kernel(x_prompt, x_sample, meta_tokens, norm_mix, w_in, na_rpb, na_meta_bias, lambda_q1, lambda_k1,
           lambda_q2, lambda_k2, diff_subln, w_out, norm_ffn, w_router, w_gate, w_up, w_down, final_norm):
    d = x_prompt.shape[-1]
    wr = jnp.zeros((LANES, d), F32).at[:N_EXPERTS].set(w_router[0].T)
    wr_hi = wr.astype(BF16)
    mb = na_meta_bias[0].astype(F32)
    idx = jnp.arange(TOK_BLK)
    shared = {
        "meta_tokens": meta_tokens,
        "norm_mix": norm_mix[0][None, :],
        "w_in": _permute_in_weights(w_in[0]).astype(BF16),
        "lambda": (lambda_q1, lambda_k1, lambda_q2, lambda_k2),
        "subln": diff_subln,
        "w_out": w_out[0].astype(BF16),
        "norm_ffn": norm_ffn[0][None, :],
        "wr_hi": wr_hi,
        "wr_lo": (wr - wr_hi.astype(F32)).astype(BF16),
        "w_gate": w_gate[0].astype(BF16),
        "w_up": w_up[0].astype(BF16),
        "w_down": w_down[0].astype(BF16),
        "final_norm": final_norm[None, :],
        "mb_stack": jnp.repeat(mb, GRID_W, axis=0).reshape(NA_HEADS // 2, 2 * GRID_W, N_META),
        "tri_u": (idx[:, None] <= idx[None, :]).astype(BF16),
        "tri_l": (idx[:, None] >= idx[None, :]).astype(BF16),
    }
    cos_m, sin_m = _rope_tables(jnp.arange(N_META))
    shared["proj_meta"] = _inproj(meta_tokens, shared["norm_mix"], shared["w_in"], cos_m, sin_m, tm=N_META)
    mb_meta = jnp.repeat(mb, N_META, axis=0).reshape(NA_HEADS // 2, 2 * N_META, N_META)
    shared["na_meta"] = _na_meta(shared["proj_meta"], mb_meta)
    shared["na_table"] = _na_bias(na_rpb[0].astype(F32))
    return (_encode_group(x_prompt, shared), _encode_group(x_sample, shared))
```

```python
import functools
import math

import jax
import jax.numpy as jnp
from jax import lax
from jax.experimental import pallas as pl
from jax.experimental.pallas import tpu as pltpu

BF16 = jnp.bfloat16
F32 = jnp.float32
I32 = jnp.int32

N_META = 16
GRID_W = 64
HEAD_DIM = 64
NA_HEADS = 8
NA_WIDTH = NA_HEADS * HEAD_DIM
NA_KH = 8
NA_KW = 16
DIFF_HEADS = 4
DIFF_WIDTH = DIFF_HEADS * 2 * HEAD_DIM
PROJ_WIDTH = 3 * NA_WIDTH + 3 * DIFF_WIDTH
ROPE_THETA = 10000.0
N_EXPERTS = 16
EC_CAPACITY_FACTOR = 2
RMS_EPS = 1e-6
SUBLN_EPS = 1e-5
LAMBDA_INIT = 0.8 - 0.6 * math.exp(-0.3 * 0)
LOG2E = math.log2(math.e)

LANES = 128
NEG_BIG = -1e30
VMEM_LIMIT = 56 * 1024 * 1024

NAQ_BLK, NAK_BLK, NAV_BLK = 0, 4, 8
DQ_BLK, DK_BLK, DV_BLK = 12, 16, 20

ROW_WIN = 256
GATHER_WIN = 192
TOK_BLK = 1024
COMBINE_EXPERTS = 4


def _cparams(sem, vmem=VMEM_LIMIT):
    return pltpu.CompilerParams(dimension_semantics=sem, vmem_limit_bytes=vmem)


def _dot(a, b):
    return jnp.dot(a, b, preferred_element_type=F32)


def _dot_nt(a, b):
    return lax.dot_general(a, b, (((1,), (1,)), ((), ())), preferred_element_type=F32)


def _inproj_body(x_ref, g_ref, w_ref, cos_ref, sin_ref, o_ref, *, tn):
    x = x_ref[...]
    ms = jnp.mean(x * x, axis=-1, keepdims=True)
    h = (x * lax.rsqrt(ms + RMS_EPS) * g_ref[...]).astype(BF16)
    cos = cos_ref[...]
    sin = sin_ref[...]
    for c in range(PROJ_WIDTH // tn):
        lo = c * tn
        acc = _dot(h, w_ref[:, lo:lo + tn])
        for s in range(tn // LANES):
            blk = (lo + s * LANES) // LANES
            a = acc[:, s * LANES:(s + 1) * LANES]
            if DQ_BLK <= blk < DV_BLK:
                a = a * cos + pltpu.roll(a, 64, 1) * sin
            if blk < NAK_BLK:
                a = a * (HEAD_DIM ** -0.5)
            if DQ_BLK <= blk < DK_BLK:
                a = a * (HEAD_DIM ** -0.5 * LOG2E)
            o_ref[:, lo + s * LANES:lo + (s + 1) * LANES] = a.astype(BF16)


def _inproj(x, g, w, cos, sin, *, tm):
    rows, d = x.shape
    nt = cos.shape[0] // tm
    return pl.pallas_call(
        functools.partial(_inproj_body, tn=512),
        grid=(rows // tm,),
        in_specs=[
            pl.BlockSpec((tm, d), lambda i: (i, 0)),
            pl.BlockSpec((1, d), lambda i: (0, 0)),
            pl.BlockSpec((d, PROJ_WIDTH), lambda i: (0, 0)),
            pl.BlockSpec((tm, LANES), lambda i: (i % nt, 0)),
            pl.BlockSpec((tm, LANES), lambda i: (i % nt, 0)),
        ],
        out_specs=pl.BlockSpec((tm, PROJ_WIDTH), lambda i: (i, 0)),
        out_shape=jax.ShapeDtypeStruct((rows, PROJ_WIDTH), BF16),
        compiler_params=_cparams(("parallel",)),
        name="inproj",
    )(x, g, w, cos, sin)


def _na_bias_body(rpb_ref, o_ref):
    h = pl.program_id(0)
    c = lax.broadcasted_iota(I32, (GRID_W, GRID_W), 0)
    kc = lax.broadcasted_iota(I32, (GRID_W, GRID_W), 1)
    cs = jnp.clip(c - NA_KW // 2, 0, GRID_W - NA_KW)
    valid = (kc >= cs) & (kc < cs + NA_KW)
    jm = kc - c + NA_KW - 1
    n_dr = 2 * NA_KH - 1
    n_j = 2 * NA_KW - 1
    tiles = []
    for dr in range(n_dr):
        base = (h * n_dr + dr) * n_j

        def body(j, acc, base=base):
            return jnp.where(jm == j, rpb_ref[base + j], acc)

        t = lax.fori_loop(0, n_j, body, jnp.zeros((GRID_W, GRID_W), F32))
        tiles.append(jnp.where(valid, t, NEG_BIG))
    for d in range(NA_KH):
        o_ref[0, d] = jnp.concatenate([tiles[kr - d + NA_KH - 1] for kr in range(NA_KH)], axis=1)


def _na_bias(rpb):
    h = rpb.shape[0]
    return pl.pallas_call(
        _na_bias_body,
        grid_spec=pltpu.PrefetchScalarGridSpec(
            num_scalar_prefetch=1,
            grid=(h,),
            in_specs=[],
            out_specs=pl.BlockSpec((1, NA_KH, GRID_W, NA_KH * GRID_W), lambda i, r: (i, 0, 0, 0)),
        ),
        out_shape=jax.ShapeDtypeStruct((h, NA_KH, GRID_W, NA_KH * GRID_W), F32),
        compiler_params=_cparams(("arbitrary",)),
        name="na_bias",
    )(rpb.reshape(-1))


def _split_heads(q):
    lane = lax.broadcasted_iota(I32, q.shape, 1)
    zero = jnp.zeros_like(q)
    return jnp.concatenate([jnp.where(lane < HEAD_DIM, q, zero), jnp.where(lane < HEAD_DIM, zero, q)], axis=0)


def _merge_heads(o):
    n = o.shape[0] // 2
    lane = lax.broadcasted_iota(I32, (n, LANES), 1)
    return jnp.where(lane < HEAD_DIM, o[:n], o[n:])


def _na_body(q_ref, kp_ref, kc_ref, kn_ref, vp_ref, vc_ref, vn_ref, km_ref, vm_ref, tab_ref, mb_ref,
             o_ref, kbuf, vbuf, qq_scr, s_scr, p_scr, *, rows):
    rb = pl.program_id(2)
    blk = NA_KH * GRID_W
    kbuf[0:blk] = kp_ref[0]
    kbuf[blk:2 * blk] = kc_ref[0]
    kbuf[2 * blk:3 * blk] = kn_ref[0]
    vbuf[0:blk] = vp_ref[0]
    vbuf[blk:2 * blk] = vc_ref[0]
    vbuf[2 * blk:3 * blk] = vn_ref[0]
    two = 2 * GRID_W
    starts = []
    for i in range(NA_KH):
        r = rb * NA_KH + i
        rs = jnp.clip(r - NA_KH // 2, 0, rows - NA_KH)
        d = r - rs
        start = pl.multiple_of((rs - (rb - 1) * NA_KH) * GRID_W, GRID_W)
        starts.append(start)
        qq = _split_heads(q_ref[0, i * GRID_W:(i + 1) * GRID_W, :])
        qq_scr[i * two:(i + 1) * two, :] = qq
        bias = jnp.concatenate([tab_ref[0, d], tab_ref[1, d]], axis=0)
        s_scr[i * two:(i + 1) * two, :] = _dot_nt(qq, kbuf[pl.ds(start, blk), :]) + bias
    s = s_scr[...]
    sm = (_dot_nt(qq_scr[...], km_ref[...]).reshape(NA_KH, two, N_META) + mb_ref[0][None]).reshape(NA_KH * two, N_META)
    m = jnp.maximum(jnp.max(s, axis=1, keepdims=True), jnp.max(sm, axis=1, keepdims=True))
    p = jnp.exp(s - m)
    pm = jnp.exp(sm - m)
    l = jnp.sum(p, axis=1, keepdims=True) + jnp.sum(pm, axis=1, keepdims=True)
    p_scr[...] = p.astype(BF16)
    om = _dot(pm.astype(BF16), vm_ref[...])
    inv_l = 1.0 / l
    for i in range(NA_KH):
        rsl = slice(i * two, (i + 1) * two)
        o = (_dot(p_scr[rsl, :], vbuf[pl.ds(starts[i], blk), :]) + om[rsl]) * inv_l[rsl]
        o_ref[0, i * GRID_W:(i + 1) * GRID_W, :] = _merge_heads(o).astype(BF16)


def _na_attention(proj, proj_meta, table, mb_stack):
    b, t, _ = proj.shape
    rows = t // GRID_W
    nrb = rows // NA_KH
    blk = NA_KH * GRID_W
    hp = NA_HEADS // 2

    def kv_spec(col, shift):
        return pl.BlockSpec((1, blk, LANES),
                            lambda h, bi, rb: (bi, jnp.clip(rb + shift, 0, nrb - 1), col + h))

    return pl.pallas_call(
        functools.partial(_na_body, rows=rows),
        grid=(hp, b, nrb),
        in_specs=[
            pl.BlockSpec((1, blk, LANES), lambda h, bi, rb: (bi, rb, NAQ_BLK + h)),
            kv_spec(NAK_BLK, -1), kv_spec(NAK_BLK, 0), kv_spec(NAK_BLK, 1),
            kv_spec(NAV_BLK, -1), kv_spec(NAV_BLK, 0), kv_spec(NAV_BLK, 1),
            pl.BlockSpec((N_META, LANES), lambda h, bi, rb: (0, NAK_BLK + h)),
            pl.BlockSpec((N_META, LANES), lambda h, bi, rb: (0, NAV_BLK + h)),
            pl.BlockSpec((2, NA_KH, GRID_W, blk), lambda h, bi, rb: (h, 0, 0, 0)),
            pl.BlockSpec((1, LANES, N_META), lambda h, bi, rb: (h, 0, 0)),
        ],
        out_specs=pl.BlockSpec((1, blk, LANES), lambda h, bi, rb: (bi, rb, h)),
        out_shape=jax.ShapeDtypeStruct((b, t, NA_WIDTH), BF16),
        scratch_shapes=[pltpu.VMEM((3 * blk, LANES), BF16), pltpu.VMEM((3 * blk, LANES), BF16),
                        pltpu.VMEM((2 * blk, LANES), BF16), pltpu.VMEM((2 * blk, blk), F32),
                        pltpu.VMEM((2 * blk, blk), BF16)],
        compiler_params=_cparams(("arbitrary", "arbitrary", "arbitrary")),
        name="na_attention",
    )(proj, proj, proj, proj, proj, proj, proj, proj_meta, proj_meta, table, mb_stack)


def _na_meta_body(p_ref, mb_ref, o_ref):
    for h in range(NA_HEADS // 2):
        qq = _split_heads(p_ref[:, (NAQ_BLK + h) * LANES:(NAQ_BLK + h + 1) * LANES])
        km = p_ref[:, (NAK_BLK + h) * LANES:(NAK_BLK + h + 1) * LANES]
        vm = p_ref[:, (NAV_BLK + h) * LANES:(NAV_BLK + h + 1) * LANES]
        s = _dot_nt(qq, km) + mb_ref[h, 0:2 * N_META, :]
        m = jnp.max(s, axis=1, keepdims=True)
        p = jnp.exp(s - m)
        l = jnp.sum(p, axis=1, keepdims=True)
        o = _dot(p.astype(BF16), vm) / l
        o_ref[:, h * LANES:(h + 1) * LANES] = _merge_heads(o).astype(BF16)


def _na_meta(proj_meta, mb_meta):
    return pl.pallas_call(
        _na_meta_body,
        out_shape=jax.ShapeDtypeStruct((N_META, NA_WIDTH), BF16),
        name="na_meta",
    )(proj_meta, mb_meta)


def _split_maps(q):
    lane = lax.broadcasted_iota(I32, q.shape, 1)
    is0 = ((lane >> 5) & 1) == 0
    zero = jnp.zeros_like(q)
    return jnp.concatenate([jnp.where(is0, q, zero), jnp.where(is0, zero, q)], axis=0)


def _online_softmax(s_ref, p_ref, a_ref, m_scr, rows):
    s = s_ref[rows, :]
    m_prev = m_scr[rows]
    m_new = jnp.maximum(m_prev, jnp.max(s, axis=1, keepdims=True))
    a_ref[rows] = jnp.exp2(m_prev - m_new)
    m_scr[rows] = m_new
    p_ref[rows, :] = jnp.exp2((s - m_new).astype(BF16))


def _diff_body(q_ref, k_ref, v_ref, km_ref, vm_ref, lq1_ref, lk1_ref, lq2_ref, lk2_ref, sub_ref,
               o_ref, qq_scr, vx_scr, s0, s1, p0, p1, a0, a1, m_scr, acc_scr, *, tq, tk, rc, nk):
    def kv_rows(ref, ki):
        return ref[0, pl.ds(pl.multiple_of(ki * tk, tk), tk), :]

    qq = _split_maps(q_ref[0])
    qq_scr[...] = qq
    s = _dot_nt(qq, km_ref[...])
    m = jnp.max(s, axis=1, keepdims=True)
    m_scr[...] = m
    vx_scr[:, LANES:] = jnp.ones((vx_scr.shape[0], LANES), BF16)
    vx_scr[0:N_META, 0:LANES] = vm_ref[...]
    acc_scr[...] = _dot(jnp.exp2((s - m).astype(BF16)), vx_scr[0:N_META, :])
    s0[...] = _dot_nt(qq, k_ref[0, 0:tk, :])
    p1[...] = jnp.zeros_like(p1)
    a1[...] = jnp.ones_like(a1)

    def step(ki, s_cur, s_nxt, p_cur, p_prv, a_cur, a_prv):
        kn = kv_rows(k_ref, jnp.minimum(ki + 1, nk - 1))
        vx_scr[:, 0:LANES] = kv_rows(v_ref, jnp.maximum(ki - 1, 0))
        for g in range(2 * tq // rc):
            rows = slice(g * rc, (g + 1) * rc)
            s_nxt[rows, :] = _dot_nt(qq_scr[rows, :], kn)
            acc_scr[rows, :] = a_prv[rows] * acc_scr[rows, :] + _dot(p_prv[rows, :], vx_scr[...])
            _online_softmax(s_cur, p_cur, a_cur, m_scr, rows)

    def block(ki, carry):
        @pl.when(ki % 2 == 0)
        def _():
            step(ki, s0, s1, p0, p1, a0, a1)

        @pl.when(ki % 2 == 1)
        def _():
            step(ki, s1, s0, p1, p0, a1, a0)

        return carry

    lax.fori_loop(0, nk, block, 0)

    p_last, a_last = (p0, a0) if (nk - 1) % 2 == 0 else (p1, a1)
    vx_scr[:, 0:LANES] = v_ref[0, (nk - 1) * tk:nk * tk, :]
    acc = a_last[...] * acc_scr[...] + _dot(p_last[...], vx_scr[...])
    lam = (jnp.exp(jnp.sum(lq1_ref[...] * lk1_ref[...], axis=1, keepdims=True))
           - jnp.exp(jnp.sum(lq2_ref[...] * lk2_ref[...], axis=1, keepdims=True)) + LAMBDA_INIT)
    o_all = acc[:, 0:LANES] / acc[:, LANES:]
    o = o_all[:tq] - lam * o_all[tq:]
    ms = jnp.mean(o * o, axis=-1, keepdims=True)
    o = o * lax.rsqrt(ms + SUBLN_EPS) * sub_ref[...] * (1.0 - LAMBDA_INIT)
    o_ref[0] = o.astype(BF16)


def _diff_attention(q_src, proj, proj_meta, lq1, lk1, lq2, lk2, subln, *, tq, tk, q_shared):
    b, t, _ = proj.shape
    tq_total = q_src.shape[1]
    nq = tq_total // tq
    nk = t // tk
    if q_shared:
        q_map = lambda bi, h, qi: (0, qi, DQ_BLK + h)
    else:
        q_map = lambda bi, h, qi: (bi, qi, DQ_BLK + h)
    vec = lambda n: pl.BlockSpec((1, n), lambda bi, h, qi: (0, 0))
    return pl.pallas_call(
        functools.partial(_diff_body, tq=tq, tk=tk, rc=min(512, 2 * tq), nk=nk),
        grid=(b, DIFF_HEADS, nq),
        in_specs=[
            pl.BlockSpec((1, tq, LANES), q_map),
            pl.BlockSpec((1, t, LANES), lambda bi, h, qi: (bi, 0, DK_BLK + h)),
            pl.BlockSpec((1, t, LANES), lambda bi, h, qi: (bi, 0, DV_BLK + h)),
            pl.BlockSpec((N_META, LANES), lambda bi, h, qi: (0, DK_BLK + h)),
            pl.BlockSpec((N_META, LANES), lambda bi, h, qi: (0, DV_BLK + h)),
            vec(HEAD_DIM), vec(HEAD_DIM), vec(HEAD_DIM), vec(HEAD_DIM), vec(2 * HEAD_DIM),
        ],
        out_specs=pl.BlockSpec((1, tq, LANES), lambda bi, h, qi: (bi, qi, h)),
        out_shape=jax.ShapeDtypeStruct((b, tq_total, DIFF_WIDTH), BF16),
        scratch_shapes=[
            pltpu.VMEM((2 * tq, LANES), BF16),
            pltpu.VMEM((tk, 2 * LANES), BF16),
            pltpu.VMEM((2 * tq, tk), F32), pltpu.VMEM((2 * tq, tk), F32),
            pltpu.VMEM((2 * tq, tk), BF16), pltpu.VMEM((2 * tq, tk), BF16),
            pltpu.VMEM((2 * tq, 1), F32), pltpu.VMEM((2 * tq, 1), F32),
            pltpu.VMEM((2 * tq, 1), F32),
            pltpu.VMEM((2 * tq, 2 * LANES), F32),
        ],
        compiler_params=_cparams(("parallel", "parallel", "parallel")),
        name="diff_attention",
    )(q_src, proj, proj, proj_meta, proj_meta, lq1, lk1, lq2, lk2, subln)


def _outproj_body(x_ref, na_ref, df_ref, wo_ref, g_ref, wrh_ref, wrl_ref, x2_ref, h2_ref, afft_ref, *aff_ref):
    x2 = x_ref[...] + _dot(na_ref[...], wo_ref[0:NA_WIDTH, :]) + _dot(df_ref[...], wo_ref[NA_WIDTH:, :])
    x2_ref[...] = x2
    ms = jnp.mean(x2 * x2, axis=-1, keepdims=True)
    hf = x2 * lax.rsqrt(ms + RMS_EPS) * g_ref[...]
    hi = hf.astype(BF16)
    h2_ref[...] = hi
    lo = (hf - hi.astype(F32)).astype(BF16)
    logits = _dot_nt(wrh_ref[...], hi) + _dot_nt(wrh_ref[...], lo) + _dot_nt(wrl_ref[...], hi)
    row = lax.broadcasted_iota(I32, logits.shape, 0)
    logits = jnp.where(row < N_EXPERTS, logits, NEG_BIG)
    m = jnp.max(logits, axis=0, keepdims=True)
    e = jnp.exp(logits - m)
    aff = e / jnp.sum(e, axis=0, keepdims=True)
    afft_ref[...] = aff[0:N_EXPERTS]
    if aff_ref:
        aff_ref[0][...] = aff.T


def _outproj(x, na, df, wo, g, wrh, wrl, *, tm, token_major):
    rows, d = x.shape
    out_shape = [jax.ShapeDtypeStruct((rows, d), F32), jax.ShapeDtypeStruct((rows, d), BF16),
                 jax.ShapeDtypeStruct((N_EXPERTS, rows), F32)]
    out_specs = [pl.BlockSpec((tm, d), lambda i: (i, 0)), pl.BlockSpec((tm, d), lambda i: (i, 0)),
                 pl.BlockSpec((N_EXPERTS, tm), lambda i: (0, i))]
    if token_major:
        out_shape.append(jax.ShapeDtypeStruct((rows, LANES), F32))
        out_specs.append(pl.BlockSpec((tm, LANES), lambda i: (i, 0)))
    return pl.pallas_call(
        _outproj_body,
        grid=(rows // tm,),
        in_specs=[
            pl.BlockSpec((tm, d), lambda i: (i, 0)),
            pl.BlockSpec((tm, NA_WIDTH), lambda i: (i, 0)),
            pl.BlockSpec((tm, DIFF_WIDTH), lambda i: (i, 0)),
            pl.BlockSpec((NA_WIDTH + DIFF_WIDTH, d), lambda i: (0, 0)),
            pl.BlockSpec((1, d), lambda i: (0, 0)),
            pl.BlockSpec((LANES, d), lambda i: (0, 0)),
            pl.BlockSpec((LANES, d), lambda i: (0, 0)),
        ],
        out_specs=out_specs,
        out_shape=out_shape,
        compiler_params=_cparams(("parallel",)),
        name="outproj_router",
    )(x, na, df, wo, g, wrh, wrl)


def _select_body(ar_ref, am_ref, gr_ref, gm_ref, tri_ref, tau_ref, cut_ref, offs_ref, cnt_ref, rank_ref, sel_scr,
                 *, cap, nb, idx_bits):
    br = pltpu.bitcast(ar_ref[...], I32)
    bm = pltpu.bitcast(am_ref[...], I32)
    gr = gr_ref[...]
    gm = gm_ref[...]

    def count(mr, mm):
        return (jnp.sum(jnp.where(mr, 1.0, 0.0), axis=1, keepdims=True)
                + jnp.sum(jnp.where(mm, 1.0, 0.0), axis=1, keepdims=True))

    def value_bit(i, ans):
        cand = ans | jnp.left_shift(jnp.int32(1), 30 - i)
        return jnp.where(count(br >= cand, bm >= cand) >= cap, cand, ans)

    tau = lax.fori_loop(0, 31, value_bit, jnp.zeros((N_EXPERTS, 1), I32))
    need = cap - count(br > tau, bm > tau)
    eq_r = br == tau
    eq_m = bm == tau

    def index_bit(i, ans):
        cand = ans | jnp.left_shift(jnp.int32(1), idx_bits - 1 - i)
        return jnp.where(count(eq_r & (gr < cand), eq_m & (gm < cand)) < need, cand, ans)

    cut = lax.fori_loop(0, idx_bits, index_bit, jnp.zeros((N_EXPERTS, 1), I32))
    tau_ref[...] = jnp.broadcast_to(tau, tau_ref.shape)
    cut_ref[...] = jnp.broadcast_to(cut, cut_ref.shape)

    sel_scr[...] = jnp.where((br > tau) | (eq_r & (gr <= cut)), 1.0, 0.0)
    lane = lax.broadcasted_iota(I32, (N_EXPERTS, LANES), 1)

    def block_count(j, acc):
        start = pl.multiple_of(j * TOK_BLK, TOK_BLK)
        sel = sel_scr[:, pl.ds(start, TOK_BLK)]
        rank_ref[:, pl.ds(start, TOK_BLK)] = sel * _dot(sel.astype(BF16), tri_ref[...])
        c = jnp.sum(sel, axis=1, keepdims=True)
        return jnp.where(lane == j, c, acc)

    counts = lax.fori_loop(0, nb, block_count, jnp.zeros((N_EXPERTS, LANES), F32))
    incl = counts
    shift = 1
    while shift < LANES:
        incl = incl + jnp.where(lane >= shift, pltpu.roll(incl, shift, 1), 0.0)
        shift *= 2
    offs_ref[...] = (incl - counts).astype(I32)
    cnt_ref[...] = counts.astype(I32)


def _select(aff_t, aff_t_meta, gidx, gidx_meta, tri_u, *, cap, n_total):
    e, nr = aff_t.shape
    nb = nr // TOK_BLK
    assert nb <= LANES and nr % TOK_BLK == 0
    idx_bits = max(1, (n_total - 1).bit_length())
    shp = jax.ShapeDtypeStruct((e, LANES), I32)
    return pl.pallas_call(
        functools.partial(_select_body, cap=cap, nb=nb, idx_bits=idx_bits),
        out_shape=[shp, shp, shp, shp, jax.ShapeDtypeStruct((e, nr), F32)],
        scratch_shapes=[pltpu.VMEM((e, nr), F32)],
        compiler_params=pltpu.CompilerParams(vmem_limit_bytes=VMEM_LIMIT),
        name="ec_select",
    )(aff_t, aff_t_meta, gidx, gidx_meta, tri_u)


def _gather_body(offs_ref, cnts_ref, rank_ref, h_ref, xe_ref, *, nb):
    e = pl.program_id(0)
    j = pl.program_id(1)

    @pl.when(j == 0)
    def _():
        xe_ref[...] = jnp.zeros_like(xe_ref)

    rank = rank_ref[0].astype(I32)
    cnt = cnts_ref[e * nb + j]
    off = offs_ref[e * nb + j]
    off_al = (off // 16) * 16
    rel = jnp.where(rank > 0, rank - 1 + (off - off_al), -1)
    nchunk = (off - off_al + cnt + GATHER_WIN - 1) // GATHER_WIN
    hblk = h_ref[...]
    rowid = lax.broadcasted_iota(I32, (GATHER_WIN, TOK_BLK), 0)

    def chunk(c, carry):
        onehot = jnp.where(rowid == rel - c * GATHER_WIN, 1.0, 0.0).astype(BF16)
        g = _dot(onehot, hblk).astype(BF16)
        start = pl.multiple_of(off_al + c * GATHER_WIN, 16)
        xe_ref[0, pl.ds(start, GATHER_WIN), :] = xe_ref[0, pl.ds(start, GATHER_WIN), :] + g
        return carry

    lax.fori_loop(0, nchunk, chunk, 0)


def _gather(offs_flat, cnts_flat, rank3, h2, *, cap_pad):
    e = rank3.shape[0]
    nr, d = h2.shape
    nb = nr // TOK_BLK
    return pl.pallas_call(
        functools.partial(_gather_body, nb=nb),
        grid_spec=pltpu.PrefetchScalarGridSpec(
            num_scalar_prefetch=2,
            grid=(e, nb),
            in_specs=[
                pl.BlockSpec((1, 1, TOK_BLK), lambda ei, j, o, c: (ei, 0, j)),
                pl.BlockSpec((TOK_BLK, d), lambda ei, j, o, c: (j, 0)),
            ],
            out_specs=pl.BlockSpec((1, cap_pad, d), lambda ei, j, o, c: (ei, 0, 0)),
        ),
        out_shape=jax.ShapeDtypeStruct((e, cap_pad, d), BF16),
        compiler_params=_cparams(("parallel", "arbitrary")),
        name="ec_gather",
    )(offs_flat, cnts_flat, rank3, h2)


def _ffn_body(x_ref, wg_ref, wu_ref, wd_ref, o_ref, *, fc):
    x = x_ref[0]
    f = wg_ref.shape[2]
    acc = jnp.zeros((x.shape[0], wd_ref.shape[2]), F32)
    for c in range(f // fc):
        a = _dot(x, wg_ref[0, :, c * fc:(c + 1) * fc])
        b = _dot(x, wu_ref[0, :, c * fc:(c + 1) * fc])
        hmid = (a * jax.nn.sigmoid(a) * b).astype(BF16)
        acc = acc + _dot(hmid, wd_ref[0, c * fc:(c + 1) * fc, :])
    o_ref[0] = acc.astype(BF16)


def _ffn(xe, wg, wu, wd, *, tm):
    e, cap_pad, d = xe.shape
    f = wg.shape[2]
    return pl.pallas_call(
        functools.partial(_ffn_body, fc=min(512, f)),
        grid=(e, cap_pad // tm),
        in_specs=[
            pl.BlockSpec((1, tm, d), lambda ei, i: (ei, i, 0)),
            pl.BlockSpec((1, d, f), lambda ei, i: (ei, 0, 0)),
            pl.BlockSpec((1, d, f), lambda ei, i: (ei, 0, 0)),
            pl.BlockSpec((1, f, d), lambda ei, i: (ei, 0, 0)),
        ],
        out_specs=pl.BlockSpec((1, tm, d), lambda ei, i: (ei, i, 0)),
        out_shape=jax.ShapeDtypeStruct((e, cap_pad, d), BF16),
        compiler_params=_cparams(("parallel", "arbitrary")),
        name="ec_ffn",
    )(xe, wg, wu, wd)


def _combine_body(offs_ref, cnts_ref, aff_ref, tau_ref, cut_ref, x2_ref, tri_ref, fn_ref, *rest, nb, t, l_total):
    ywin_refs = rest[:COMBINE_EXPERTS]
    ye_hbm, o_ref, rank_scr, gate_scr, acc_scr, ybuf, sem = rest[COMBINE_EXPERTS:]
    j = pl.program_id(0)
    eg = pl.program_id(1)

    @pl.when(eg == 0)
    def _():
        aff = aff_ref[...]
        bits = pltpu.bitcast(aff, I32)
        tau = tau_ref[0:1, :]
        cut = cut_ref[0:1, :]
        row0 = j * TOK_BLK
        gidx = row0 + (row0 // t) * (l_total - t) + (l_total - t) + lax.broadcasted_iota(I32, bits.shape, 0)
        sel = (bits > tau) | ((bits == tau) & (gidx <= cut))
        sel = sel & (lax.broadcasted_iota(I32, bits.shape, 1) < N_EXPERTS)
        self32 = jnp.where(sel, 1.0, 0.0)
        incl = _dot(tri_ref[...], self32.astype(BF16))
        rank_scr[...] = jnp.where(sel, incl - 1.0, -1.0)
        gate_scr[...] = jnp.where(sel, aff, 0.0)
        acc_scr[...] = x2_ref[...]

    lane = lax.broadcasted_iota(I32, (TOK_BLK, LANES), 1)
    colid = lax.broadcasted_iota(I32, (TOK_BLK, ROW_WIN), 1)

    def placement(e):
        pick = lane == e
        rank = jnp.sum(jnp.where(pick, rank_scr[...], 0.0), axis=1, keepdims=True).astype(I32)
        gate = jnp.sum(jnp.where(pick, gate_scr[...], 0.0), axis=1, keepdims=True)
        off = offs_ref[e * nb + j]
        off_al = (off // 16) * 16
        rel = jnp.where(rank >= 0, rank + (off - off_al), -1)
        return rel, gate, off_al, (off - off_al + cnts_ref[e * nb + j] + ROW_WIN - 1) // ROW_WIN

    acc = acc_scr[...]
    for u in range(COMBINE_EXPERTS):
        rel, gate, _, _ = placement(eg * COMBINE_EXPERTS + u)
        onehot = jnp.where(colid == rel, 1.0, 0.0).astype(BF16)
        acc = acc + _dot(onehot, ywin_refs[u][...]) * gate
    acc_scr[...] = acc

    for u in range(COMBINE_EXPERTS):
        e = eg * COMBINE_EXPERTS + u
        nchunk = placement(e)[3]

        def chunk(c, carry, e=e):
            rel, gate, off_al, _ = placement(e)
            start = pl.multiple_of(off_al + c * ROW_WIN, 16)
            cp = pltpu.make_async_copy(ye_hbm.at[e, pl.ds(start, ROW_WIN), :], ybuf, sem)
            cp.start()
            cp.wait()
            oh = jnp.where(colid == rel - c * ROW_WIN, 1.0, 0.0).astype(BF16)
            acc_scr[...] += _dot(oh, ybuf[...]) * gate
            return carry

        lax.fori_loop(1, nchunk, chunk, 0)

    @pl.when(eg == pl.num_programs(1) - 1)
    def _():
        y = acc_scr[...]
        ms = jnp.mean(y * y, axis=-1, keepdims=True)
        o_ref[...] = y * lax.rsqrt(ms + RMS_EPS) * fn_ref[...]


def _combine(offs_flat, cnts_flat, aff, tau_row, cut_row, x2, tri_l, fnorm, ye, *, t, l_total):
    nr, d = x2.shape
    nb = nr // TOK_BLK
    e = ye.shape[0]

    def win_spec(u):
        def win_map(j, eg, o, c):
            ei = eg * COMBINE_EXPERTS + u
            return (ei, (o[ei * nb + j] // 16) * 16, 0)
        return pl.BlockSpec((pl.Squeezed(), pl.Element(ROW_WIN), pl.Element(d)), win_map)

    return pl.pallas_call(
        functools.partial(_combine_body, nb=nb, t=t, l_total=l_total),
        grid_spec=pltpu.PrefetchScalarGridSpec(
            num_scalar_prefetch=2,
            grid=(nb, e // COMBINE_EXPERTS),
            in_specs=[
                pl.BlockSpec((TOK_BLK, LANES), lambda j, ei, o, c: (j, 0)),
                pl.BlockSpec((8, LANES), lambda j, ei, o, c: (0, 0)),
                pl.BlockSpec((8, LANES), lambda j, ei, o, c: (0, 0)),
                pl.BlockSpec((TOK_BLK, d), lambda j, ei, o, c: (j, 0)),
                pl.BlockSpec((TOK_BLK, TOK_BLK), lambda j, ei, o, c: (0, 0)),
                pl.BlockSpec((1, d), lambda j, ei, o, c: (0, 0)),
                *[win_spec(u) for u in range(COMBINE_EXPERTS)],
                pl.BlockSpec(memory_space=pl.ANY),
            ],
            out_specs=pl.BlockSpec((TOK_BLK, d), lambda j, ei, o, c: (j, 0)),
            scratch_shapes=[
                pltpu.VMEM((TOK_BLK, LANES), F32),
                pltpu.VMEM((TOK_BLK, LANES), F32),
                pltpu.VMEM((TOK_BLK, d), F32),
                pltpu.VMEM((ROW_WIN, d), BF16),
                pltpu.SemaphoreType.DMA(()),
            ],
        ),
        out_shape=jax.ShapeDtypeStruct((nr, d), F32),
        compiler_params=_cparams(("parallel", "arbitrary")),
        name="ec_combine",
    )(offs_flat, cnts_flat, aff, tau_row, cut_row, x2, tri_l, fnorm, *([ye] * COMBINE_EXPERTS), ye)


def _rope_tables(positions):
    inv_freq = 1.0 / (ROPE_THETA ** (jnp.arange(0, HEAD_DIM, 2, dtype=F32) / HEAD_DIM))
    ang = positions.astype(F32)[:, None] * inv_freq[None, :]
    cos, sin = jnp.cos(ang), jnp.sin(ang)
    return (jnp.concatenate([cos, cos, cos, cos], axis=1),
            jnp.concatenate([-sin, -sin, sin, sin], axis=1))


def _permute_in_weights(w_in):
    l = jnp.arange(LANES)
    g = l // 32
    within = (g % 2) * HEAD_DIM + (g // 2) * 32 + l % 32
    cols = jnp.arange(PROJ_WIDTH)
    blk = cols // LANES
    permuted = blk * LANES + within[cols % LANES]
    cols = jnp.where((blk >= DQ_BLK) & (blk < DV_BLK), permuted, cols)
    return w_in[:, cols]


def _encode_group(x, shared):
    b, t, d = x.shape
    l_total = N_META + t
    n_total = b * l_total
    cap = EC_CAPACITY_FACTOR * n_total // N_EXPERTS
    rows = b * t
    xf = x.reshape(rows, d)

    cos, sin = _rope_tables(N_META + jnp.arange(t))
    proj = _inproj(xf, shared["norm_mix"], shared["w_in"], cos, sin, tm=512).reshape(b, t, PROJ_WIDTH)
    proj_meta = shared["proj_meta"]

    na = _na_attention(proj, proj_meta, shared["na_table"], shared["mb_stack"])
    lam = shared["lambda"]
    tk = min(1024, t)
    df = _diff_attention(proj, proj, proj_meta, *lam, shared["subln"], tq=min(1024, t), tk=tk, q_shared=False)
    df_meta = _diff_attention(proj_meta[None], proj, proj_meta, *lam, shared["subln"],
                              tq=N_META, tk=tk, q_shared=True)

    wr = (shared["wr_hi"], shared["wr_lo"])
    x2, h2, aff_t, aff = _outproj(xf, na.reshape(rows, NA_WIDTH), df.reshape(rows, DIFF_WIDTH), shared["w_out"],
                                  shared["norm_ffn"], *wr, tm=512, token_major=True)
    xm = jnp.tile(shared["meta_tokens"], (b, 1))
    nam = jnp.tile(shared["na_meta"], (b, 1))
    _, _, aff_t_meta = _outproj(xm, nam, df_meta.reshape(b * N_META, DIFF_WIDTH), shared["w_out"],
                                shared["norm_ffn"], *wr, tm=b * N_META, token_major=False)

    r = jnp.arange(rows, dtype=I32)
    gidx = (r + (r // t + 1) * N_META)[None, :]
    rm = jnp.arange(b * N_META, dtype=I32)
    gidx_meta = ((rm // N_META) * l_total + rm % N_META)[None, :]
    tau, cut, offs, cnts, rank_t = _select(aff_t, aff_t_meta, gidx, gidx_meta, shared["tri_u"],
                                           cap=cap, n_total=n_total)

    nb = rows // TOK_BLK
    offs_flat = offs[:, :nb].reshape(-1)
    cnts_flat = cnts[:, :nb].reshape(-1)
    cap_pad = -(-(cap + ROW_WIN) // 256) * 256
    ffn_tm = max(tm for tm in range(16, 1025, 16) if cap_pad % tm == 0)
    xe = _gather(offs_flat, cnts_flat, rank_t.reshape(N_EXPERTS, 1, rows), h2, cap_pad=cap_pad)
    ye = _ffn(xe, shared["w_gate"], shared["w_up"], shared["w_down"], tm=ffn_tm)

    pad = jnp.zeros((8, LANES - N_EXPERTS), I32)
    tau_row = jnp.concatenate([jnp.broadcast_to(tau[:, 0][None, :], (8, N_EXPERTS)), pad], axis=1)
    cut_row = jnp.concatenate([jnp.broadcast_to(cut[:, 0][None, :], (8, N_EXPERTS)), pad], axis=1)
    y = _combine(offs_flat, cnts_flat, aff, tau_row, cut_row, x2, shared["tri_l"], shared["final_norm"], ye,
                 t=t, l_total=l_total)
    return y.reshape(b, t, d)


def kernel(x_prompt, x_sample, meta_tokens, norm_mix, w_in, na_rpb, na_meta_bias, lambda_q1, lambda_k1,
           lambda_q2, lambda_k2, diff_subln, w_out, norm_ffn, w_router, w_gate, w_up, w_down, final_norm):
    d = x_prompt.shape[-1]
    wr = jnp.zeros((LANES, d), F32).at[:N_EXPERTS].set(w_router[0].T)
    wr_hi = wr.astype(BF16)
    mb = na_meta_bias[0].astype(F32)
    idx = jnp.arange(TOK_BLK)
    shared = {
        "meta_tokens": meta_tokens,
        "norm_mix": norm_mix[0][None, :],
        "w_in": _permute_in_weights(w_in[0]).astype(BF16),
        "lambda": (lambda_q1, lambda_k1, lambda_q2, lambda_k2),
        "subln": diff_subln,
        "w_out": w_out[0].astype(BF16),
        "norm_ffn": norm_ffn[0][None, :],
        "wr_hi": wr_hi,
        "wr_lo": (wr - wr_hi.astype(F32)).astype(BF16),
        "w_gate": w_gate[0].astype(BF16),
        "w_up": w_up[0].astype(BF16),
        "w_down": w_down[0].astype(BF16),
        "final_norm": final_norm[None, :],
        "mb_stack": jnp.repeat(mb, GRID_W, axis=0).reshape(NA_HEADS // 2, 2 * GRID_W, N_META),
        "tri_u": (idx[:, None] <= idx[None, :]).astype(BF16),
        "tri_l": (idx[:, None] >= idx[None, :]).astype(BF16),
    }
    cos_m, sin_m = _rope_tables(jnp.arange(N_META))
    shared["proj_meta"] = _inproj(meta_tokens, shared["norm_mix"], shared["w_in"], cos_m, sin_m, tm=N_META)
    mb_meta = jnp.repeat(mb, N_META, axis=0).reshape(NA_HEADS // 2, 2 * N_META, N_META)
    shared["na_meta"] = _na_meta(shared["proj_meta"], mb_meta)
    shared["na_table"] = _na_bias(na_rpb[0].astype(F32))
    return (_encode_group(x_prompt, shared), _encode_group(x_sample, shared))
```

```python
import functools
import math

import jax
import jax.numpy as jnp
from jax import lax
from jax.experimental import pallas as pl
from jax.experimental.pallas import tpu as pltpu

BF16 = jnp.bfloat16
F32 = jnp.float32
I32 = jnp.int32

N_META = 16
GRID_W = 64
HEAD_DIM = 64
NA_HEADS = 8
NA_WIDTH = NA_HEADS * HEAD_DIM
NA_KH = 8
NA_KW = 16
DIFF_HEADS = 4
DIFF_WIDTH = DIFF_HEADS * 2 * HEAD_DIM
PROJ_WIDTH = 3 * NA_WIDTH + 3 * DIFF_WIDTH
ROPE_THETA = 10000.0
N_EXPERTS = 16
EC_CAPACITY_FACTOR = 2
RMS_EPS = 1e-6
SUBLN_EPS = 1e-5
LAMBDA_INIT = 0.8 - 0.6 * math.exp(-0.3 * 0)
LOG2E = math.log2(math.e)

LANES = 128
NEG_BIG = -1e30
VMEM_LIMIT = 56 * 1024 * 1024

NAQ_BLK, NAK_BLK, NAV_BLK = 0, 4, 8
DQ_BLK, DK_BLK, DV_BLK = 12, 16, 20

ROW_WIN = 256
GATHER_WIN = 192
GATHER_RING = 3
TOK_BLK = 1024
COMBINE_EXPERTS = 4


def _cparams(sem, vmem=VMEM_LIMIT):
    return pltpu.CompilerParams(dimension_semantics=sem, vmem_limit_bytes=vmem)


def _dot(a, b):
    return jnp.dot(a, b, preferred_element_type=F32)


def _dot_nt(a, b):
    return lax.dot_general(a, b, (((1,), (1,)), ((), ())), preferred_element_type=F32)


def _inproj_body(x_ref, g_ref, w_ref, cos_ref, sin_ref, o_ref, *, tn):
    x = x_ref[...]
    ms = jnp.mean(x * x, axis=-1, keepdims=True)
    h = (x * lax.rsqrt(ms + RMS_EPS) * g_ref[...]).astype(BF16)
    cos = cos_ref[...]
    sin = sin_ref[...]
    for c in range(PROJ_WIDTH // tn):
        lo = c * tn
        acc = _dot(h, w_ref[:, lo:lo + tn])
        for s in range(tn // LANES):
            blk = (lo + s * LANES) // LANES
            a = acc[:, s * LANES:(s + 1) * LANES]
            if DQ_BLK <= blk < DV_BLK:
                a = a * cos + pltpu.roll(a, 64, 1) * sin
            if blk < NAK_BLK:
                a = a * (HEAD_DIM ** -0.5)
            if DQ_BLK <= blk < DK_BLK:
                a = a * (HEAD_DIM ** -0.5 * LOG2E)
            o_ref[:, lo + s * LANES:lo + (s + 1) * LANES] = a.astype(BF16)


def _inproj(x, g, w, cos, sin, *, tm):
    rows, d = x.shape
    nt = cos.shape[0] // tm
    return pl.pallas_call(
        functools.partial(_inproj_body, tn=512),
        grid=(rows // tm,),
        in_specs=[
            pl.BlockSpec((tm, d), lambda i: (i, 0)),
            pl.BlockSpec((1, d), lambda i: (0, 0)),
            pl.BlockSpec((d, PROJ_WIDTH), lambda i: (0, 0)),
            pl.BlockSpec((tm, LANES), lambda i: (i % nt, 0)),
            pl.BlockSpec((tm, LANES), lambda i: (i % nt, 0)),
        ],
        out_specs=pl.BlockSpec((tm, PROJ_WIDTH), lambda i: (i, 0)),
        out_shape=jax.ShapeDtypeStruct((rows, PROJ_WIDTH), BF16),
        compiler_params=_cparams(("parallel",)),
        name="inproj",
    )(x, g, w, cos, sin)


def _na_bias_body(rpb_ref, o_ref):
    h = pl.program_id(0)
    c = lax.broadcasted_iota(I32, (GRID_W, GRID_W), 0)
    kc = lax.broadcasted_iota(I32, (GRID_W, GRID_W), 1)
    cs = jnp.clip(c - NA_KW // 2, 0, GRID_W - NA_KW)
    valid = (kc >= cs) & (kc < cs + NA_KW)
    jm = kc - c + NA_KW - 1
    n_dr = 2 * NA_KH - 1
    n_j = 2 * NA_KW - 1
    tiles = []
    for dr in range(n_dr):
        base = (h * n_dr + dr) * n_j

        def body(j, acc, base=base):
            return jnp.where(jm == j, rpb_ref[base + j], acc)

        t = lax.fori_loop(0, n_j, body, jnp.zeros((GRID_W, GRID_W), F32))
        tiles.append(jnp.where(valid, t, NEG_BIG))
    for d in range(NA_KH):
        o_ref[0, d] = jnp.concatenate([tiles[kr - d + NA_KH - 1] for kr in range(NA_KH)], axis=1)


def _na_bias(rpb):
    h = rpb.shape[0]
    return pl.pallas_call(
        _na_bias_body,
        grid_spec=pltpu.PrefetchScalarGridSpec(
            num_scalar_prefetch=1,
            grid=(h,),
            in_specs=[],
            out_specs=pl.BlockSpec((1, NA_KH, GRID_W, NA_KH * GRID_W), lambda i, r: (i, 0, 0, 0)),
        ),
        out_shape=jax.ShapeDtypeStruct((h, NA_KH, GRID_W, NA_KH * GRID_W), F32),
        compiler_params=_cparams(("arbitrary",)),
        name="na_bias",
    )(rpb.reshape(-1))


def _split_heads(q):
    lane = lax.broadcasted_iota(I32, q.shape, 1)
    zero = jnp.zeros_like(q)
    return jnp.concatenate([jnp.where(lane < HEAD_DIM, q, zero), jnp.where(lane < HEAD_DIM, zero, q)], axis=0)


def _merge_heads(o):
    n = o.shape[0] // 2
    lane = lax.broadcasted_iota(I32, (n, LANES), 1)
    return jnp.where(lane < HEAD_DIM, o[:n], o[n:])


def _na_body(q_ref, kp_ref, kc_ref, kn_ref, vp_ref, vc_ref, vn_ref, km_ref, vm_ref, tab_ref, mb_ref,
             o_ref, kbuf, vbuf, qq_scr, s_scr, p_scr, *, rows):
    rb = pl.program_id(2)
    blk = NA_KH * GRID_W
    kbuf[0:blk] = kp_ref[0]
    kbuf[blk:2 * blk] = kc_ref[0]
    kbuf[2 * blk:3 * blk] = kn_ref[0]
    vbuf[0:blk] = vp_ref[0]
    vbuf[blk:2 * blk] = vc_ref[0]
    vbuf[2 * blk:3 * blk] = vn_ref[0]
    two = 2 * GRID_W
    starts = []
    for i in range(NA_KH):
        r = rb * NA_KH + i
        rs = jnp.clip(r - NA_KH // 2, 0, rows - NA_KH)
        d = r - rs
        start = pl.multiple_of((rs - (rb - 1) * NA_KH) * GRID_W, GRID_W)
        starts.append(start)
        qq = _split_heads(q_ref[0, i * GRID_W:(i + 1) * GRID_W, :])
        qq_scr[i * two:(i + 1) * two, :] = qq
        bias = jnp.concatenate([tab_ref[0, d], tab_ref[1, d]], axis=0)
        s_scr[i * two:(i + 1) * two, :] = _dot_nt(qq, kbuf[pl.ds(start, blk), :]) + bias
    s = s_scr[...]
    sm = (_dot_nt(qq_scr[...], km_ref[...]).reshape(NA_KH, two, N_META) + mb_ref[0][None]).reshape(NA_KH * two, N_META)
    m = jnp.maximum(jnp.max(s, axis=1, keepdims=True), jnp.max(sm, axis=1, keepdims=True))
    p = jnp.exp(s - m)
    pm = jnp.exp(sm - m)
    l = jnp.sum(p, axis=1, keepdims=True) + jnp.sum(pm, axis=1, keepdims=True)
    p_scr[...] = p.astype(BF16)
    om = _dot(pm.astype(BF16), vm_ref[...])
    inv_l = 1.0 / l
    for i in range(NA_KH):
        rsl = slice(i * two, (i + 1) * two)
        o = (_dot(p_scr[rsl, :], vbuf[pl.ds(starts[i], blk), :]) + om[rsl]) * inv_l[rsl]
        o_ref[0, i * GRID_W:(i + 1) * GRID_W, :] = _merge_heads(o).astype(BF16)


def _na_attention(proj, proj_meta, table, mb_stack):
    b, t, _ = proj.shape
    rows = t // GRID_W
    nrb = rows // NA_KH
    blk = NA_KH * GRID_W
    hp = NA_HEADS // 2

    def kv_spec(col, shift):
        return pl.BlockSpec((1, blk, LANES),
                            lambda h, bi, rb: (bi, jnp.clip(rb + shift, 0, nrb - 1), col + h))

    return pl.pallas_call(
        functools.partial(_na_body, rows=rows),
        grid=(hp, b, nrb),
        in_specs=[
            pl.BlockSpec((1, blk, LANES), lambda h, bi, rb: (bi, rb, NAQ_BLK + h)),
            kv_spec(NAK_BLK, -1), kv_spec(NAK_BLK, 0), kv_spec(NAK_BLK, 1),
            kv_spec(NAV_BLK, -1), kv_spec(NAV_BLK, 0), kv_spec(NAV_BLK, 1),
            pl.BlockSpec((N_META, LANES), lambda h, bi, rb: (0, NAK_BLK + h)),
            pl.BlockSpec((N_META, LANES), lambda h, bi, rb: (0, NAV_BLK + h)),
            pl.BlockSpec((2, NA_KH, GRID_W, blk), lambda h, bi, rb: (h, 0, 0, 0)),
            pl.BlockSpec((1, LANES, N_META), lambda h, bi, rb: (h, 0, 0)),
        ],
        out_specs=pl.BlockSpec((1, blk, LANES), lambda h, bi, rb: (bi, rb, h)),
        out_shape=jax.ShapeDtypeStruct((b, t, NA_WIDTH), BF16),
        scratch_shapes=[pltpu.VMEM((3 * blk, LANES), BF16), pltpu.VMEM((3 * blk, LANES), BF16),
                        pltpu.VMEM((2 * blk, LANES), BF16), pltpu.VMEM((2 * blk, blk), F32),
                        pltpu.VMEM((2 * blk, blk), BF16)],
        compiler_params=_cparams(("arbitrary", "arbitrary", "arbitrary")),
        name="na_attention",
    )(proj, proj, proj, proj, proj, proj, proj, proj_meta, proj_meta, table, mb_stack)


def _na_meta_body(p_ref, mb_ref, o_ref):
    for h in range(NA_HEADS // 2):
        qq = _split_heads(p_ref[:, (NAQ_BLK + h) * LANES:(NAQ_BLK + h + 1) * LANES])
        km = p_ref[:, (NAK_BLK + h) * LANES:(NAK_BLK + h + 1) * LANES]
        vm = p_ref[:, (NAV_BLK + h) * LANES:(NAV_BLK + h + 1) * LANES]
        s = _dot_nt(qq, km) + mb_ref[h, 0:2 * N_META, :]
        m = jnp.max(s, axis=1, keepdims=True)
        p = jnp.exp(s - m)
        l = jnp.sum(p, axis=1, keepdims=True)
        o = _dot(p.astype(BF16), vm) / l
        o_ref[:, h * LANES:(h + 1) * LANES] = _merge_heads(o).astype(BF16)


def _na_meta(proj_meta, mb_meta):
    return pl.pallas_call(
        _na_meta_body,
        out_shape=jax.ShapeDtypeStruct((N_META, NA_WIDTH), BF16),
        name="na_meta",
    )(proj_meta, mb_meta)


def _split_maps(q):
    lane = lax.broadcasted_iota(I32, q.shape, 1)
    is0 = ((lane >> 5) & 1) == 0
    zero = jnp.zeros_like(q)
    return jnp.concatenate([jnp.where(is0, q, zero), jnp.where(is0, zero, q)], axis=0)


def _online_softmax(s_ref, p_ref, a_ref, m_scr, rows):
    s = s_ref[rows, :]
    m_prev = m_scr[rows]
    m_new = jnp.maximum(m_prev, jnp.max(s, axis=1, keepdims=True))
    a_ref[rows] = jnp.exp2(m_prev - m_new)
    m_scr[rows] = m_new
    p_ref[rows, :] = jnp.exp2((s - m_new).astype(BF16))


def _diff_body(q_ref, k_ref, v_ref, km_ref, vm_ref, lq1_ref, lk1_ref, lq2_ref, lk2_ref, sub_ref,
               o_ref, qq_scr, vx_scr, s0, s1, p0, p1, a0, a1, m_scr, acc_scr, *, tq, tk, rc, nk):
    def kv_rows(ref, ki):
        return ref[0, pl.ds(pl.multiple_of(ki * tk, tk), tk), :]

    qq = _split_maps(q_ref[0])
    qq_scr[...] = qq
    s = _dot_nt(qq, km_ref[...])
    m = jnp.max(s, axis=1, keepdims=True)
    m_scr[...] = m
    vx_scr[:, LANES:] = jnp.ones((vx_scr.shape[0], LANES), BF16)
    vx_scr[0:N_META, 0:LANES] = vm_ref[...]
    acc_scr[...] = _dot(jnp.exp2((s - m).astype(BF16)), vx_scr[0:N_META, :])
    s0[...] = _dot_nt(qq, k_ref[0, 0:tk, :])
    p1[...] = jnp.zeros_like(p1)
    a1[...] = jnp.ones_like(a1)

    def step(ki, s_cur, s_nxt, p_cur, p_prv, a_cur, a_prv):
        kn = kv_rows(k_ref, jnp.minimum(ki + 1, nk - 1))
        vx_scr[:, 0:LANES] = kv_rows(v_ref, jnp.maximum(ki - 1, 0))
        for g in range(2 * tq // rc):
            rows = slice(g * rc, (g + 1) * rc)
            s_nxt[rows, :] = _dot_nt(qq_scr[rows, :], kn)
            acc_scr[rows, :] = a_prv[rows] * acc_scr[rows, :] + _dot(p_prv[rows, :], vx_scr[...])
            _online_softmax(s_cur, p_cur, a_cur, m_scr, rows)

    def block(ki, carry):
        @pl.when(ki % 2 == 0)
        def _():
            step(ki, s0, s1, p0, p1, a0, a1)

        @pl.when(ki % 2 == 1)
        def _():
            step(ki, s1, s0, p1, p0, a1, a0)

        return carry

    lax.fori_loop(0, nk, block, 0)

    p_last, a_last = (p0, a0) if (nk - 1) % 2 == 0 else (p1, a1)
    vx_scr[:, 0:LANES] = v_ref[0, (nk - 1) * tk:nk * tk, :]
    acc = a_last[...] * acc_scr[...] + _dot(p_last[...], vx_scr[...])
    lam = (jnp.exp(jnp.sum(lq1_ref[...] * lk1_ref[...], axis=1, keepdims=True))
           - jnp.exp(jnp.sum(lq2_ref[...] * lk2_ref[...], axis=1, keepdims=True)) + LAMBDA_INIT)
    o_all = acc[:, 0:LANES] / acc[:, LANES:]
    o = o_all[:tq] - lam * o_all[tq:]
    ms = jnp.mean(o * o, axis=-1, keepdims=True)
    o = o * lax.rsqrt(ms + SUBLN_EPS) * sub_ref[...] * (1.0 - LAMBDA_INIT)
    o_ref[0] = o.astype(BF16)


def _diff_attention(q_src, proj, proj_meta, lq1, lk1, lq2, lk2, subln, *, tq, tk, q_shared):
    b, t, _ = proj.shape
    tq_total = q_src.shape[1]
    nq = tq_total // tq
    nk = t // tk
    if q_shared:
        q_map = lambda bi, h, qi: (0, qi, DQ_BLK + h)
    else:
        q_map = lambda bi, h, qi: (bi, qi, DQ_BLK + h)
    vec = lambda n: pl.BlockSpec((1, n), lambda bi, h, qi: (0, 0))
    return pl.pallas_call(
        functools.partial(_diff_body, tq=tq, tk=tk, rc=min(512, 2 * tq), nk=nk),
        grid=(b, DIFF_HEADS, nq),
        in_specs=[
            pl.BlockSpec((1, tq, LANES), q_map),
            pl.BlockSpec((1, t, LANES), lambda bi, h, qi: (bi, 0, DK_BLK + h)),
            pl.BlockSpec((1, t, LANES), lambda bi, h, qi: (bi, 0, DV_BLK + h)),
            pl.BlockSpec((N_META, LANES), lambda bi, h, qi: (0, DK_BLK + h)),
            pl.BlockSpec((N_META, LANES), lambda bi, h, qi: (0, DV_BLK + h)),
            vec(HEAD_DIM), vec(HEAD_DIM), vec(HEAD_DIM), vec(HEAD_DIM), vec(2 * HEAD_DIM),
        ],
        out_specs=pl.BlockSpec((1, tq, LANES), lambda bi, h, qi: (bi, qi, h)),
        out_shape=jax.ShapeDtypeStruct((b, tq_total, DIFF_WIDTH), BF16),
        scratch_shapes=[
            pltpu.VMEM((2 * tq, LANES), BF16),
            pltpu.VMEM((tk, 2 * LANES), BF16),
            pltpu.VMEM((2 * tq, tk), F32), pltpu.VMEM((2 * tq, tk), F32),
            pltpu.VMEM((2 * tq, tk), BF16), pltpu.VMEM((2 * tq, tk), BF16),
            pltpu.VMEM((2 * tq, 1), F32), pltpu.VMEM((2 * tq, 1), F32),
            pltpu.VMEM((2 * tq, 1), F32),
            pltpu.VMEM((2 * tq, 2 * LANES), F32),
        ],
        compiler_params=_cparams(("parallel", "parallel", "parallel")),
        name="diff_attention",
    )(q_src, proj, proj, proj_meta, proj_meta, lq1, lk1, lq2, lk2, subln)


def _outproj_body(x_ref, na_ref, df_ref, wo_ref, g_ref, wrh_ref, wrl_ref, x2_ref, h2_ref, afft_ref, *aff_ref):
    x2 = x_ref[...] + _dot(na_ref[...], wo_ref[0:NA_WIDTH, :]) + _dot(df_ref[...], wo_ref[NA_WIDTH:, :])
    x2_ref[...] = x2
    ms = jnp.mean(x2 * x2, axis=-1, keepdims=True)
    hf = x2 * lax.rsqrt(ms + RMS_EPS) * g_ref[...]
    hi = hf.astype(BF16)
    h2_ref[...] = hi
    lo = (hf - hi.astype(F32)).astype(BF16)
    logits = _dot_nt(wrh_ref[...], hi) + _dot_nt(wrh_ref[...], lo) + _dot_nt(wrl_ref[...], hi)
    row = lax.broadcasted_iota(I32, logits.shape, 0)
    logits = jnp.where(row < N_EXPERTS, logits, NEG_BIG)
    m = jnp.max(logits, axis=0, keepdims=True)
    e = jnp.exp(logits - m)
    aff = e / jnp.sum(e, axis=0, keepdims=True)
    afft_ref[...] = aff[0:N_EXPERTS]
    if aff_ref:
        aff_ref[0][...] = aff.T


def _outproj(x, na, df, wo, g, wrh, wrl, *, tm, token_major):
    rows, d = x.shape
    out_shape = [jax.ShapeDtypeStruct((rows, d), F32), jax.ShapeDtypeStruct((rows, d), BF16),
                 jax.ShapeDtypeStruct((N_EXPERTS, rows), F32)]
    out_specs = [pl.BlockSpec((tm, d), lambda i: (i, 0)), pl.BlockSpec((tm, d), lambda i: (i, 0)),
                 pl.BlockSpec((N_EXPERTS, tm), lambda i: (0, i))]
    if token_major:
        out_shape.append(jax.ShapeDtypeStruct((rows, LANES), F32))
        out_specs.append(pl.BlockSpec((tm, LANES), lambda i: (i, 0)))
    return pl.pallas_call(
        _outproj_body,
        grid=(rows // tm,),
        in_specs=[
            pl.BlockSpec((tm, d), lambda i: (i, 0)),
            pl.BlockSpec((tm, NA_WIDTH), lambda i: (i, 0)),
            pl.BlockSpec((tm, DIFF_WIDTH), lambda i: (i, 0)),
            pl.BlockSpec((NA_WIDTH + DIFF_WIDTH, d), lambda i: (0, 0)),
            pl.BlockSpec((1, d), lambda i: (0, 0)),
            pl.BlockSpec((LANES, d), lambda i: (0, 0)),
            pl.BlockSpec((LANES, d), lambda i: (0, 0)),
        ],
        out_specs=out_specs,
        out_shape=out_shape,
        compiler_params=_cparams(("parallel",)),
        name="outproj_router",
    )(x, na, df, wo, g, wrh, wrl)


def _select_body(ar_ref, am_ref, gr_ref, gm_ref, tri_ref, tau_ref, cut_ref, offs_ref, cnt_ref, rank_ref, sel_scr,
                 *, cap, nb, idx_bits):
    br = pltpu.bitcast(ar_ref[...], I32)
    bm = pltpu.bitcast(am_ref[...], I32)
    gr = gr_ref[...]
    gm = gm_ref[...]

    def count(mr, mm):
        return (jnp.sum(jnp.where(mr, 1.0, 0.0), axis=1, keepdims=True)
                + jnp.sum(jnp.where(mm, 1.0, 0.0), axis=1, keepdims=True))

    def value_bit(i, ans):
        cand = ans | jnp.left_shift(jnp.int32(1), 30 - i)
        return jnp.where(count(br >= cand, bm >= cand) >= cap, cand, ans)

    tau = lax.fori_loop(0, 31, value_bit, jnp.zeros((N_EXPERTS, 1), I32))
    need = cap - count(br > tau, bm > tau)
    eq_r = br == tau
    eq_m = bm == tau

    def index_bit(i, ans):
        cand = ans | jnp.left_shift(jnp.int32(1), idx_bits - 1 - i)
        return jnp.where(count(eq_r & (gr < cand), eq_m & (gm < cand)) < need, cand, ans)

    cut = lax.fori_loop(0, idx_bits, index_bit, jnp.zeros((N_EXPERTS, 1), I32))
    tau_ref[...] = jnp.broadcast_to(tau, tau_ref.shape)
    cut_ref[...] = jnp.broadcast_to(cut, cut_ref.shape)

    sel_scr[...] = jnp.where((br > tau) | (eq_r & (gr <= cut)), 1.0, 0.0)
    lane = lax.broadcasted_iota(I32, (N_EXPERTS, LANES), 1)

    def block_count(j, acc):
        start = pl.multiple_of(j * TOK_BLK, TOK_BLK)
        sel = sel_scr[:, pl.ds(start, TOK_BLK)]
        rank_ref[:, pl.ds(start, TOK_BLK)] = sel * _dot(sel.astype(BF16), tri_ref[...])
        c = jnp.sum(sel, axis=1, keepdims=True)
        return jnp.where(lane == j, c, acc)

    counts = lax.fori_loop(0, nb, block_count, jnp.zeros((N_EXPERTS, LANES), F32))
    incl = counts
    shift = 1
    while shift < LANES:
        incl = incl + jnp.where(lane >= shift, pltpu.roll(incl, shift, 1), 0.0)
        shift *= 2
    offs_ref[...] = (incl - counts).astype(I32)
    cnt_ref[...] = counts.astype(I32)


def _select(aff_t, aff_t_meta, gidx, gidx_meta, tri_u, *, cap, n_total):
    e, nr = aff_t.shape
    nb = nr // TOK_BLK
    assert nb <= LANES and nr % TOK_BLK == 0
    idx_bits = max(1, (n_total - 1).bit_length())
    shp = jax.ShapeDtypeStruct((e, LANES), I32)
    return pl.pallas_call(
        functools.partial(_select_body, cap=cap, nb=nb, idx_bits=idx_bits),
        out_shape=[shp, shp, shp, shp, jax.ShapeDtypeStruct((e, nr), F32)],
        scratch_shapes=[pltpu.VMEM((e, nr), F32)],
        compiler_params=pltpu.CompilerParams(vmem_limit_bytes=VMEM_LIMIT),
        name="ec_select",
    )(aff_t, aff_t_meta, gidx, gidx_meta, tri_u)


def _gather_body(offs_ref, cnts_ref, rank_ref, h_hbm, xe_ref, hbuf, sems, *, nb, n_steps):
    e = pl.program_id(0)
    j = pl.program_id(1)
    step = e * nb + j

    def h_copy(s):
        slot = s % GATHER_RING
        rows = pl.ds(pl.multiple_of((s % nb) * TOK_BLK, TOK_BLK), TOK_BLK)
        return pltpu.make_async_copy(h_hbm.at[rows, :], hbuf.at[slot], sems.at[slot])

    @pl.when(step == 0)
    def _():
        for s in range(GATHER_RING - 1):
            h_copy(s).start()

    @pl.when(step + GATHER_RING - 1 < n_steps)
    def _():
        h_copy(step + GATHER_RING - 1).start()

    @pl.when(j == 0)
    def _():
        xe_ref[...] = jnp.zeros_like(xe_ref)

    rank = rank_ref[0].astype(I32)
    cnt = cnts_ref[e * nb + j]
    off = offs_ref[e * nb + j]
    off_al = (off // 16) * 16
    rel = jnp.where(rank > 0, rank - 1 + (off - off_al), -1)
    nchunk = (off - off_al + cnt + GATHER_WIN - 1) // GATHER_WIN
    rowid = lax.broadcasted_iota(I32, (GATHER_WIN, TOK_BLK), 0)
    h_copy(step).wait()
    slot = step % GATHER_RING

    def chunk(c, carry):
        onehot = jnp.where(rowid == rel - c * GATHER_WIN, 1.0, 0.0).astype(BF16)
        g = _dot(onehot, hbuf[slot]).astype(BF16)
        start = pl.multiple_of(off_al + c * GATHER_WIN, 16)
        xe_ref[0, pl.ds(start, GATHER_WIN), :] = xe_ref[0, pl.ds(start, GATHER_WIN), :] + g
        return carry

    lax.fori_loop(0, nchunk, chunk, 0)


def _gather(offs_flat, cnts_flat, rank3, h2, *, cap_pad):
    e = rank3.shape[0]
    nr, d = h2.shape
    nb = nr // TOK_BLK
    return pl.pallas_call(
        functools.partial(_gather_body, nb=nb, n_steps=e * nb),
        grid_spec=pltpu.PrefetchScalarGridSpec(
            num_scalar_prefetch=2,
            grid=(e, nb),
            in_specs=[
                pl.BlockSpec((1, 1, TOK_BLK), lambda ei, j, o, c: (ei, 0, j)),
                pl.BlockSpec(memory_space=pl.ANY),
            ],
            out_specs=pl.BlockSpec((1, cap_pad, d), lambda ei, j, o, c: (ei, 0, 0)),
            scratch_shapes=[pltpu.VMEM((GATHER_RING, TOK_BLK, d), BF16),
                            pltpu.SemaphoreType.DMA((GATHER_RING,))],
        ),
        out_shape=jax.ShapeDtypeStruct((e, cap_pad, d), BF16),
        compiler_params=_cparams(("arbitrary", "arbitrary")),
        name="ec_gather",
    )(offs_flat, cnts_flat, rank3, h2)


def _ffn_body(x_ref, wg_ref, wu_ref, wd_ref, o_ref, *, fc):
    x = x_ref[0]
    f = wg_ref.shape[2]
    acc = jnp.zeros((x.shape[0], wd_ref.shape[2]), F32)
    for c in range(f // fc):
        a = _dot(x, wg_ref[0, :, c * fc:(c + 1) * fc])
        b = _dot(x, wu_ref[0, :, c * fc:(c + 1) * fc])
        hmid = (a * jax.nn.sigmoid(a) * b).astype(BF16)
        acc = acc + _dot(hmid, wd_ref[0, c * fc:(c + 1) * fc, :])
    o_ref[0] = acc.astype(BF16)


def _ffn(xe, wg, wu, wd, *, tm):
    e, cap_pad, d = xe.shape
    f = wg.shape[2]
    return pl.pallas_call(
        functools.partial(_ffn_body, fc=min(512, f)),
        grid=(e, cap_pad // tm),
        in_specs=[
            pl.BlockSpec((1, tm, d), lambda ei, i: (ei, i, 0)),
            pl.BlockSpec((1, d, f), lambda ei, i: (ei, 0, 0)),
            pl.BlockSpec((1, d, f), lambda ei, i: (ei, 0, 0)),
            pl.BlockSpec((1, f, d), lambda ei, i: (ei, 0, 0)),
        ],
        out_specs=pl.BlockSpec((1, tm, d), lambda ei, i: (ei, i, 0)),
        out_shape=jax.ShapeDtypeStruct((e, cap_pad, d), BF16),
        compiler_params=_cparams(("parallel", "arbitrary")),
        name="ec_ffn",
    )(xe, wg, wu, wd)


def _combine_body(offs_ref, cnts_ref, aff_ref, tau_ref, cut_ref, x2_ref, tri_ref, fn_ref, *rest, nb, t, l_total):
    ywin_refs = rest[:COMBINE_EXPERTS]
    ye_hbm, o_ref, rank_scr, gate_scr, acc_scr, ybuf, sem = rest[COMBINE_EXPERTS:]
    j = pl.program_id(0)
    eg = pl.program_id(1)

    @pl.when(eg == 0)
    def _():
        aff = aff_ref[...]
        bits = pltpu.bitcast(aff, I32)
        tau = tau_ref[0:1, :]
        cut = cut_ref[0:1, :]
        row0 = j * TOK_BLK
        gidx = row0 + (row0 // t) * (l_total - t) + (l_total - t) + lax.broadcasted_iota(I32, bits.shape, 0)
        sel = (bits > tau) | ((bits == tau) & (gidx <= cut))
        sel = sel & (lax.broadcasted_iota(I32, bits.shape, 1) < N_EXPERTS)
        self32 = jnp.where(sel, 1.0, 0.0)
        incl = _dot(tri_ref[...], self32.astype(BF16))
        rank_scr[...] = jnp.where(sel, incl - 1.0, -1.0)
        gate_scr[...] = jnp.where(sel, aff, 0.0)
        acc_scr[...] = x2_ref[...]

    lane = lax.broadcasted_iota(I32, (TOK_BLK, LANES), 1)
    colid = lax.broadcasted_iota(I32, (TOK_BLK, ROW_WIN), 1)

    def placement(e):
        pick = lane == e
        rank = jnp.sum(jnp.where(pick, rank_scr[...], 0.0), axis=1, keepdims=True).astype(I32)
        gate = jnp.sum(jnp.where(pick, gate_scr[...], 0.0), axis=1, keepdims=True)
        off = offs_ref[e * nb + j]
        off_al = (off // 16) * 16
        rel = jnp.where(rank >= 0, rank + (off - off_al), -1)
        return rel, gate, off_al, (off - off_al + cnts_ref[e * nb + j] + ROW_WIN - 1) // ROW_WIN

    acc = acc_scr[...]
    for u in range(COMBINE_EXPERTS):
        rel, gate, _, _ = placement(eg * COMBINE_EXPERTS + u)
        onehot = jnp.where(colid == rel, 1.0, 0.0).astype(BF16)
        acc = acc + _dot(onehot, ywin_refs[u][...]) * gate
    acc_scr[...] = acc

    for u in range(COMBINE_EXPERTS):
        e = eg * COMBINE_EXPERTS + u
        nchunk = placement(e)[3]

        def chunk(c, carry, e=e):
            rel, gate, off_al, _ = placement(e)
            start = pl.multiple_of(off_al + c * ROW_WIN, 16)
            cp = pltpu.make_async_copy(ye_hbm.at[e, pl.ds(start, ROW_WIN), :], ybuf, sem)
            cp.start()
            cp.wait()
            oh = jnp.where(colid == rel - c * ROW_WIN, 1.0, 0.0).astype(BF16)
            acc_scr[...] += _dot(oh, ybuf[...]) * gate
            return carry

        lax.fori_loop(1, nchunk, chunk, 0)

    @pl.when(eg == pl.num_programs(1) - 1)
    def _():
        y = acc_scr[...]
        ms = jnp.mean(y * y, axis=-1, keepdims=True)
        o_ref[...] = y * lax.rsqrt(ms + RMS_EPS) * fn_ref[...]


def _combine(offs_flat, cnts_flat, aff, tau_row, cut_row, x2, tri_l, fnorm, ye, *, t, l_total):
    nr, d = x2.shape
    nb = nr // TOK_BLK
    e = ye.shape[0]

    def win_spec(u):
        def win_map(j, eg, o, c):
            ei = eg * COMBINE_EXPERTS + u
            return (ei, (o[ei * nb + j] // 16) * 16, 0)
        return pl.BlockSpec((pl.Squeezed(), pl.Element(ROW_WIN), pl.Element(d)), win_map)

    return pl.pallas_call(
        functools.partial(_combine_body, nb=nb, t=t, l_total=l_total),
        grid_spec=pltpu.PrefetchScalarGridSpec(
            num_scalar_prefetch=2,
            grid=(nb, e // COMBINE_EXPERTS),
            in_specs=[
                pl.BlockSpec((TOK_BLK, LANES), lambda j, ei, o, c: (j, 0)),
                pl.BlockSpec((8, LANES), lambda j, ei, o, c: (0, 0)),
                pl.BlockSpec((8, LANES), lambda j, ei, o, c: (0, 0)),
                pl.BlockSpec((TOK_BLK, d), lambda j, ei, o, c: (j, 0)),
                pl.BlockSpec((TOK_BLK, TOK_BLK), lambda j, ei, o, c: (0, 0)),
                pl.BlockSpec((1, d), lambda j, ei, o, c: (0, 0)),
                *[win_spec(u) for u in range(COMBINE_EXPERTS)],
                pl.BlockSpec(memory_space=pl.ANY),
            ],
            out_specs=pl.BlockSpec((TOK_BLK, d), lambda j, ei, o, c: (j, 0)),
            scratch_shapes=[
                pltpu.VMEM((TOK_BLK, LANES), F32),
                pltpu.VMEM((TOK_BLK, LANES), F32),
                pltpu.VMEM((TOK_BLK, d), F32),
                pltpu.VMEM((ROW_WIN, d), BF16),
                pltpu.SemaphoreType.DMA(()),
            ],
        ),
        out_shape=jax.ShapeDtypeStruct((nr, d), F32),
        compiler_params=_cparams(("parallel", "arbitrary")),
        name="ec_combine",
    )(offs_flat, cnts_flat, aff, tau_row, cut_row, x2, tri_l, fnorm, *([ye] * COMBINE_EXPERTS), ye)


def _rope_tables(positions):
    inv_freq = 1.0 / (ROPE_THETA ** (jnp.arange(0, HEAD_DIM, 2, dtype=F32) / HEAD_DIM))
    ang = positions.astype(F32)[:, None] * inv_freq[None, :]
    cos, sin = jnp.cos(ang), jnp.sin(ang)
    return (jnp.concatenate([cos, cos, cos, cos], axis=1),
            jnp.concatenate([-sin, -sin, sin, sin], axis=1))


def _permute_in_weights(w_in):
    l = jnp.arange(LANES)
    g = l // 32
    within = (g % 2) * HEAD_DIM + (g // 2) * 32 + l % 32
    cols = jnp.arange(PROJ_WIDTH)
    blk = cols // LANES
    permuted = blk * LANES + within[cols % LANES]
    cols = jnp.where((blk >= DQ_BLK) & (blk < DV_BLK), permuted, cols)
    return w_in[:, cols]


def _encode_group(x, shared):
    b, t, d = x.shape
    l_total = N_META + t
    n_total = b * l_total
    cap = EC_CAPACITY_FACTOR * n_total // N_EXPERTS
    rows = b * t
    xf = x.reshape(rows, d)

    cos, sin = _rope_tables(N_META + jnp.arange(t))
    proj = _inproj(xf, shared["norm_mix"], shared["w_in"], cos, sin, tm=512).reshape(b, t, PROJ_WIDTH)
    proj_meta = shared["proj_meta"]

    na = _na_attention(proj, proj_meta, shared["na_table"], shared["mb_stack"])
    lam = shared["lambda"]
    tk = min(1024, t)
    df = _diff_attention(proj, proj, proj_meta, *lam, shared["subln"], tq=min(1024, t), tk=tk, q_shared=False)
    df_meta = _diff_attention(proj_meta[None], proj, proj_meta, *lam, shared["subln"],
                              tq=N_META, tk=tk, q_shared=True)

    wr = (shared["wr_hi"], shared["wr_lo"])
    x2, h2, aff_t, aff = _outproj(xf, na.reshape(rows, NA_WIDTH), df.reshape(rows, DIFF_WIDTH), shared["w_out"],
                                  shared["norm_ffn"], *wr, tm=512, token_major=True)
    xm = jnp.tile(shared["meta_tokens"], (b, 1))
    nam = jnp.tile(shared["na_meta"], (b, 1))
    _, _, aff_t_meta = _outproj(xm, nam, df_meta.reshape(b * N_META, DIFF_WIDTH), shared["w_out"],
                                shared["norm_ffn"], *wr, tm=b * N_META, token_major=False)

    r = jnp.arange(rows, dtype=I32)
    gidx = (r + (r // t + 1) * N_META)[None, :]
    rm = jnp.arange(b * N_META, dtype=I32)
    gidx_meta = ((rm // N_META) * l_total + rm % N_META)[None, :]
    tau, cut, offs, cnts, rank_t = _select(aff_t, aff_t_meta, gidx, gidx_meta, shared["tri_u"],
                                           cap=cap, n_total=n_total)

    nb = rows // TOK_BLK
    offs_flat = offs[:, :nb].reshape(-1)
    cnts_flat = cnts[:, :nb].reshape(-1)
    cap_pad = -(-(cap + ROW_WIN) // 256) * 256
    ffn_tm = max(tm for tm in range(16, 1025, 16) if cap_pad % tm == 0)
    xe = _gather(offs_flat, cnts_flat, rank_t.reshape(N_EXPERTS, 1, rows), h2, cap_pad=cap_pad)
    ye = _ffn(xe, shared["w_gate"], shared["w_up"], shared["w_down"], tm=ffn_tm)

    pad = jnp.zeros((8, LANES - N_EXPERTS), I32)
    tau_row = jnp.concatenate([jnp.broadcast_to(tau[:, 0][None, :], (8, N_EXPERTS)), pad], axis=1)
    cut_row = jnp.concatenate([jnp.broadcast_to(cut[:, 0][None, :], (8, N_EXPERTS)), pad], axis=1)
    y = _combine(offs_flat, cnts_flat, aff, tau_row, cut_row, x2, shared["tri_l"], shared["final_norm"], ye,
                 t=t, l_total=l_total)
    return y.reshape(b, t, d)


def kernel(x_prompt, x_sample, meta_tokens, norm_mix, w_in, na_rpb, na_meta_bias, lambda_q1, lambda_k1,
           lambda_q2, lambda_k2, diff_subln, w_out, norm_ffn, w_router, w_gate, w_up, w_down, final_norm):
    d = x_prompt.shape[-1]
    wr = jnp.zeros((LANES, d), F32).at[:N_EXPERTS].set(w_router[0].T)
    wr_hi = wr.astype(BF16)
    mb = na_meta_bias[0].astype(F32)
    idx = jnp.arange(TOK_BLK)
    shared = {
        "meta_tokens": meta_tokens,
        "norm_mix": norm_mix[0][None, :],
        "w_in": _permute_in_weights(w_in[0]).astype(BF16),
        "lambda": (lambda_q1, lambda_k1, lambda_q2, lambda_k2),
        "subln": diff_subln,
        "w_out": w_out[0].astype(BF16),
        "norm_ffn": norm_ffn[0][None, :],
        "wr_hi": wr_hi,
        "wr_lo": (wr - wr_hi.astype(F32)).astype(BF16),
        "w_gate": w_gate[0].astype(BF16),
        "w_up": w_up[0].astype(BF16),
        "w_down": w_down[0].astype(BF16),
        "final_norm": final_norm[None, :],
        "mb_stack": jnp.repeat(mb, GRID_W, axis=0).reshape(NA_HEADS // 2, 2 * GRID_W, N_META),
        "tri_u": (idx[:, None] <= idx[None, :]).astype(BF16),
        "tri_l": (idx[:, None] >= idx[None, :]).astype(BF16),
    }
    cos_m, sin_m = _rope_tables(jnp.arange(N_META))
    shared["proj_meta"] = _inproj(meta_tokens, shared["norm_mix"], shared["w_in"], cos_m, sin_m, tm=N_META)
    mb_meta = jnp.repeat(mb, N_META, axis=0).reshape(NA_HEADS // 2, 2 * N_META, N_META)
    shared["na_meta"] = _na_meta(shared["proj_meta"], mb_meta)
    shared["na_table"] = _na_bias(na_rpb[0].astype(F32))
    return (_encode_group(x_prompt, shared), _encode_group(x_sample, shared))
```

```python
import functools
import math

import jax
import jax.numpy as jnp
from jax import lax
from jax.experimental import pallas as pl
from jax.experimental.pallas import tpu as pltpu

BF16 = jnp.bfloat16
F32 = jnp.float32
I32 = jnp.int32

N_META = 16
GRID_W = 64
HEAD_DIM = 64
NA_HEADS = 8
NA_WIDTH = NA_HEADS * HEAD_DIM
NA_KH = 8
NA_KW = 16
DIFF_HEADS = 4
DIFF_WIDTH = DIFF_HEADS * 2 * HEAD_DIM
PROJ_WIDTH = 3 * NA_WIDTH + 3 * DIFF_WIDTH
ROPE_THETA = 10000.0
N_EXPERTS = 16
EC_CAPACITY_FACTOR = 2
RMS_EPS = 1e-6
SUBLN_EPS = 1e-5
LAMBDA_INIT = 0.8 - 0.6 * math.exp(-0.3 * 0)
LOG2E = math.log2(math.e)

LANES = 128
NEG_BIG = -1e30
VMEM_LIMIT = 56 * 1024 * 1024

NAQ_BLK, NAK_BLK, NAV_BLK = 0, 4, 8
DQ_BLK, DK_BLK, DV_BLK = 12, 16, 20

ROW_WIN = 256
GATHER_WIN = 192
GATHER_RING = 3
TOK_BLK = 1024
COMBINE_EXPERTS = 4


def _cparams(sem, vmem=VMEM_LIMIT):
    return pltpu.CompilerParams(dimension_semantics=sem, vmem_limit_bytes=vmem)


def _dot(a, b):
    return jnp.dot(a, b, preferred_element_type=F32)


def _dot_nt(a, b):
    return lax.dot_general(a, b, (((1,), (1,)), ((), ())), preferred_element_type=F32)


def _inproj_body(x_ref, g_ref, w_ref, cos_ref, sin_ref, o_ref, *, tn):
    x = x_ref[...]
    ms = jnp.mean(x * x, axis=-1, keepdims=True)
    h = (x * lax.rsqrt(ms + RMS_EPS) * g_ref[...]).astype(BF16)
    cos = cos_ref[...]
    sin = sin_ref[...]
    for c in range(PROJ_WIDTH // tn):
        lo = c * tn
        acc = _dot(h, w_ref[:, lo:lo + tn])
        for s in range(tn // LANES):
            blk = (lo + s * LANES) // LANES
            a = acc[:, s * LANES:(s + 1) * LANES]
            if DQ_BLK <= blk < DV_BLK:
                a = a * cos + pltpu.roll(a, 64, 1) * sin
            if blk < NAK_BLK:
                a = a * (HEAD_DIM ** -0.5)
            if DQ_BLK <= blk < DK_BLK:
                a = a * (HEAD_DIM ** -0.5 * LOG2E)
            o_ref[:, lo + s * LANES:lo + (s + 1) * LANES] = a.astype(BF16)


def _inproj(x, g, w, cos, sin, *, tm):
    rows, d = x.shape
    nt = cos.shape[0] // tm
    return pl.pallas_call(
        functools.partial(_inproj_body, tn=512),
        grid=(rows // tm,),
        in_specs=[
            pl.BlockSpec((tm, d), lambda i: (i, 0)),
            pl.BlockSpec((1, d), lambda i: (0, 0)),
            pl.BlockSpec((d, PROJ_WIDTH), lambda i: (0, 0)),
            pl.BlockSpec((tm, LANES), lambda i: (i % nt, 0)),
            pl.BlockSpec((tm, LANES), lambda i: (i % nt, 0)),
        ],
        out_specs=pl.BlockSpec((tm, PROJ_WIDTH), lambda i: (i, 0)),
        out_shape=jax.ShapeDtypeStruct((rows, PROJ_WIDTH), BF16),
        compiler_params=_cparams(("parallel",)),
        name="inproj",
    )(x, g, w, cos, sin)


def _na_bias_body(rpb_ref, o_ref):
    h = pl.program_id(0)
    c = lax.broadcasted_iota(I32, (GRID_W, GRID_W), 0)
    kc = lax.broadcasted_iota(I32, (GRID_W, GRID_W), 1)
    cs = jnp.clip(c - NA_KW // 2, 0, GRID_W - NA_KW)
    valid = (kc >= cs) & (kc < cs + NA_KW)
    jm = kc - c + NA_KW - 1
    n_dr = 2 * NA_KH - 1
    n_j = 2 * NA_KW - 1
    tiles = []
    for dr in range(n_dr):
        base = (h * n_dr + dr) * n_j

        def body(j, acc, base=base):
            return jnp.where(jm == j, rpb_ref[base + j], acc)

        t = lax.fori_loop(0, n_j, body, jnp.zeros((GRID_W, GRID_W), F32))
        tiles.append(jnp.where(valid, t, NEG_BIG))
    for d in range(NA_KH):
        o_ref[0, d] = jnp.concatenate([tiles[kr - d + NA_KH - 1] for kr in range(NA_KH)], axis=1)


def _na_bias(rpb):
    h = rpb.shape[0]
    return pl.pallas_call(
        _na_bias_body,
        grid_spec=pltpu.PrefetchScalarGridSpec(
            num_scalar_prefetch=1,
            grid=(h,),
            in_specs=[],
            out_specs=pl.BlockSpec((1, NA_KH, GRID_W, NA_KH * GRID_W), lambda i, r: (i, 0, 0, 0)),
        ),
        out_shape=jax.ShapeDtypeStruct((h, NA_KH, GRID_W, NA_KH * GRID_W), F32),
        compiler_params=_cparams(("arbitrary",)),
        name="na_bias",
    )(rpb.reshape(-1))


def _split_heads(q):
    lane = lax.broadcasted_iota(I32, q.shape, 1)
    zero = jnp.zeros_like(q)
    return jnp.concatenate([jnp.where(lane < HEAD_DIM, q, zero), jnp.where(lane < HEAD_DIM, zero, q)], axis=0)


def _merge_heads(o):
    n = o.shape[0] // 2
    lane = lax.broadcasted_iota(I32, (n, LANES), 1)
    return jnp.where(lane < HEAD_DIM, o[:n], o[n:])


def _na_body(q_ref, kp_ref, kc_ref, kn_ref, vp_ref, vc_ref, vn_ref, km_ref, vm_ref, tab_ref, mb_ref,
             o_ref, kbuf, vbuf, qq_scr, s_scr, p_scr, *, rows):
    rb = pl.program_id(2)
    blk = NA_KH * GRID_W
    kbuf[0:blk] = kp_ref[0]
    kbuf[blk:2 * blk] = kc_ref[0]
    kbuf[2 * blk:3 * blk] = kn_ref[0]
    vbuf[0:blk] = vp_ref[0]
    vbuf[blk:2 * blk] = vc_ref[0]
    vbuf[2 * blk:3 * blk] = vn_ref[0]
    two = 2 * GRID_W
    starts = []
    for i in range(NA_KH):
        r = rb * NA_KH + i
        rs = jnp.clip(r - NA_KH // 2, 0, rows - NA_KH)
        d = r - rs
        start = pl.multiple_of((rs - (rb - 1) * NA_KH) * GRID_W, GRID_W)
        starts.append(start)
        qq = _split_heads(q_ref[0, i * GRID_W:(i + 1) * GRID_W, :])
        qq_scr[i * two:(i + 1) * two, :] = qq
        bias = jnp.concatenate([tab_ref[0, d], tab_ref[1, d]], axis=0)
        s_scr[i * two:(i + 1) * two, :] = _dot_nt(qq, kbuf[pl.ds(start, blk), :]) + bias
    s = s_scr[...]
    sm = (_dot_nt(qq_scr[...], km_ref[...]).reshape(NA_KH, two, N_META) + mb_ref[0][None]).reshape(NA_KH * two, N_META)
    m = jnp.maximum(jnp.max(s, axis=1, keepdims=True), jnp.max(sm, axis=1, keepdims=True))
    p = jnp.exp(s - m)
    pm = jnp.exp(sm - m)
    l = jnp.sum(p, axis=1, keepdims=True) + jnp.sum(pm, axis=1, keepdims=True)
    p_scr[...] = p.astype(BF16)
    om = _dot(pm.astype(BF16), vm_ref[...])
    inv_l = 1.0 / l
    for i in range(NA_KH):
        rsl = slice(i * two, (i + 1) * two)
        o = (_dot(p_scr[rsl, :], vbuf[pl.ds(starts[i], blk), :]) + om[rsl]) * inv_l[rsl]
        o_ref[0, i * GRID_W:(i + 1) * GRID_W, :] = _merge_heads(o).astype(BF16)


def _na_attention(proj, proj_meta, table, mb_stack):
    b, t, _ = proj.shape
    rows = t // GRID_W
    nrb = rows // NA_KH
    blk = NA_KH * GRID_W
    hp = NA_HEADS // 2

    def kv_spec(col, shift):
        return pl.BlockSpec((1, blk, LANES),
                            lambda h, bi, rb: (bi, jnp.clip(rb + shift, 0, nrb - 1), col + h))

    return pl.pallas_call(
        functools.partial(_na_body, rows=rows),
        grid=(hp, b, nrb),
        in_specs=[
            pl.BlockSpec((1, blk, LANES), lambda h, bi, rb: (bi, rb, NAQ_BLK + h)),
            kv_spec(NAK_BLK, -1), kv_spec(NAK_BLK, 0), kv_spec(NAK_BLK, 1),
            kv_spec(NAV_BLK, -1), kv_spec(NAV_BLK, 0), kv_spec(NAV_BLK, 1),
            pl.BlockSpec((N_META, LANES), lambda h, bi, rb: (0, NAK_BLK + h)),
            pl.BlockSpec((N_META, LANES), lambda h, bi, rb: (0, NAV_BLK + h)),
            pl.BlockSpec((2, NA_KH, GRID_W, blk), lambda h, bi, rb: (h, 0, 0, 0)),
            pl.BlockSpec((1, LANES, N_META), lambda h, bi, rb: (h, 0, 0)),
        ],
        out_specs=pl.BlockSpec((1, blk, LANES), lambda h, bi, rb: (bi, rb, h)),
        out_shape=jax.ShapeDtypeStruct((b, t, NA_WIDTH), BF16),
        scratch_shapes=[pltpu.VMEM((3 * blk, LANES), BF16), pltpu.VMEM((3 * blk, LANES), BF16),
                        pltpu.VMEM((2 * blk, LANES), BF16), pltpu.VMEM((2 * blk, blk), F32),
                        pltpu.VMEM((2 * blk, blk), BF16)],
        compiler_params=_cparams(("arbitrary", "arbitrary", "arbitrary")),
        name="na_attention",
    )(proj, proj, proj, proj, proj, proj, proj, proj_meta, proj_meta, table, mb_stack)


def _na_meta_body(p_ref, mb_ref, o_ref):
    for h in range(NA_HEADS // 2):
        qq = _split_heads(p_ref[:, (NAQ_BLK + h) * LANES:(NAQ_BLK + h + 1) * LANES])
        km = p_ref[:, (NAK_BLK + h) * LANES:(NAK_BLK + h + 1) * LANES]
        vm = p_ref[:, (NAV_BLK + h) * LANES:(NAV_BLK + h + 1) * LANES]
        s = _dot_nt(qq, km) + mb_ref[h, 0:2 * N_META, :]
        m = jnp.max(s, axis=1, keepdims=True)
        p = jnp.exp(s - m)
        l = jnp.sum(p, axis=1, keepdims=True)
        o = _dot(p.astype(BF16), vm) / l
        o_ref[:, h * LANES:(h + 1) * LANES] = _merge_heads(o).astype(BF16)


def _na_meta(proj_meta, mb_meta):
    return pl.pallas_call(
        _na_meta_body,
        out_shape=jax.ShapeDtypeStruct((N_META, NA_WIDTH), BF16),
        name="na_meta",
    )(proj_meta, mb_meta)


def _split_maps(q):
    lane = lax.broadcasted_iota(I32, q.shape, 1)
    is0 = ((lane >> 5) & 1) == 0
    zero = jnp.zeros_like(q)
    return jnp.concatenate([jnp.where(is0, q, zero), jnp.where(is0, zero, q)], axis=0)


def _online_softmax(s_ref, p_ref, a_ref, m_scr, rows):
    s = s_ref[rows, :]
    m_prev = m_scr[rows]
    m_new = jnp.maximum(m_prev, jnp.max(s, axis=1, keepdims=True))
    a_ref[rows] = jnp.exp2(m_prev - m_new)
    m_scr[rows] = m_new
    p_ref[rows, :] = jnp.exp2((s - m_new).astype(BF16))


def _diff_body(q_ref, k_ref, v_ref, km_ref, vm_ref, lq1_ref, lk1_ref, lq2_ref, lk2_ref, sub_ref,
               o_ref, qq_scr, vx_scr, s0, s1, p0, p1, a0, a1, m_scr, acc_scr, *, tq, tk, rc, nk):
    def kv_rows(ref, ki):
        return ref[0, pl.ds(pl.multiple_of(ki * tk, tk), tk), :]

    qq = _split_maps(q_ref[0])
    qq_scr[...] = qq
    s = _dot_nt(qq, km_ref[...])
    m = jnp.max(s, axis=1, keepdims=True)
    m_scr[...] = m
    vx_scr[:, LANES:] = jnp.ones((vx_scr.shape[0], LANES), BF16)
    vx_scr[0:N_META, 0:LANES] = vm_ref[...]
    acc_scr[...] = _dot(jnp.exp2((s - m).astype(BF16)), vx_scr[0:N_META, :])
    s0[...] = _dot_nt(qq, k_ref[0, 0:tk, :])
    p1[...] = jnp.zeros_like(p1)
    a1[...] = jnp.ones_like(a1)

    def step(ki, s_cur, s_nxt, p_cur, p_prv, a_cur, a_prv):
        kn = kv_rows(k_ref, jnp.minimum(ki + 1, nk - 1))
        vx_scr[:, 0:LANES] = kv_rows(v_ref, jnp.maximum(ki - 1, 0))
        for g in range(2 * tq // rc):
            rows = slice(g * rc, (g + 1) * rc)
            s_nxt[rows, :] = _dot_nt(qq_scr[rows, :], kn)
            acc_scr[rows, :] = a_prv[rows] * acc_scr[rows, :] + _dot(p_prv[rows, :], vx_scr[...])
            _online_softmax(s_cur, p_cur, a_cur, m_scr, rows)

    def block(ki, carry):
        @pl.when(ki % 2 == 0)
        def _():
            step(ki, s0, s1, p0, p1, a0, a1)

        @pl.when(ki % 2 == 1)
        def _():
            step(ki, s1, s0, p1, p0, a1, a0)

        return carry

    lax.fori_loop(0, nk, block, 0)

    p_last, a_last = (p0, a0) if (nk - 1) % 2 == 0 else (p1, a1)
    vx_scr[:, 0:LANES] = v_ref[0, (nk - 1) * tk:nk * tk, :]
    acc = a_last[...] * acc_scr[...] + _dot(p_last[...], vx_scr[...])
    lam = (jnp.exp(jnp.sum(lq1_ref[...] * lk1_ref[...], axis=1, keepdims=True))
           - jnp.exp(jnp.sum(lq2_ref[...] * lk2_ref[...], axis=1, keepdims=True)) + LAMBDA_INIT)
    o_all = acc[:, 0:LANES] / acc[:, LANES:]
    o = o_all[:tq] - lam * o_all[tq:]
    ms = jnp.mean(o * o, axis=-1, keepdims=True)
    o = o * lax.rsqrt(ms + SUBLN_EPS) * sub_ref[...] * (1.0 - LAMBDA_INIT)
    o_ref[0] = o.astype(BF16)


def _diff_attention(q_src, proj, proj_meta, lq1, lk1, lq2, lk2, subln, *, tq, tk, q_shared):
    b, t, _ = proj.shape
    tq_total = q_src.shape[1]
    nq = tq_total // tq
    nk = t // tk
    if q_shared:
        q_map = lambda bi, h, qi: (0, qi, DQ_BLK + h)
    else:
        q_map = lambda bi, h, qi: (bi, qi, DQ_BLK + h)
    vec = lambda n: pl.BlockSpec((1, n), lambda bi, h, qi: (0, 0))
    return pl.pallas_call(
        functools.partial(_diff_body, tq=tq, tk=tk, rc=min(512, 2 * tq), nk=nk),
        grid=(b, DIFF_HEADS, nq),
        in_specs=[
            pl.BlockSpec((1, tq, LANES), q_map),
            pl.BlockSpec((1, t, LANES), lambda bi, h, qi: (bi, 0, DK_BLK + h)),
            pl.BlockSpec((1, t, LANES), lambda bi, h, qi: (bi, 0, DV_BLK + h)),
            pl.BlockSpec((N_META, LANES), lambda bi, h, qi: (0, DK_BLK + h)),
            pl.BlockSpec((N_META, LANES), lambda bi, h, qi: (0, DV_BLK + h)),
            vec(HEAD_DIM), vec(HEAD_DIM), vec(HEAD_DIM), vec(HEAD_DIM), vec(2 * HEAD_DIM),
        ],
        out_specs=pl.BlockSpec((1, tq, LANES), lambda bi, h, qi: (bi, qi, h)),
        out_shape=jax.ShapeDtypeStruct((b, tq_total, DIFF_WIDTH), BF16),
        scratch_shapes=[
            pltpu.VMEM((2 * tq, LANES), BF16),
            pltpu.VMEM((tk, 2 * LANES), BF16),
            pltpu.VMEM((2 * tq, tk), F32), pltpu.VMEM((2 * tq, tk), F32),
            pltpu.VMEM((2 * tq, tk), BF16), pltpu.VMEM((2 * tq, tk), BF16),
            pltpu.VMEM((2 * tq, 1), F32), pltpu.VMEM((2 * tq, 1), F32),
            pltpu.VMEM((2 * tq, 1), F32),
            pltpu.VMEM((2 * tq, 2 * LANES), F32),
        ],
        compiler_params=_cparams(("parallel", "parallel", "parallel")),
        name="diff_attention",
    )(q_src, proj, proj, proj_meta, proj_meta, lq1, lk1, lq2, lk2, subln)


def _outproj_body(x_ref, na_ref, df_ref, wo_ref, g_ref, wrh_ref, wrl_ref, x2_ref, h2_ref, afft_ref, *aff_ref):
    x2 = x_ref[...] + _dot(na_ref[...], wo_ref[0:NA_WIDTH, :]) + _dot(df_ref[...], wo_ref[NA_WIDTH:, :])
    x2_ref[...] = x2
    ms = jnp.mean(x2 * x2, axis=-1, keepdims=True)
    hf = x2 * lax.rsqrt(ms + RMS_EPS) * g_ref[...]
    hi = hf.astype(BF16)
    h2_ref[...] = hi
    lo = (hf - hi.astype(F32)).astype(BF16)
    logits = _dot_nt(wrh_ref[...], hi) + _dot_nt(wrh_ref[...], lo) + _dot_nt(wrl_ref[...], hi)
    row = lax.broadcasted_iota(I32, logits.shape, 0)
    logits = jnp.where(row < N_EXPERTS, logits, NEG_BIG)
    m = jnp.max(logits, axis=0, keepdims=True)
    e = jnp.exp(logits - m)
    aff = e / jnp.sum(e, axis=0, keepdims=True)
    afft_ref[...] = aff[0:N_EXPERTS]
    if aff_ref:
        aff_ref[0][...] = aff.T


def _outproj(x, na, df, wo, g, wrh, wrl, *, tm, token_major):
    rows, d = x.shape
    out_shape = [jax.ShapeDtypeStruct((rows, d), F32), jax.ShapeDtypeStruct((rows, d), BF16),
                 jax.ShapeDtypeStruct((N_EXPERTS, rows), F32)]
    out_specs = [pl.BlockSpec((tm, d), lambda i: (i, 0)), pl.BlockSpec((tm, d), lambda i: (i, 0)),
                 pl.BlockSpec((N_EXPERTS, tm), lambda i: (0, i))]
    if token_major:
        out_shape.append(jax.ShapeDtypeStruct((rows, LANES), F32))
        out_specs.append(pl.BlockSpec((tm, LANES), lambda i: (i, 0)))
    return pl.pallas_call(
        _outproj_body,
        grid=(rows // tm,),
        in_specs=[
            pl.BlockSpec((tm, d), lambda i: (i, 0)),
            pl.BlockSpec((tm, NA_WIDTH), lambda i: (i, 0)),
            pl.BlockSpec((tm, DIFF_WIDTH), lambda i: (i, 0)),
            pl.BlockSpec((NA_WIDTH + DIFF_WIDTH, d), lambda i: (0, 0)),
            pl.BlockSpec((1, d), lambda i: (0, 0)),
            pl.BlockSpec((LANES, d), lambda i: (0, 0)),
            pl.BlockSpec((LANES, d), lambda i: (0, 0)),
        ],
        out_specs=out_specs,
        out_shape=out_shape,
        compiler_params=_cparams(("parallel",)),
        name="outproj_router",
    )(x, na, df, wo, g, wrh, wrl)


def _select_body(ar_ref, am_ref, gr_ref, gm_ref, tri_ref, tau_ref, cut_ref, offs_ref, cnt_ref, rank_ref, sel_scr,
                 *, cap, nb, idx_bits):
    br = pltpu.bitcast(ar_ref[...], I32)
    bm = pltpu.bitcast(am_ref[...], I32)
    gr = gr_ref[...]
    gm = gm_ref[...]

    def count(mr, mm):
        return (jnp.sum(jnp.where(mr, 1.0, 0.0), axis=1, keepdims=True)
                + jnp.sum(jnp.where(mm, 1.0, 0.0), axis=1, keepdims=True))

    def value_bit(i, ans):
        cand = ans | jnp.left_shift(jnp.int32(1), 30 - i)
        return jnp.where(count(br >= cand, bm >= cand) >= cap, cand, ans)

    tau = lax.fori_loop(0, 31, value_bit, jnp.zeros((N_EXPERTS, 1), I32))
    need = cap - count(br > tau, bm > tau)
    eq_r = br == tau
    eq_m = bm == tau

    def index_bit(i, ans):
        cand = ans | jnp.left_shift(jnp.int32(1), idx_bits - 1 - i)
        return jnp.where(count(eq_r & (gr < cand), eq_m & (gm < cand)) < need, cand, ans)

    cut = lax.fori_loop(0, idx_bits, index_bit, jnp.zeros((N_EXPERTS, 1), I32))
    tau_ref[...] = jnp.broadcast_to(tau, tau_ref.shape)
    cut_ref[...] = jnp.broadcast_to(cut, cut_ref.shape)

    sel_scr[...] = jnp.where((br > tau) | (eq_r & (gr <= cut)), 1.0, 0.0)
    lane = lax.broadcasted_iota(I32, (N_EXPERTS, LANES), 1)

    def block_count(j, acc):
        start = pl.multiple_of(j * TOK_BLK, TOK_BLK)
        sel = sel_scr[:, pl.ds(start, TOK_BLK)]
        rank_ref[:, pl.ds(start, TOK_BLK)] = sel * _dot(sel.astype(BF16), tri_ref[...])
        c = jnp.sum(sel, axis=1, keepdims=True)
        return jnp.where(lane == j, c, acc)

    counts = lax.fori_loop(0, nb, block_count, jnp.zeros((N_EXPERTS, LANES), F32))
    incl = counts
    shift = 1
    while shift < LANES:
        incl = incl + jnp.where(lane >= shift, pltpu.roll(incl, shift, 1), 0.0)
        shift *= 2
    offs_ref[...] = (incl - counts).astype(I32)
    cnt_ref[...] = counts.astype(I32)


def _select(aff_t, aff_t_meta, gidx, gidx_meta, tri_u, *, cap, n_total):
    e, nr = aff_t.shape
    nb = nr // TOK_BLK
    assert nb <= LANES and nr % TOK_BLK == 0
    idx_bits = max(1, (n_total - 1).bit_length())
    shp = jax.ShapeDtypeStruct((e, LANES), I32)
    return pl.pallas_call(
        functools.partial(_select_body, cap=cap, nb=nb, idx_bits=idx_bits),
        out_shape=[shp, shp, shp, shp, jax.ShapeDtypeStruct((e, nr), F32)],
        scratch_shapes=[pltpu.VMEM((e, nr), F32)],
        compiler_params=pltpu.CompilerParams(vmem_limit_bytes=VMEM_LIMIT),
        name="ec_select",
    )(aff_t, aff_t_meta, gidx, gidx_meta, tri_u)


def _gather_body(offs_ref, cnts_ref, rank_ref, h_hbm, xe_ref, hbuf, sems, *, nb, n_steps):
    e = pl.program_id(0)

    def h_copy(s):
        slot = s % GATHER_RING
        rows = pl.ds(pl.multiple_of((s % nb) * TOK_BLK, TOK_BLK), TOK_BLK)
        return pltpu.make_async_copy(h_hbm.at[rows, :], hbuf.at[slot], sems.at[slot])

    @pl.when(e == 0)
    def _():
        for s in range(GATHER_RING - 1):
            h_copy(s).start()

    xe_ref[...] = jnp.zeros_like(xe_ref)
    rowid = lax.broadcasted_iota(I32, (GATHER_WIN, TOK_BLK), 0)

    def block(j, carry):
        step = e * nb + j

        @pl.when(step + GATHER_RING - 1 < n_steps)
        def _():
            h_copy(step + GATHER_RING - 1).start()

        rank = rank_ref[0, :, pl.ds(pl.multiple_of(j * TOK_BLK, TOK_BLK), TOK_BLK)].astype(I32)
        cnt = cnts_ref[step]
        off = offs_ref[step]
        off_al = (off // 16) * 16
        rel = jnp.where(rank > 0, rank - 1 + (off - off_al), -1)
        nchunk = (off - off_al + cnt + GATHER_WIN - 1) // GATHER_WIN
        h_copy(step).wait()
        slot = step % GATHER_RING

        def chunk(c, inner):
            onehot = jnp.where(rowid == rel - c * GATHER_WIN, 1.0, 0.0).astype(BF16)
            g = _dot(onehot, hbuf[slot]).astype(BF16)
            start = pl.multiple_of(off_al + c * GATHER_WIN, 16)
            xe_ref[0, pl.ds(start, GATHER_WIN), :] = xe_ref[0, pl.ds(start, GATHER_WIN), :] + g
            return inner

        lax.fori_loop(0, nchunk, chunk, 0)
        return carry

    lax.fori_loop(0, nb, block, 0)


def _gather(offs_flat, cnts_flat, rank3, h2, *, cap_pad):
    e = rank3.shape[0]
    nr, d = h2.shape
    nb = nr // TOK_BLK
    return pl.pallas_call(
        functools.partial(_gather_body, nb=nb, n_steps=e * nb),
        grid_spec=pltpu.PrefetchScalarGridSpec(
            num_scalar_prefetch=2,
            grid=(e,),
            in_specs=[
                pl.BlockSpec((1, 1, nr), lambda ei, o, c: (ei, 0, 0)),
                pl.BlockSpec(memory_space=pl.ANY),
            ],
            out_specs=pl.BlockSpec((1, cap_pad, d), lambda ei, o, c: (ei, 0, 0)),
            scratch_shapes=[pltpu.VMEM((GATHER_RING, TOK_BLK, d), BF16),
                            pltpu.SemaphoreType.DMA((GATHER_RING,))],
        ),
        out_shape=jax.ShapeDtypeStruct((e, cap_pad, d), BF16),
        compiler_params=_cparams(("arbitrary",)),
        name="ec_gather",
    )(offs_flat, cnts_flat, rank3, h2)


def _ffn_body(x_ref, wg_ref, wu_ref, wd_ref, o_ref, *, fc):
    x = x_ref[0]
    f = wg_ref.shape[2]
    acc = jnp.zeros((x.shape[0], wd_ref.shape[2]), F32)
    for c in range(f // fc):
        a = _dot(x, wg_ref[0, :, c * fc:(c + 1) * fc])
        b = _dot(x, wu_ref[0, :, c * fc:(c + 1) * fc])
        hmid = (a * jax.nn.sigmoid(a) * b).astype(BF16)
        acc = acc + _dot(hmid, wd_ref[0, c * fc:(c + 1) * fc, :])
    o_ref[0] = acc.astype(BF16)


def _ffn(xe, wg, wu, wd, *, tm):
    e, cap_pad, d = xe.shape
    f = wg.shape[2]
    return pl.pallas_call(
        functools.partial(_ffn_body, fc=min(512, f)),
        grid=(e, cap_pad // tm),
        in_specs=[
            pl.BlockSpec((1, tm, d), lambda ei, i: (ei, i, 0)),
            pl.BlockSpec((1, d, f), lambda ei, i: (ei, 0, 0)),
            pl.BlockSpec((1, d, f), lambda ei, i: (ei, 0, 0)),
            pl.BlockSpec((1, f, d), lambda ei, i: (ei, 0, 0)),
        ],
        out_specs=pl.BlockSpec((1, tm, d), lambda ei, i: (ei, i, 0)),
        out_shape=jax.ShapeDtypeStruct((e, cap_pad, d), BF16),
        compiler_params=_cparams(("parallel", "arbitrary")),
        name="ec_ffn",
    )(xe, wg, wu, wd)


def _combine_body(offs_ref, cnts_ref, aff_ref, tau_ref, cut_ref, x2_ref, tri_ref, fn_ref, *rest, nb, t, l_total):
    ywin_refs = rest[:COMBINE_EXPERTS]
    ye_hbm, o_ref, rank_scr, gate_scr, acc_scr, ybuf, sem = rest[COMBINE_EXPERTS:]
    j = pl.program_id(0)
    eg = pl.program_id(1)

    @pl.when(eg == 0)
    def _():
        aff = aff_ref[...]
        bits = pltpu.bitcast(aff, I32)
        tau = tau_ref[0:1, :]
        cut = cut_ref[0:1, :]
        row0 = j * TOK_BLK
        gidx = row0 + (row0 // t) * (l_total - t) + (l_total - t) + lax.broadcasted_iota(I32, bits.shape, 0)
        sel = (bits > tau) | ((bits == tau) & (gidx <= cut))
        sel = sel & (lax.broadcasted_iota(I32, bits.shape, 1) < N_EXPERTS)
        self32 = jnp.where(sel, 1.0, 0.0)
        incl = _dot(tri_ref[...], self32.astype(BF16))
        rank_scr[...] = jnp.where(sel, incl - 1.0, -1.0)
        gate_scr[...] = jnp.where(sel, aff, 0.0)
        acc_scr[...] = x2_ref[...]

    lane = lax.broadcasted_iota(I32, (TOK_BLK, LANES), 1)
    colid = lax.broadcasted_iota(I32, (TOK_BLK, ROW_WIN), 1)

    def placement(e):
        pick = lane == e
        rank = jnp.sum(jnp.where(pick, rank_scr[...], 0.0), axis=1, keepdims=True).astype(I32)
        gate = jnp.sum(jnp.where(pick, gate_scr[...], 0.0), axis=1, keepdims=True)
        off = offs_ref[e * nb + j]
        off_al = (off // 16) * 16
        rel = jnp.where(rank >= 0, rank + (off - off_al), -1)
        return rel, gate, off_al, (off - off_al + cnts_ref[e * nb + j] + ROW_WIN - 1) // ROW_WIN

    acc = acc_scr[...]
    for u in range(COMBINE_EXPERTS):
        rel, gate, _, _ = placement(eg * COMBINE_EXPERTS + u)
        onehot = jnp.where(colid == rel, 1.0, 0.0).astype(BF16)
        acc = acc + _dot(onehot, ywin_refs[u][...]) * gate
    acc_scr[...] = acc

    for u in range(COMBINE_EXPERTS):
        e = eg * COMBINE_EXPERTS + u
        nchunk = placement(e)[3]

        def chunk(c, carry, e=e):
            rel, gate, off_al, _ = placement(e)
            start = pl.multiple_of(off_al + c * ROW_WIN, 16)
            cp = pltpu.make_async_copy(ye_hbm.at[e, pl.ds(start, ROW_WIN), :], ybuf, sem)
            cp.start()
            cp.wait()
            oh = jnp.where(colid == rel - c * ROW_WIN, 1.0, 0.0).astype(BF16)
            acc_scr[...] += _dot(oh, ybuf[...]) * gate
            return carry

        lax.fori_loop(1, nchunk, chunk, 0)

    @pl.when(eg == pl.num_programs(1) - 1)
    def _():
        y = acc_scr[...]
        ms = jnp.mean(y * y, axis=-1, keepdims=True)
        o_ref[...] = y * lax.rsqrt(ms + RMS_EPS) * fn_ref[...]


def _combine(offs_flat, cnts_flat, aff, tau_row, cut_row, x2, tri_l, fnorm, ye, *, t, l_total):
    nr, d = x2.shape
    nb = nr // TOK_BLK
    e = ye.shape[0]

    def win_spec(u):
        def win_map(j, eg, o, c):
            ei = eg * COMBINE_EXPERTS + u
            return (ei, (o[ei * nb + j] // 16) * 16, 0)
        return pl.BlockSpec((pl.Squeezed(), pl.Element(ROW_WIN), pl.Element(d)), win_map)

    return pl.pallas_call(
        functools.partial(_combine_body, nb=nb, t=t, l_total=l_total),
        grid_spec=pltpu.PrefetchScalarGridSpec(
            num_scalar_prefetch=2,
            grid=(nb, e // COMBINE_EXPERTS),
            in_specs=[
                pl.BlockSpec((TOK_BLK, LANES), lambda j, ei, o, c: (j, 0)),
                pl.BlockSpec((8, LANES), lambda j, ei, o, c: (0, 0)),
                pl.BlockSpec((8, LANES), lambda j, ei, o, c: (0, 0)),
                pl.BlockSpec((TOK_BLK, d), lambda j, ei, o, c: (j, 0)),
                pl.BlockSpec((TOK_BLK, TOK_BLK), lambda j, ei, o, c: (0, 0)),
                pl.BlockSpec((1, d), lambda j, ei, o, c: (0, 0)),
                *[win_spec(u) for u in range(COMBINE_EXPERTS)],
                pl.BlockSpec(memory_space=pl.ANY),
            ],
            out_specs=pl.BlockSpec((TOK_BLK, d), lambda j, ei, o, c: (j, 0)),
            scratch_shapes=[
                pltpu.VMEM((TOK_BLK, LANES), F32),
                pltpu.VMEM((TOK_BLK, LANES), F32),
                pltpu.VMEM((TOK_BLK, d), F32),
                pltpu.VMEM((ROW_WIN, d), BF16),
                pltpu.SemaphoreType.DMA(()),
            ],
        ),
        out_shape=jax.ShapeDtypeStruct((nr, d), F32),
        compiler_params=_cparams(("parallel", "arbitrary")),
        name="ec_combine",
    )(offs_flat, cnts_flat, aff, tau_row, cut_row, x2, tri_l, fnorm, *([ye] * COMBINE_EXPERTS), ye)


def _rope_tables(positions):
    inv_freq = 1.0 / (ROPE_THETA ** (jnp.arange(0, HEAD_DIM, 2, dtype=F32) / HEAD_DIM))
    ang = positions.astype(F32)[:, None] * inv_freq[None, :]
    cos, sin = jnp.cos(ang), jnp.sin(ang)
    return (jnp.concatenate([cos, cos, cos, cos], axis=1),
            jnp.concatenate([-sin, -sin, sin, sin], axis=1))


def _permute_in_weights(w_in):
    l = jnp.arange(LANES)
    g = l // 32
    within = (g % 2) * HEAD_DIM + (g // 2) * 32 + l % 32
    cols = jnp.arange(PROJ_WIDTH)
    blk = cols // LANES
    permuted = blk * LANES + within[cols % LANES]
    cols = jnp.where((blk >= DQ_BLK) & (blk < DV_BLK), permuted, cols)
    return w_in[:, cols]


def _encode_group(x, shared):
    b, t, d = x.shape
    l_total = N_META + t
    n_total = b * l_total
    cap = EC_CAPACITY_FACTOR * n_total // N_EXPERTS
    rows = b * t
    xf = x.reshape(rows, d)

    cos, sin = _rope_tables(N_META + jnp.arange(t))
    proj = _inproj(xf, shared["norm_mix"], shared["w_in"], cos, sin, tm=512).reshape(b, t, PROJ_WIDTH)
    proj_meta = shared["proj_meta"]

    na = _na_attention(proj, proj_meta, shared["na_table"], shared["mb_stack"])
    lam = shared["lambda"]
    tk = min(1024, t)
    df = _diff_attention(proj, proj, proj_meta, *lam, shared["subln"], tq=min(1024, t), tk=tk, q_shared=False)
    df_meta = _diff_attention(proj_meta[None], proj, proj_meta, *lam, shared["subln"],
                              tq=N_META, tk=tk, q_shared=True)

    wr = (shared["wr_hi"], shared["wr_lo"])
    x2, h2, aff_t, aff = _outproj(xf, na.reshape(rows, NA_WIDTH), df.reshape(rows, DIFF_WIDTH), shared["w_out"],
                                  shared["norm_ffn"], *wr, tm=512, token_major=True)
    xm = jnp.tile(shared["meta_tokens"], (b, 1))
    nam = jnp.tile(shared["na_meta"], (b, 1))
    _, _, aff_t_meta = _outproj(xm, nam, df_meta.reshape(b * N_META, DIFF_WIDTH), shared["w_out"],
                                shared["norm_ffn"], *wr, tm=b * N_META, token_major=False)

    r = jnp.arange(rows, dtype=I32)
    gidx = (r + (r // t + 1) * N_META)[None, :]
    rm = jnp.arange(b * N_META, dtype=I32)
    gidx_meta = ((rm // N_META) * l_total + rm % N_META)[None, :]
    tau, cut, offs, cnts, rank_t = _select(aff_t, aff_t_meta, gidx, gidx_meta, shared["tri_u"],
                                           cap=cap, n_total=n_total)

    nb = rows // TOK_BLK
    offs_flat = offs[:, :nb].reshape(-1)
    cnts_flat = cnts[:, :nb].reshape(-1)
    cap_pad = -(-(cap + ROW_WIN) // 256) * 256
    ffn_tm = max(tm for tm in range(16, 1025, 16) if cap_pad % tm == 0)
    xe = _gather(offs_flat, cnts_flat, rank_t.reshape(N_EXPERTS, 1, rows), h2, cap_pad=cap_pad)
    ye = _ffn(xe, shared["w_gate"], shared["w_up"], shared["w_down"], tm=ffn_tm)

    pad = jnp.zeros((8, LANES - N_EXPERTS), I32)
    tau_row = jnp.concatenate([jnp.broadcast_to(tau[:, 0][None, :], (8, N_EXPERTS)), pad], axis=1)
    cut_row = jnp.concatenate([jnp.broadcast_to(cut[:, 0][None, :], (8, N_EXPERTS)), pad], axis=1)
    y = _combine(offs_flat, cnts_flat, aff, tau_row, cut_row, x2, shared["tri_l"], shared["final_norm"], ye,
                 t=t, l_total=l_total)
    return y.reshape(b, t, d)


def kernel(x_prompt, x_sample, meta_tokens, norm_mix, w_in, na_rpb, na_meta_bias, lambda_q1, lambda_k1,
           lambda_q2, lambda_k2, diff_subln, w_out, norm_ffn, w_router, w_gate, w_up, w_down, final_norm):
    d = x_prompt.shape[-1]
    wr = jnp.zeros((LANES, d), F32).at[:N_EXPERTS].set(w_router[0].T)
    wr_hi = wr.astype(BF16)
    mb = na_meta_bias[0].astype(F32)
    idx = jnp.arange(TOK_BLK)
    shared = {
        "meta_tokens": meta_tokens,
        "norm_mix": norm_mix[0][None, :],
        "w_in": _permute_in_weights(w_in[0]).astype(BF16),
        "lambda": (lambda_q1, lambda_k1, lambda_q2, lambda_k2),
        "subln": diff_subln,
        "w_out": w_out[0].astype(BF16),
        "norm_ffn": norm_ffn[0][None, :],
        "wr_hi": wr_hi,
        "wr_lo": (wr - wr_hi.astype(F32)).astype(BF16),
        "w_gate": w_gate[0].astype(BF16),
        "w_up": w_up[0].astype(BF16),
        "w_down": w_down[0].astype(BF16),
        "final_norm": final_norm[None, :],
        "mb_stack": jnp.repeat(mb, GRID_W, axis=0).reshape(NA_HEADS // 2, 2 * GRID_W, N_META),
        "tri_u": (idx[:, None] <= idx[None, :]).astype(BF16),
        "tri_l": (idx[:, None] >= idx[None, :]).astype(BF16),
    }
    cos_m, sin_m = _rope_tables(jnp.arange(N_META))
    shared["proj_meta"] = _inproj(meta_tokens, shared["norm_mix"], shared["w_in"], cos_m, sin_m, tm=N_META)
    mb_meta = jnp.repeat(mb, N_META, axis=0).reshape(NA_HEADS // 2, 2 * N_META, N_META)
    shared["na_meta"] = _na_meta(shared["proj_meta"], mb_meta)
    shared["na_table"] = _na_bias(na_rpb[0].astype(F32))
    return (_encode_group(x_prompt, shared), _encode_group(x_sample, shared))
```

```python
import functools
import math

import jax
import jax.numpy as jnp
from jax import lax
from jax.experimental import pallas as pl
from jax.experimental.pallas import tpu as pltpu

BF16 = jnp.bfloat16
F32 = jnp.float32
I32 = jnp.int32

N_META = 16
GRID_W = 64
HEAD_DIM = 64
NA_HEADS = 8
NA_WIDTH = NA_HEADS * HEAD_DIM
NA_KH = 8
NA_KW = 16
DIFF_HEADS = 4
DIFF_WIDTH = DIFF_HEADS * 2 * HEAD_DIM
PROJ_WIDTH = 3 * NA_WIDTH + 3 * DIFF_WIDTH
ROPE_THETA = 10000.0
N_EXPERTS = 16
EC_CAPACITY_FACTOR = 2
RMS_EPS = 1e-6
SUBLN_EPS = 1e-5
LAMBDA_INIT = 0.8 - 0.6 * math.exp(-0.3 * 0)
LOG2E = math.log2(math.e)

LANES = 128
NEG_BIG = -1e30
VMEM_LIMIT = 56 * 1024 * 1024

NAQ_BLK, NAK_BLK, NAV_BLK = 0, 4, 8
DQ_BLK, DK_BLK, DV_BLK = 12, 16, 20

ROW_WIN = 256
GATHER_WIN = 192
GATHER_RING = 3
GATHER_EXPERTS = 2
TOK_BLK = 1024
COMBINE_EXPERTS = 4


def _cparams(sem, vmem=VMEM_LIMIT):
    return pltpu.CompilerParams(dimension_semantics=sem, vmem_limit_bytes=vmem)


def _dot(a, b):
    return jnp.dot(a, b, preferred_element_type=F32)


def _dot_nt(a, b):
    return lax.dot_general(a, b, (((1,), (1,)), ((), ())), preferred_element_type=F32)


def _inproj_body(x_ref, g_ref, w_ref, cos_ref, sin_ref, o_ref, *, tn):
    x = x_ref[...]
    ms = jnp.mean(x * x, axis=-1, keepdims=True)
    h = (x * lax.rsqrt(ms + RMS_EPS) * g_ref[...]).astype(BF16)
    cos = cos_ref[...]
    sin = sin_ref[...]
    for c in range(PROJ_WIDTH // tn):
        lo = c * tn
        acc = _dot(h, w_ref[:, lo:lo + tn])
        for s in range(tn // LANES):
            blk = (lo + s * LANES) // LANES
            a = acc[:, s * LANES:(s + 1) * LANES]
            if DQ_BLK <= blk < DV_BLK:
                a = a * cos + pltpu.roll(a, 64, 1) * sin
            if blk < NAK_BLK:
                a = a * (HEAD_DIM ** -0.5)
            if DQ_BLK <= blk < DK_BLK:
                a = a * (HEAD_DIM ** -0.5 * LOG2E)
            o_ref[:, lo + s * LANES:lo + (s + 1) * LANES] = a.astype(BF16)


def _inproj(x, g, w, cos, sin, *, tm):
    rows, d = x.shape
    nt = cos.shape[0] // tm
    return pl.pallas_call(
        functools.partial(_inproj_body, tn=512),
        grid=(rows // tm,),
        in_specs=[
            pl.BlockSpec((tm, d), lambda i: (i, 0)),
            pl.BlockSpec((1, d), lambda i: (0, 0)),
            pl.BlockSpec((d, PROJ_WIDTH), lambda i: (0, 0)),
            pl.BlockSpec((tm, LANES), lambda i: (i % nt, 0)),
            pl.BlockSpec((tm, LANES), lambda i: (i % nt, 0)),
        ],
        out_specs=pl.BlockSpec((tm, PROJ_WIDTH), lambda i: (i, 0)),
        out_shape=jax.ShapeDtypeStruct((rows, PROJ_WIDTH), BF16),
        compiler_params=_cparams(("parallel",)),
        name="inproj",
    )(x, g, w, cos, sin)


def _na_bias_body(rpb_ref, o_ref):
    h = pl.program_id(0)
    c = lax.broadcasted_iota(I32, (GRID_W, GRID_W), 0)
    kc = lax.broadcasted_iota(I32, (GRID_W, GRID_W), 1)
    cs = jnp.clip(c - NA_KW // 2, 0, GRID_W - NA_KW)
    valid = (kc >= cs) & (kc < cs + NA_KW)
    jm = kc - c + NA_KW - 1
    n_dr = 2 * NA_KH - 1
    n_j = 2 * NA_KW - 1
    tiles = []
    for dr in range(n_dr):
        base = (h * n_dr + dr) * n_j

        def body(j, acc, base=base):
            return jnp.where(jm == j, rpb_ref[base + j], acc)

        t = lax.fori_loop(0, n_j, body, jnp.zeros((GRID_W, GRID_W), F32))
        tiles.append(jnp.where(valid, t, NEG_BIG))
    for d in range(NA_KH):
        o_ref[0, d] = jnp.concatenate([tiles[kr - d + NA_KH - 1] for kr in range(NA_KH)], axis=1)


def _na_bias(rpb):
    h = rpb.shape[0]
    return pl.pallas_call(
        _na_bias_body,
        grid_spec=pltpu.PrefetchScalarGridSpec(
            num_scalar_prefetch=1,
            grid=(h,),
            in_specs=[],
            out_specs=pl.BlockSpec((1, NA_KH, GRID_W, NA_KH * GRID_W), lambda i, r: (i, 0, 0, 0)),
        ),
        out_shape=jax.ShapeDtypeStruct((h, NA_KH, GRID_W, NA_KH * GRID_W), F32),
        compiler_params=_cparams(("arbitrary",)),
        name="na_bias",
    )(rpb.reshape(-1))


def _split_heads(q):
    lane = lax.broadcasted_iota(I32, q.shape, 1)
    zero = jnp.zeros_like(q)
    return jnp.concatenate([jnp.where(lane < HEAD_DIM, q, zero), jnp.where(lane < HEAD_DIM, zero, q)], axis=0)


def _merge_heads(o):
    n = o.shape[0] // 2
    lane = lax.broadcasted_iota(I32, (n, LANES), 1)
    return jnp.where(lane < HEAD_DIM, o[:n], o[n:])


def _na_body(q_ref, kp_ref, kc_ref, kn_ref, vp_ref, vc_ref, vn_ref, km_ref, vm_ref, tab_ref, mb_ref,
             o_ref, kbuf, vbuf, qq_scr, s_scr, p_scr, *, rows):
    rb = pl.program_id(2)
    blk = NA_KH * GRID_W
    kbuf[0:blk] = kp_ref[0]
    kbuf[blk:2 * blk] = kc_ref[0]
    kbuf[2 * blk:3 * blk] = kn_ref[0]
    vbuf[0:blk] = vp_ref[0]
    vbuf[blk:2 * blk] = vc_ref[0]
    vbuf[2 * blk:3 * blk] = vn_ref[0]
    two = 2 * GRID_W
    starts = []
    for i in range(NA_KH):
        r = rb * NA_KH + i
        rs = jnp.clip(r - NA_KH // 2, 0, rows - NA_KH)
        d = r - rs
        start = pl.multiple_of((rs - (rb - 1) * NA_KH) * GRID_W, GRID_W)
        starts.append(start)
        qq = _split_heads(q_ref[0, i * GRID_W:(i + 1) * GRID_W, :])
        qq_scr[i * two:(i + 1) * two, :] = qq
        bias = jnp.concatenate([tab_ref[0, d], tab_ref[1, d]], axis=0)
        s_scr[i * two:(i + 1) * two, :] = _dot_nt(qq, kbuf[pl.ds(start, blk), :]) + bias
    s = s_scr[...]
    sm = (_dot_nt(qq_scr[...], km_ref[...]).reshape(NA_KH, two, N_META) + mb_ref[0][None]).reshape(NA_KH * two, N_META)
    m = jnp.maximum(jnp.max(s, axis=1, keepdims=True), jnp.max(sm, axis=1, keepdims=True))
    p = jnp.exp(s - m)
    pm = jnp.exp(sm - m)
    l = jnp.sum(p, axis=1, keepdims=True) + jnp.sum(pm, axis=1, keepdims=True)
    p_scr[...] = p.astype(BF16)
    om = _dot(pm.astype(BF16), vm_ref[...])
    inv_l = 1.0 / l
    for i in range(NA_KH):
        rsl = slice(i * two, (i + 1) * two)
        o = (_dot(p_scr[rsl, :], vbuf[pl.ds(starts[i], blk), :]) + om[rsl]) * inv_l[rsl]
        o_ref[0, i * GRID_W:(i + 1) * GRID_W, :] = _merge_heads(o).astype(BF16)


def _na_attention(proj, proj_meta, table, mb_stack):
    b, t, _ = proj.shape
    rows = t // GRID_W
    nrb = rows // NA_KH
    blk = NA_KH * GRID_W
    hp = NA_HEADS // 2

    def kv_spec(col, shift):
        return pl.BlockSpec((1, blk, LANES),
                            lambda h, bi, rb: (bi, jnp.clip(rb + shift, 0, nrb - 1), col + h))

    return pl.pallas_call(
        functools.partial(_na_body, rows=rows),
        grid=(hp, b, nrb),
        in_specs=[
            pl.BlockSpec((1, blk, LANES), lambda h, bi, rb: (bi, rb, NAQ_BLK + h)),
            kv_spec(NAK_BLK, -1), kv_spec(NAK_BLK, 0), kv_spec(NAK_BLK, 1),
            kv_spec(NAV_BLK, -1), kv_spec(NAV_BLK, 0), kv_spec(NAV_BLK, 1),
            pl.BlockSpec((N_META, LANES), lambda h, bi, rb: (0, NAK_BLK + h)),
            pl.BlockSpec((N_META, LANES), lambda h, bi, rb: (0, NAV_BLK + h)),
            pl.BlockSpec((2, NA_KH, GRID_W, blk), lambda h, bi, rb: (h, 0, 0, 0)),
            pl.BlockSpec((1, LANES, N_META), lambda h, bi, rb: (h, 0, 0)),
        ],
        out_specs=pl.BlockSpec((1, blk, LANES), lambda h, bi, rb: (bi, rb, h)),
        out_shape=jax.ShapeDtypeStruct((b, t, NA_WIDTH), BF16),
        scratch_shapes=[pltpu.VMEM((3 * blk, LANES), BF16), pltpu.VMEM((3 * blk, LANES), BF16),
                        pltpu.VMEM((2 * blk, LANES), BF16), pltpu.VMEM((2 * blk, blk), F32),
                        pltpu.VMEM((2 * blk, blk), BF16)],
        compiler_params=_cparams(("arbitrary", "arbitrary", "arbitrary")),
        name="na_attention",
    )(proj, proj, proj, proj, proj, proj, proj, proj_meta, proj_meta, table, mb_stack)


def _na_meta_body(p_ref, mb_ref, o_ref):
    for h in range(NA_HEADS // 2):
        qq = _split_heads(p_ref[:, (NAQ_BLK + h) * LANES:(NAQ_BLK + h + 1) * LANES])
        km = p_ref[:, (NAK_BLK + h) * LANES:(NAK_BLK + h + 1) * LANES]
        vm = p_ref[:, (NAV_BLK + h) * LANES:(NAV_BLK + h + 1) * LANES]
        s = _dot_nt(qq, km) + mb_ref[h, 0:2 * N_META, :]
        m = jnp.max(s, axis=1, keepdims=True)
        p = jnp.exp(s - m)
        l = jnp.sum(p, axis=1, keepdims=True)
        o = _dot(p.astype(BF16), vm) / l
        o_ref[:, h * LANES:(h + 1) * LANES] = _merge_heads(o).astype(BF16)


def _na_meta(proj_meta, mb_meta):
    return pl.pallas_call(
        _na_meta_body,
        out_shape=jax.ShapeDtypeStruct((N_META, NA_WIDTH), BF16),
        name="na_meta",
    )(proj_meta, mb_meta)


def _split_maps(q):
    lane = lax.broadcasted_iota(I32, q.shape, 1)
    is0 = ((lane >> 5) & 1) == 0
    zero = jnp.zeros_like(q)
    return jnp.concatenate([jnp.where(is0, q, zero), jnp.where(is0, zero, q)], axis=0)


def _online_softmax(s_ref, p_ref, a_ref, m_scr, rows):
    s = s_ref[rows, :]
    m_prev = m_scr[rows]
    m_new = jnp.maximum(m_prev, jnp.max(s, axis=1, keepdims=True))
    a_ref[rows] = jnp.exp2(m_prev - m_new)
    m_scr[rows] = m_new
    p_ref[rows, :] = jnp.exp2((s - m_new).astype(BF16))


def _diff_body(q_ref, k_ref, v_ref, km_ref, vm_ref, lq1_ref, lk1_ref, lq2_ref, lk2_ref, sub_ref,
               o_ref, qq_scr, vx_scr, s0, s1, p0, p1, a0, a1, m_scr, acc_scr, *, tq, tk, rc, nk):
    def kv_rows(ref, ki):
        return ref[0, pl.ds(pl.multiple_of(ki * tk, tk), tk), :]

    qq = _split_maps(q_ref[0])
    qq_scr[...] = qq
    s = _dot_nt(qq, km_ref[...])
    m = jnp.max(s, axis=1, keepdims=True)
    m_scr[...] = m
    vx_scr[:, LANES:] = jnp.ones((vx_scr.shape[0], LANES), BF16)
    vx_scr[0:N_META, 0:LANES] = vm_ref[...]
    acc_scr[...] = _dot(jnp.exp2((s - m).astype(BF16)), vx_scr[0:N_META, :])
    s0[...] = _dot_nt(qq, k_ref[0, 0:tk, :])
    p1[...] = jnp.zeros_like(p1)
    a1[...] = jnp.ones_like(a1)

    def step(ki, s_cur, s_nxt, p_cur, p_prv, a_cur, a_prv):
        kn = kv_rows(k_ref, jnp.minimum(ki + 1, nk - 1))
        vx_scr[:, 0:LANES] = kv_rows(v_ref, jnp.maximum(ki - 1, 0))
        for g in range(2 * tq // rc):
            rows = slice(g * rc, (g + 1) * rc)
            s_nxt[rows, :] = _dot_nt(qq_scr[rows, :], kn)
            acc_scr[rows, :] = a_prv[rows] * acc_scr[rows, :] + _dot(p_prv[rows, :], vx_scr[...])
            _online_softmax(s_cur, p_cur, a_cur, m_scr, rows)

    def block(ki, carry):
        @pl.when(ki % 2 == 0)
        def _():
            step(ki, s0, s1, p0, p1, a0, a1)

        @pl.when(ki % 2 == 1)
        def _():
            step(ki, s1, s0, p1, p0, a1, a0)

        return carry

    lax.fori_loop(0, nk, block, 0)

    p_last, a_last = (p0, a0) if (nk - 1) % 2 == 0 else (p1, a1)
    vx_scr[:, 0:LANES] = v_ref[0, (nk - 1) * tk:nk * tk, :]
    acc = a_last[...] * acc_scr[...] + _dot(p_last[...], vx_scr[...])
    lam = (jnp.exp(jnp.sum(lq1_ref[...] * lk1_ref[...], axis=1, keepdims=True))
           - jnp.exp(jnp.sum(lq2_ref[...] * lk2_ref[...], axis=1, keepdims=True)) + LAMBDA_INIT)
    o_all = acc[:, 0:LANES] / acc[:, LANES:]
    o = o_all[:tq] - lam * o_all[tq:]
    ms = jnp.mean(o * o, axis=-1, keepdims=True)
    o = o * lax.rsqrt(ms + SUBLN_EPS) * sub_ref[...] * (1.0 - LAMBDA_INIT)
    o_ref[0] = o.astype(BF16)


def _diff_attention(q_src, proj, proj_meta, lq1, lk1, lq2, lk2, subln, *, tq, tk, q_shared):
    b, t, _ = proj.shape
    tq_total = q_src.shape[1]
    nq = tq_total // tq
    nk = t // tk
    if q_shared:
        q_map = lambda bi, h, qi: (0, qi, DQ_BLK + h)
    else:
        q_map = lambda bi, h, qi: (bi, qi, DQ_BLK + h)
    vec = lambda n: pl.BlockSpec((1, n), lambda bi, h, qi: (0, 0))
    return pl.pallas_call(
        functools.partial(_diff_body, tq=tq, tk=tk, rc=min(512, 2 * tq), nk=nk),
        grid=(b, DIFF_HEADS, nq),
        in_specs=[
            pl.BlockSpec((1, tq, LANES), q_map),
            pl.BlockSpec((1, t, LANES), lambda bi, h, qi: (bi, 0, DK_BLK + h)),
            pl.BlockSpec((1, t, LANES), lambda bi, h, qi: (bi, 0, DV_BLK + h)),
            pl.BlockSpec((N_META, LANES), lambda bi, h, qi: (0, DK_BLK + h)),
            pl.BlockSpec((N_META, LANES), lambda bi, h, qi: (0, DV_BLK + h)),
            vec(HEAD_DIM), vec(HEAD_DIM), vec(HEAD_DIM), vec(HEAD_DIM), vec(2 * HEAD_DIM),
        ],
        out_specs=pl.BlockSpec((1, tq, LANES), lambda bi, h, qi: (bi, qi, h)),
        out_shape=jax.ShapeDtypeStruct((b, tq_total, DIFF_WIDTH), BF16),
        scratch_shapes=[
            pltpu.VMEM((2 * tq, LANES), BF16),
            pltpu.VMEM((tk, 2 * LANES), BF16),
            pltpu.VMEM((2 * tq, tk), F32), pltpu.VMEM((2 * tq, tk), F32),
            pltpu.VMEM((2 * tq, tk), BF16), pltpu.VMEM((2 * tq, tk), BF16),
            pltpu.VMEM((2 * tq, 1), F32), pltpu.VMEM((2 * tq, 1), F32),
            pltpu.VMEM((2 * tq, 1), F32),
            pltpu.VMEM((2 * tq, 2 * LANES), F32),
        ],
        compiler_params=_cparams(("parallel", "parallel", "parallel")),
        name="diff_attention",
    )(q_src, proj, proj, proj_meta, proj_meta, lq1, lk1, lq2, lk2, subln)


def _outproj_body(x_ref, na_ref, df_ref, wo_ref, g_ref, wrh_ref, wrl_ref, x2_ref, h2_ref, afft_ref, *aff_ref):
    x2 = x_ref[...] + _dot(na_ref[...], wo_ref[0:NA_WIDTH, :]) + _dot(df_ref[...], wo_ref[NA_WIDTH:, :])
    x2_ref[...] = x2
    ms = jnp.mean(x2 * x2, axis=-1, keepdims=True)
    hf = x2 * lax.rsqrt(ms + RMS_EPS) * g_ref[...]
    hi = hf.astype(BF16)
    h2_ref[...] = hi
    lo = (hf - hi.astype(F32)).astype(BF16)
    logits = _dot_nt(wrh_ref[...], hi) + _dot_nt(wrh_ref[...], lo) + _dot_nt(wrl_ref[...], hi)
    row = lax.broadcasted_iota(I32, logits.shape, 0)
    logits = jnp.where(row < N_EXPERTS, logits, NEG_BIG)
    m = jnp.max(logits, axis=0, keepdims=True)
    e = jnp.exp(logits - m)
    aff = e / jnp.sum(e, axis=0, keepdims=True)
    afft_ref[...] = aff[0:N_EXPERTS]
    if aff_ref:
        aff_ref[0][...] = aff.T


def _outproj(x, na, df, wo, g, wrh, wrl, *, tm, token_major):
    rows, d = x.shape
    out_shape = [jax.ShapeDtypeStruct((rows, d), F32), jax.ShapeDtypeStruct((rows, d), BF16),
                 jax.ShapeDtypeStruct((N_EXPERTS, rows), F32)]
    out_specs = [pl.BlockSpec((tm, d), lambda i: (i, 0)), pl.BlockSpec((tm, d), lambda i: (i, 0)),
                 pl.BlockSpec((N_EXPERTS, tm), lambda i: (0, i))]
    if token_major:
        out_shape.append(jax.ShapeDtypeStruct((rows, LANES), F32))
        out_specs.append(pl.BlockSpec((tm, LANES), lambda i: (i, 0)))
    return pl.pallas_call(
        _outproj_body,
        grid=(rows // tm,),
        in_specs=[
            pl.BlockSpec((tm, d), lambda i: (i, 0)),
            pl.BlockSpec((tm, NA_WIDTH), lambda i: (i, 0)),
            pl.BlockSpec((tm, DIFF_WIDTH), lambda i: (i, 0)),
            pl.BlockSpec((NA_WIDTH + DIFF_WIDTH, d), lambda i: (0, 0)),
            pl.BlockSpec((1, d), lambda i: (0, 0)),
            pl.BlockSpec((LANES, d), lambda i: (0, 0)),
            pl.BlockSpec((LANES, d), lambda i: (0, 0)),
        ],
        out_specs=out_specs,
        out_shape=out_shape,
        compiler_params=_cparams(("parallel",)),
        name="outproj_router",
    )(x, na, df, wo, g, wrh, wrl)


def _select_body(ar_ref, am_ref, gr_ref, gm_ref, tri_ref, tau_ref, cut_ref, offs_ref, cnt_ref, rank_ref, sel_scr,
                 *, cap, nb, idx_bits):
    br = pltpu.bitcast(ar_ref[...], I32)
    bm = pltpu.bitcast(am_ref[...], I32)
    gr = gr_ref[...]
    gm = gm_ref[...]

    def count(mr, mm):
        return (jnp.sum(jnp.where(mr, 1.0, 0.0), axis=1, keepdims=True)
                + jnp.sum(jnp.where(mm, 1.0, 0.0), axis=1, keepdims=True))

    def value_bit(i, ans):
        cand = ans | jnp.left_shift(jnp.int32(1), 30 - i)
        return jnp.where(count(br >= cand, bm >= cand) >= cap, cand, ans)

    tau = lax.fori_loop(0, 31, value_bit, jnp.zeros((N_EXPERTS, 1), I32))
    need = cap - count(br > tau, bm > tau)
    eq_r = br == tau
    eq_m = bm == tau

    def index_bit(i, ans):
        cand = ans | jnp.left_shift(jnp.int32(1), idx_bits - 1 - i)
        return jnp.where(count(eq_r & (gr < cand), eq_m & (gm < cand)) < need, cand, ans)

    cut = lax.fori_loop(0, idx_bits, index_bit, jnp.zeros((N_EXPERTS, 1), I32))
    tau_ref[...] = jnp.broadcast_to(tau, tau_ref.shape)
    cut_ref[...] = jnp.broadcast_to(cut, cut_ref.shape)

    sel_scr[...] = jnp.where((br > tau) | (eq_r & (gr <= cut)), 1.0, 0.0)
    lane = lax.broadcasted_iota(I32, (N_EXPERTS, LANES), 1)

    def block_count(j, acc):
        start = pl.multiple_of(j * TOK_BLK, TOK_BLK)
        sel = sel_scr[:, pl.ds(start, TOK_BLK)]
        rank_ref[:, pl.ds(start, TOK_BLK)] = sel * _dot(sel.astype(BF16), tri_ref[...])
        c = jnp.sum(sel, axis=1, keepdims=True)
        return jnp.where(lane == j, c, acc)

    counts = lax.fori_loop(0, nb, block_count, jnp.zeros((N_EXPERTS, LANES), F32))
    incl = counts
    shift = 1
    while shift < LANES:
        incl = incl + jnp.where(lane >= shift, pltpu.roll(incl, shift, 1), 0.0)
        shift *= 2
    offs_ref[...] = (incl - counts).astype(I32)
    cnt_ref[...] = counts.astype(I32)


def _select(aff_t, aff_t_meta, gidx, gidx_meta, tri_u, *, cap, n_total):
    e, nr = aff_t.shape
    nb = nr // TOK_BLK
    assert nb <= LANES and nr % TOK_BLK == 0
    idx_bits = max(1, (n_total - 1).bit_length())
    shp = jax.ShapeDtypeStruct((e, LANES), I32)
    return pl.pallas_call(
        functools.partial(_select_body, cap=cap, nb=nb, idx_bits=idx_bits),
        out_shape=[shp, shp, shp, shp, jax.ShapeDtypeStruct((e, nr), F32)],
        scratch_shapes=[pltpu.VMEM((e, nr), F32)],
        compiler_params=pltpu.CompilerParams(vmem_limit_bytes=VMEM_LIMIT),
        name="ec_select",
    )(aff_t, aff_t_meta, gidx, gidx_meta, tri_u)


def _gather_body(offs_ref, cnts_ref, rank_ref, h_hbm, xe_ref, hbuf, sems, *, nb, n_steps):
    eg = pl.program_id(0)

    def h_copy(s):
        slot = s % GATHER_RING
        rows = pl.ds(pl.multiple_of((s % nb) * TOK_BLK, TOK_BLK), TOK_BLK)
        return pltpu.make_async_copy(h_hbm.at[rows, :], hbuf.at[slot], sems.at[slot])

    @pl.when(eg == 0)
    def _():
        for s in range(GATHER_RING - 1):
            h_copy(s).start()

    xe_ref[...] = jnp.zeros_like(xe_ref)
    rowid = lax.broadcasted_iota(I32, (GATHER_WIN, TOK_BLK), 0)

    def block(j, carry):
        step = eg * nb + j

        @pl.when(step + GATHER_RING - 1 < n_steps)
        def _():
            h_copy(step + GATHER_RING - 1).start()

        h_copy(step).wait()
        slot = step % GATHER_RING
        for u in range(GATHER_EXPERTS):
            e = eg * GATHER_EXPERTS + u
            rank = rank_ref[u, :, pl.ds(pl.multiple_of(j * TOK_BLK, TOK_BLK), TOK_BLK)].astype(I32)
            off = offs_ref[e * nb + j]
            off_al = (off // 16) * 16
            rel = jnp.where(rank > 0, rank - 1 + (off - off_al), -1)
            nchunk = (off - off_al + cnts_ref[e * nb + j] + GATHER_WIN - 1) // GATHER_WIN

            def chunk(c, inner, u=u, rel=rel, off_al=off_al):
                onehot = jnp.where(rowid == rel - c * GATHER_WIN, 1.0, 0.0).astype(BF16)
                g = _dot(onehot, hbuf[slot]).astype(BF16)
                start = pl.multiple_of(off_al + c * GATHER_WIN, 16)
                xe_ref[u, pl.ds(start, GATHER_WIN), :] = xe_ref[u, pl.ds(start, GATHER_WIN), :] + g
                return inner

            lax.fori_loop(0, nchunk, chunk, 0)
        return carry

    lax.fori_loop(0, nb, block, 0)


def _gather(offs_flat, cnts_flat, rank3, h2, *, cap_pad):
    e = rank3.shape[0]
    nr, d = h2.shape
    nb = nr // TOK_BLK
    return pl.pallas_call(
        functools.partial(_gather_body, nb=nb, n_steps=e // GATHER_EXPERTS * nb),
        grid_spec=pltpu.PrefetchScalarGridSpec(
            num_scalar_prefetch=2,
            grid=(e // GATHER_EXPERTS,),
            in_specs=[
                pl.BlockSpec((GATHER_EXPERTS, 1, nr), lambda eg, o, c: (eg, 0, 0)),
                pl.BlockSpec(memory_space=pl.ANY),
            ],
            out_specs=pl.BlockSpec((GATHER_EXPERTS, cap_pad, d), lambda eg, o, c: (eg, 0, 0),
                                   pipeline_mode=pl.Buffered(1)),
            scratch_shapes=[pltpu.VMEM((GATHER_RING, TOK_BLK, d), BF16),
                            pltpu.SemaphoreType.DMA((GATHER_RING,))],
        ),
        out_shape=jax.ShapeDtypeStruct((e, cap_pad, d), BF16),
        compiler_params=_cparams(("arbitrary",)),
        name="ec_gather",
    )(offs_flat, cnts_flat, rank3, h2)


def _ffn_body(x_ref, wg_ref, wu_ref, wd_ref, o_ref, *, fc):
    x = x_ref[0]
    f = wg_ref.shape[2]
    acc = jnp.zeros((x.shape[0], wd_ref.shape[2]), F32)
    for c in range(f // fc):
        a = _dot(x, wg_ref[0, :, c * fc:(c + 1) * fc])
        b = _dot(x, wu_ref[0, :, c * fc:(c + 1) * fc])
        hmid = (a * jax.nn.sigmoid(a) * b).astype(BF16)
        acc = acc + _dot(hmid, wd_ref[0, c * fc:(c + 1) * fc, :])
    o_ref[0] = acc.astype(BF16)


def _ffn(xe, wg, wu, wd, *, tm):
    e, cap_pad, d = xe.shape
    f = wg.shape[2]
    return pl.pallas_call(
        functools.partial(_ffn_body, fc=min(512, f)),
        grid=(e, cap_pad // tm),
        in_specs=[
            pl.BlockSpec((1, tm, d), lambda ei, i: (ei, i, 0)),
            pl.BlockSpec((1, d, f), lambda ei, i: (ei, 0, 0)),
            pl.BlockSpec((1, d, f), lambda ei, i: (ei, 0, 0)),
            pl.BlockSpec((1, f, d), lambda ei, i: (ei, 0, 0)),
        ],
        out_specs=pl.BlockSpec((1, tm, d), lambda ei, i: (ei, i, 0)),
        out_shape=jax.ShapeDtypeStruct((e, cap_pad, d), BF16),
        compiler_params=_cparams(("parallel", "arbitrary")),
        name="ec_ffn",
    )(xe, wg, wu, wd)


def _combine_body(offs_ref, cnts_ref, aff_ref, tau_ref, cut_ref, x2_ref, tri_ref, fn_ref, *rest, nb, t, l_total):
    ywin_refs = rest[:COMBINE_EXPERTS]
    ye_hbm, o_ref, rank_scr, gate_scr, acc_scr, ybuf, sem = rest[COMBINE_EXPERTS:]
    j = pl.program_id(0)
    eg = pl.program_id(1)

    @pl.when(eg == 0)
    def _():
        aff = aff_ref[...]
        bits = pltpu.bitcast(aff, I32)
        tau = tau_ref[0:1, :]
        cut = cut_ref[0:1, :]
        row0 = j * TOK_BLK
        gidx = row0 + (row0 // t) * (l_total - t) + (l_total - t) + lax.broadcasted_iota(I32, bits.shape, 0)
        sel = (bits > tau) | ((bits == tau) & (gidx <= cut))
        sel = sel & (lax.broadcasted_iota(I32, bits.shape, 1) < N_EXPERTS)
        self32 = jnp.where(sel, 1.0, 0.0)
        incl = _dot(tri_ref[...], self32.astype(BF16))
        rank_scr[...] = jnp.where(sel, incl - 1.0, -1.0)
        gate_scr[...] = jnp.where(sel, aff, 0.0)
        acc_scr[...] = x2_ref[...]

    lane = lax.broadcasted_iota(I32, (TOK_BLK, LANES), 1)
    colid = lax.broadcasted_iota(I32, (TOK_BLK, ROW_WIN), 1)

    def placement(e):
        pick = lane == e
        rank = jnp.sum(jnp.where(pick, rank_scr[...], 0.0), axis=1, keepdims=True).astype(I32)
        gate = jnp.sum(jnp.where(pick, gate_scr[...], 0.0), axis=1, keepdims=True)
        off = offs_ref[e * nb + j]
        off_al = (off // 16) * 16
        rel = jnp.where(rank >= 0, rank + (off - off_al), -1)
        return rel, gate, off_al, (off - off_al + cnts_ref[e * nb + j] + ROW_WIN - 1) // ROW_WIN

    acc = acc_scr[...]
    for u in range(COMBINE_EXPERTS):
        rel, gate, _, _ = placement(eg * COMBINE_EXPERTS + u)
        onehot = jnp.where(colid == rel, 1.0, 0.0).astype(BF16)
        acc = acc + _dot(onehot, ywin_refs[u][...]) * gate
    acc_scr[...] = acc

    for u in range(COMBINE_EXPERTS):
        e = eg * COMBINE_EXPERTS + u
        nchunk = placement(e)[3]

        def chunk(c, carry, e=e):
            rel, gate, off_al, _ = placement(e)
            start = pl.multiple_of(off_al + c * ROW_WIN, 16)
            cp = pltpu.make_async_copy(ye_hbm.at[e, pl.ds(start, ROW_WIN), :], ybuf, sem)
            cp.start()
            cp.wait()
            oh = jnp.where(colid == rel - c * ROW_WIN, 1.0, 0.0).astype(BF16)
            acc_scr[...] += _dot(oh, ybuf[...]) * gate
            return carry

        lax.fori_loop(1, nchunk, chunk, 0)

    @pl.when(eg == pl.num_programs(1) - 1)
    def _():
        y = acc_scr[...]
        ms = jnp.mean(y * y, axis=-1, keepdims=True)
        o_ref[...] = y * lax.rsqrt(ms + RMS_EPS) * fn_ref[...]


def _combine(offs_flat, cnts_flat, aff, tau_row, cut_row, x2, tri_l, fnorm, ye, *, t, l_total):
    nr, d = x2.shape
    nb = nr // TOK_BLK
    e = ye.shape[0]

    def win_spec(u):
        def win_map(j, eg, o, c):
            ei = eg * COMBINE_EXPERTS + u
            return (ei, (o[ei * nb + j] // 16) * 16, 0)
        return pl.BlockSpec((pl.Squeezed(), pl.Element(ROW_WIN), pl.Element(d)), win_map)

    return pl.pallas_call(
        functools.partial(_combine_body, nb=nb, t=t, l_total=l_total),
        grid_spec=pltpu.PrefetchScalarGridSpec(
            num_scalar_prefetch=2,
            grid=(nb, e // COMBINE_EXPERTS),
            in_specs=[
                pl.BlockSpec((TOK_BLK, LANES), lambda j, ei, o, c: (j, 0)),
                pl.BlockSpec((8, LANES), lambda j, ei, o, c: (0, 0)),
                pl.BlockSpec((8, LANES), lambda j, ei, o, c: (0, 0)),
                pl.BlockSpec((TOK_BLK, d), lambda j, ei, o, c: (j, 0)),
                pl.BlockSpec((TOK_BLK, TOK_BLK), lambda j, ei, o, c: (0, 0)),
                pl.BlockSpec((1, d), lambda j, ei, o, c: (0, 0)),
                *[win_spec(u) for u in range(COMBINE_EXPERTS)],
                pl.BlockSpec(memory_space=pl.ANY),
            ],
            out_specs=pl.BlockSpec((TOK_BLK, d), lambda j, ei, o, c: (j, 0)),
            scratch_shapes=[
                pltpu.VMEM((TOK_BLK, LANES), F32),
                pltpu.VMEM((TOK_BLK, LANES), F32),
                pltpu.VMEM((TOK_BLK, d), F32),
                pltpu.VMEM((ROW_WIN, d), BF16),
                pltpu.SemaphoreType.DMA(()),
            ],
        ),
        out_shape=jax.ShapeDtypeStruct((nr, d), F32),
        compiler_params=_cparams(("parallel", "arbitrary")),
        name="ec_combine",
    )(offs_flat, cnts_flat, aff, tau_row, cut_row, x2, tri_l, fnorm, *([ye] * COMBINE_EXPERTS), ye)


def _rope_tables(positions):
    inv_freq = 1.0 / (ROPE_THETA ** (jnp.arange(0, HEAD_DIM, 2, dtype=F32) / HEAD_DIM))
    ang = positions.astype(F32)[:, None] * inv_freq[None, :]
    cos, sin = jnp.cos(ang), jnp.sin(ang)
    return (jnp.concatenate([cos, cos, cos, cos], axis=1),
            jnp.concatenate([-sin, -sin, sin, sin], axis=1))


def _permute_in_weights(w_in):
    l = jnp.arange(LANES)
    g = l // 32
    within = (g % 2) * HEAD_DIM + (g // 2) * 32 + l % 32
    cols = jnp.arange(PROJ_WIDTH)
    blk = cols // LANES
    permuted = blk * LANES + within[cols % LANES]
    cols = jnp.where((blk >= DQ_BLK) & (blk < DV_BLK), permuted, cols)
    return w_in[:, cols]


def _encode_group(x, shared):
    b, t, d = x.shape
    l_total = N_META + t
    n_total = b * l_total
    cap = EC_CAPACITY_FACTOR * n_total // N_EXPERTS
    rows = b * t
    xf = x.reshape(rows, d)

    cos, sin = _rope_tables(N_META + jnp.arange(t))
    proj = _inproj(xf, shared["norm_mix"], shared["w_in"], cos, sin, tm=512).reshape(b, t, PROJ_WIDTH)
    proj_meta = shared["proj_meta"]

    na = _na_attention(proj, proj_meta, shared["na_table"], shared["mb_stack"])
    lam = shared["lambda"]
    tk = min(1024, t)
    df = _diff_attention(proj, proj, proj_meta, *lam, shared["subln"], tq=min(1024, t), tk=tk, q_shared=False)
    df_meta = _diff_attention(proj_meta[None], proj, proj_meta, *lam, shared["subln"],
                              tq=N_META, tk=tk, q_shared=True)

    wr = (shared["wr_hi"], shared["wr_lo"])
    x2, h2, aff_t, aff = _outproj(xf, na.reshape(rows, NA_WIDTH), df.reshape(rows, DIFF_WIDTH), shared["w_out"],
                                  shared["norm_ffn"], *wr, tm=512, token_major=True)
    xm = jnp.tile(shared["meta_tokens"], (b, 1))
    nam = jnp.tile(shared["na_meta"], (b, 1))
    _, _, aff_t_meta = _outproj(xm, nam, df_meta.reshape(b * N_META, DIFF_WIDTH), shared["w_out"],
                                shared["norm_ffn"], *wr, tm=b * N_META, token_major=False)

    r = jnp.arange(rows, dtype=I32)
    gidx = (r + (r // t + 1) * N_META)[None, :]
    rm = jnp.arange(b * N_META, dtype=I32)
    gidx_meta = ((rm // N_META) * l_total + rm % N_META)[None, :]
    tau, cut, offs, cnts, rank_t = _select(aff_t, aff_t_meta, gidx, gidx_meta, shared["tri_u"],
                                           cap=cap, n_total=n_total)

    nb = rows // TOK_BLK
    offs_flat = offs[:, :nb].reshape(-1)
    cnts_flat = cnts[:, :nb].reshape(-1)
    cap_pad = -(-(cap + ROW_WIN) // 256) * 256
    ffn_tm = max(tm for tm in range(16, 1025, 16) if cap_pad % tm == 0)
    xe = _gather(offs_flat, cnts_flat, rank_t.reshape(N_EXPERTS, 1, rows), h2, cap_pad=cap_pad)
    ye = _ffn(xe, shared["w_gate"], shared["w_up"], shared["w_down"], tm=ffn_tm)

    pad = jnp.zeros((8, LANES - N_EXPERTS), I32)
    tau_row = jnp.concatenate([jnp.broadcast_to(tau[:, 0][None, :], (8, N_EXPERTS)), pad], axis=1)
    cut_row = jnp.concatenate([jnp.broadcast_to(cut[:, 0][None, :], (8, N_EXPERTS)), pad], axis=1)
    y = _combine(offs_flat, cnts_flat, aff, tau_row, cut_row, x2, shared["tri_l"], shared["final_norm"], ye,
                 t=t, l_total=l_total)
    return y.reshape(b, t, d)


def kernel(x_prompt, x_sample, meta_tokens, norm_mix, w_in, na_rpb, na_meta_bias, lambda_q1, lambda_k1,
           lambda_q2, lambda_k2, diff_subln, w_out, norm_ffn, w_router, w_gate, w_up, w_down, final_norm):
    d = x_prompt.shape[-1]
    wr = jnp.zeros((LANES, d), F32).at[:N_EXPERTS].set(w_router[0].T)
    wr_hi = wr.astype(BF16)
    mb = na_meta_bias[0].astype(F32)
    idx = jnp.arange(TOK_BLK)
    shared = {
        "meta_tokens": meta_tokens,
        "norm_mix": norm_mix[0][None, :],
        "w_in": _permute_in_weights(w_in[0]).astype(BF16),
        "lambda": (lambda_q1, lambda_k1, lambda_q2, lambda_k2),
        "subln": diff_subln,
        "w_out": w_out[0].astype(BF16),
        "norm_ffn": norm_ffn[0][None, :],
        "wr_hi": wr_hi,
        "wr_lo": (wr - wr_hi.astype(F32)).astype(BF16),
        "w_gate": w_gate[0].astype(BF16),
        "w_up": w_up[0].astype(BF16),
        "w_down": w_down[0].astype(BF16),
        "final_norm": final_norm[None, :],
        "mb_stack": jnp.repeat(mb, GRID_W, axis=0).reshape(NA_HEADS // 2, 2 * GRID_W, N_META),
        "tri_u": (idx[:, None] <= idx[None, :]).astype(BF16),
        "tri_l": (idx[:, None] >= idx[None, :]).astype(BF16),
    }
    cos_m, sin_m = _rope_tables(jnp.arange(N_META))
    shared["proj_meta"] = _inproj(meta_tokens, shared["norm_mix"], shared["w_in"], cos_m, sin_m, tm=N_META)
    mb_meta = jnp.repeat(mb, N_META, axis=0).reshape(NA_HEADS // 2, 2 * N_META, N_META)
    shared["na_meta"] = _na_meta(shared["proj_meta"], mb_meta)
    shared["na_table"] = _na_bias(na_rpb[0].astype(F32))
    return (_encode_group(x_prompt, shared), _encode_group(x_sample, shared))
```

```python
import functools
import math

import jax
import jax.numpy as jnp
from jax import lax
from jax.experimental import pallas as pl
from jax.experimental.pallas import tpu as pltpu

BF16 = jnp.bfloat16
F32 = jnp.float32
I32 = jnp.int32

N_META = 16
GRID_W = 64
HEAD_DIM = 64
NA_HEADS = 8
NA_WIDTH = NA_HEADS * HEAD_DIM
NA_KH = 8
NA_KW = 16
DIFF_HEADS = 4
DIFF_WIDTH = DIFF_HEADS * 2 * HEAD_DIM
PROJ_WIDTH = 3 * NA_WIDTH + 3 * DIFF_WIDTH
ROPE_THETA = 10000.0
N_EXPERTS = 16
EC_CAPACITY_FACTOR = 2
RMS_EPS = 1e-6
SUBLN_EPS = 1e-5
LAMBDA_INIT = 0.8 - 0.6 * math.exp(-0.3 * 0)
LOG2E = math.log2(math.e)

LANES = 128
NEG_BIG = -1e30
VMEM_LIMIT = 56 * 1024 * 1024

NAQ_BLK, NAK_BLK, NAV_BLK = 0, 4, 8
DQ_BLK, DK_BLK, DV_BLK = 12, 16, 20

ROW_WIN = 256
GATHER_WIN = 192
GATHER_RING = 3
GATHER_EXPERTS = 2
TOK_BLK = 1024
COMBINE_EXPERTS = 4


def _cparams(sem, vmem=VMEM_LIMIT):
    return pltpu.CompilerParams(dimension_semantics=sem, vmem_limit_bytes=vmem)


def _dot(a, b):
    return jnp.dot(a, b, preferred_element_type=F32)


def _dot_nt(a, b):
    return lax.dot_general(a, b, (((1,), (1,)), ((), ())), preferred_element_type=F32)


def _inproj_body(x_ref, g_ref, w_ref, cos_ref, sin_ref, o_ref, *, tn):
    x = x_ref[...]
    ms = jnp.mean(x * x, axis=-1, keepdims=True)
    h = (x * lax.rsqrt(ms + RMS_EPS) * g_ref[...]).astype(BF16)
    cos = cos_ref[...]
    sin = sin_ref[...]
    for c in range(PROJ_WIDTH // tn):
        lo = c * tn
        acc = _dot(h, w_ref[:, lo:lo + tn])
        for s in range(tn // LANES):
            blk = (lo + s * LANES) // LANES
            a = acc[:, s * LANES:(s + 1) * LANES]
            if DQ_BLK <= blk < DV_BLK:
                a = a * cos + pltpu.roll(a, 64, 1) * sin
            if blk < NAK_BLK:
                a = a * (HEAD_DIM ** -0.5)
            if DQ_BLK <= blk < DK_BLK:
                a = a * (HEAD_DIM ** -0.5 * LOG2E)
            o_ref[:, lo + s * LANES:lo + (s + 1) * LANES] = a.astype(BF16)


def _inproj(x, g, w, cos, sin, *, tm):
    rows, d = x.shape
    nt = cos.shape[0] // tm
    return pl.pallas_call(
        functools.partial(_inproj_body, tn=512),
        grid=(rows // tm,),
        in_specs=[
            pl.BlockSpec((tm, d), lambda i: (i, 0)),
            pl.BlockSpec((1, d), lambda i: (0, 0)),
            pl.BlockSpec((d, PROJ_WIDTH), lambda i: (0, 0)),
            pl.BlockSpec((tm, LANES), lambda i: (i % nt, 0)),
            pl.BlockSpec((tm, LANES), lambda i: (i % nt, 0)),
        ],
        out_specs=pl.BlockSpec((tm, PROJ_WIDTH), lambda i: (i, 0)),
        out_shape=jax.ShapeDtypeStruct((rows, PROJ_WIDTH), BF16),
        compiler_params=_cparams(("parallel",)),
        name="inproj",
    )(x, g, w, cos, sin)


def _na_bias_body(rpb_ref, o_ref):
    h = pl.program_id(0)
    c = lax.broadcasted_iota(I32, (GRID_W, GRID_W), 0)
    kc = lax.broadcasted_iota(I32, (GRID_W, GRID_W), 1)
    cs = jnp.clip(c - NA_KW // 2, 0, GRID_W - NA_KW)
    valid = (kc >= cs) & (kc < cs + NA_KW)
    jm = kc - c + NA_KW - 1
    n_dr = 2 * NA_KH - 1
    n_j = 2 * NA_KW - 1
    tiles = []
    for dr in range(n_dr):
        base = (h * n_dr + dr) * n_j

        def body(j, acc, base=base):
            return jnp.where(jm == j, rpb_ref[base + j], acc)

        t = lax.fori_loop(0, n_j, body, jnp.zeros((GRID_W, GRID_W), F32))
        tiles.append(jnp.where(valid, t, NEG_BIG))
    for d in range(NA_KH):
        o_ref[0, d] = jnp.concatenate([tiles[kr - d + NA_KH - 1] for kr in range(NA_KH)], axis=1)


def _na_bias(rpb):
    h = rpb.shape[0]
    return pl.pallas_call(
        _na_bias_body,
        grid_spec=pltpu.PrefetchScalarGridSpec(
            num_scalar_prefetch=1,
            grid=(h,),
            in_specs=[],
            out_specs=pl.BlockSpec((1, NA_KH, GRID_W, NA_KH * GRID_W), lambda i, r: (i, 0, 0, 0)),
        ),
        out_shape=jax.ShapeDtypeStruct((h, NA_KH, GRID_W, NA_KH * GRID_W), F32),
        compiler_params=_cparams(("arbitrary",)),
        name="na_bias",
    )(rpb.reshape(-1))


def _split_heads(q):
    lane = lax.broadcasted_iota(I32, q.shape, 1)
    zero = jnp.zeros_like(q)
    return jnp.concatenate([jnp.where(lane < HEAD_DIM, q, zero), jnp.where(lane < HEAD_DIM, zero, q)], axis=0)


def _merge_heads(o):
    n = o.shape[0] // 2
    lane = lax.broadcasted_iota(I32, (n, LANES), 1)
    return jnp.where(lane < HEAD_DIM, o[:n], o[n:])


def _na_body(q_ref, kp_ref, kc_ref, kn_ref, vp_ref, vc_ref, vn_ref, km_ref, vm_ref, tab_ref, mb_ref,
             o_ref, kbuf, vbuf, qq_scr, s_scr, p_scr, *, rows):
    rb = pl.program_id(2)
    blk = NA_KH * GRID_W
    kbuf[0:blk] = kp_ref[0]
    kbuf[blk:2 * blk] = kc_ref[0]
    kbuf[2 * blk:3 * blk] = kn_ref[0]
    vbuf[0:blk] = vp_ref[0]
    vbuf[blk:2 * blk] = vc_ref[0]
    vbuf[2 * blk:3 * blk] = vn_ref[0]
    two = 2 * GRID_W
    starts = []
    for i in range(NA_KH):
        r = rb * NA_KH + i
        rs = jnp.clip(r - NA_KH // 2, 0, rows - NA_KH)
        d = r - rs
        start = pl.multiple_of((rs - (rb - 1) * NA_KH) * GRID_W, GRID_W)
        starts.append(start)
        qq = _split_heads(q_ref[0, i * GRID_W:(i + 1) * GRID_W, :])
        qq_scr[i * two:(i + 1) * two, :] = qq
        bias = jnp.concatenate([tab_ref[0, d], tab_ref[1, d]], axis=0)
        s_scr[i * two:(i + 1) * two, :] = _dot_nt(qq, kbuf[pl.ds(start, blk), :]) + bias
    s = s_scr[...]
    sm = (_dot_nt(qq_scr[...], km_ref[...]).reshape(NA_KH, two, N_META) + mb_ref[0][None]).reshape(NA_KH * two, N_META)
    m = jnp.maximum(jnp.max(s, axis=1, keepdims=True), jnp.max(sm, axis=1, keepdims=True))
    p = jnp.exp(s - m)
    pm = jnp.exp(sm - m)
    l = jnp.sum(p, axis=1, keepdims=True) + jnp.sum(pm, axis=1, keepdims=True)
    p_scr[...] = p.astype(BF16)
    om = _dot(pm.astype(BF16), vm_ref[...])
    inv_l = 1.0 / l
    for i in range(NA_KH):
        rsl = slice(i * two, (i + 1) * two)
        o = (_dot(p_scr[rsl, :], vbuf[pl.ds(starts[i], blk), :]) + om[rsl]) * inv_l[rsl]
        o_ref[0, i * GRID_W:(i + 1) * GRID_W, :] = _merge_heads(o).astype(BF16)


def _na_attention(proj, proj_meta, table, mb_stack):
    b, t, _ = proj.shape
    rows = t // GRID_W
    nrb = rows // NA_KH
    blk = NA_KH * GRID_W
    hp = NA_HEADS // 2

    def kv_spec(col, shift):
        return pl.BlockSpec((1, blk, LANES),
                            lambda h, bi, rb: (bi, jnp.clip(rb + shift, 0, nrb - 1), col + h))

    return pl.pallas_call(
        functools.partial(_na_body, rows=rows),
        grid=(hp, b, nrb),
        in_specs=[
            pl.BlockSpec((1, blk, LANES), lambda h, bi, rb: (bi, rb, NAQ_BLK + h)),
            kv_spec(NAK_BLK, -1), kv_spec(NAK_BLK, 0), kv_spec(NAK_BLK, 1),
            kv_spec(NAV_BLK, -1), kv_spec(NAV_BLK, 0), kv_spec(NAV_BLK, 1),
            pl.BlockSpec((N_META, LANES), lambda h, bi, rb: (0, NAK_BLK + h)),
            pl.BlockSpec((N_META, LANES), lambda h, bi, rb: (0, NAV_BLK + h)),
            pl.BlockSpec((2, NA_KH, GRID_W, blk), lambda h, bi, rb: (h, 0, 0, 0)),
            pl.BlockSpec((1, LANES, N_META), lambda h, bi, rb: (h, 0, 0)),
        ],
        out_specs=pl.BlockSpec((1, blk, LANES), lambda h, bi, rb: (bi, rb, h)),
        out_shape=jax.ShapeDtypeStruct((b, t, NA_WIDTH), BF16),
        scratch_shapes=[pltpu.VMEM((3 * blk, LANES), BF16), pltpu.VMEM((3 * blk, LANES), BF16),
                        pltpu.VMEM((2 * blk, LANES), BF16), pltpu.VMEM((2 * blk, blk), F32),
                        pltpu.VMEM((2 * blk, blk), BF16)],
        compiler_params=_cparams(("arbitrary", "arbitrary", "arbitrary")),
        name="na_attention",
    )(proj, proj, proj, proj, proj, proj, proj, proj_meta, proj_meta, table, mb_stack)


def _na_meta_body(p_ref, mb_ref, o_ref):
    for h in range(NA_HEADS // 2):
        qq = _split_heads(p_ref[:, (NAQ_BLK + h) * LANES:(NAQ_BLK + h + 1) * LANES])
        km = p_ref[:, (NAK_BLK + h) * LANES:(NAK_BLK + h + 1) * LANES]
        vm = p_ref[:, (NAV_BLK + h) * LANES:(NAV_BLK + h + 1) * LANES]
        s = _dot_nt(qq, km) + mb_ref[h, 0:2 * N_META, :]
        m = jnp.max(s, axis=1, keepdims=True)
        p = jnp.exp(s - m)
        l = jnp.sum(p, axis=1, keepdims=True)
        o = _dot(p.astype(BF16), vm) / l
        o_ref[:, h * LANES:(h + 1) * LANES] = _merge_heads(o).astype(BF16)


def _na_meta(proj_meta, mb_meta):
    return pl.pallas_call(
        _na_meta_body,
        out_shape=jax.ShapeDtypeStruct((N_META, NA_WIDTH), BF16),
        name="na_meta",
    )(proj_meta, mb_meta)


def _split_maps(q):
    lane = lax.broadcasted_iota(I32, q.shape, 1)
    is0 = ((lane >> 5) & 1) == 0
    zero = jnp.zeros_like(q)
    return jnp.concatenate([jnp.where(is0, q, zero), jnp.where(is0, zero, q)], axis=0)


def _online_softmax(s_ref, p_ref, a_ref, m_scr, rows):
    s = s_ref[rows, :]
    m_prev = m_scr[rows]
    m_new = jnp.maximum(m_prev, jnp.max(s, axis=1, keepdims=True))
    a_ref[rows] = jnp.exp2(m_prev - m_new)
    m_scr[rows] = m_new
    p_ref[rows, :] = jnp.exp2((s - m_new).astype(BF16))


def _diff_body(q_ref, k_ref, v_ref, km_ref, vm_ref, lq1_ref, lk1_ref, lq2_ref, lk2_ref, sub_ref,
               o_ref, qq_scr, vx_scr, s0, s1, p0, p1, a0, a1, m_scr, acc_scr, *, tq, tk, rc, nk):
    def kv_rows(ref, ki):
        return ref[0, pl.ds(pl.multiple_of(ki * tk, tk), tk), :]

    qq = _split_maps(q_ref[0])
    qq_scr[...] = qq
    s = _dot_nt(qq, km_ref[...])
    m = jnp.max(s, axis=1, keepdims=True)
    m_scr[...] = m
    vx_scr[:, LANES:] = jnp.ones((vx_scr.shape[0], LANES), BF16)
    vx_scr[0:N_META, 0:LANES] = vm_ref[...]
    acc_scr[...] = _dot(jnp.exp2((s - m).astype(BF16)), vx_scr[0:N_META, :])
    s0[...] = _dot_nt(qq, k_ref[0, 0:tk, :])
    p1[...] = jnp.zeros_like(p1)
    a1[...] = jnp.ones_like(a1)

    def step(ki, s_cur, s_nxt, p_cur, p_prv, a_cur, a_prv, *, has_next):
        if has_next:
            kn = kv_rows(k_ref, ki + 1)
        vx_scr[:, 0:LANES] = kv_rows(v_ref, jnp.maximum(ki - 1, 0))
        for g in range(2 * tq // rc):
            rows = slice(g * rc, (g + 1) * rc)
            if has_next:
                s_nxt[rows, :] = _dot_nt(qq_scr[rows, :], kn)
            acc_scr[rows, :] = a_prv[rows] * acc_scr[rows, :] + _dot(p_prv[rows, :], vx_scr[...])
            _online_softmax(s_cur, p_cur, a_cur, m_scr, rows)

    even = (s0, s1, p0, p1, a0, a1)
    odd = (s1, s0, p1, p0, a1, a0)

    def block(ki, carry):
        @pl.when(ki % 2 == 0)
        def _():
            step(ki, *even, has_next=True)

        @pl.when(ki % 2 == 1)
        def _():
            step(ki, *odd, has_next=True)

        return carry

    lax.fori_loop(0, nk - 1, block, 0)

    @pl.when(pl.program_id(2) >= 0)
    def _():
        step(nk - 1, *(even if (nk - 1) % 2 == 0 else odd), has_next=False)

    p_last, a_last = (p0, a0) if (nk - 1) % 2 == 0 else (p1, a1)
    vx_scr[:, 0:LANES] = v_ref[0, (nk - 1) * tk:nk * tk, :]
    acc = a_last[...] * acc_scr[...] + _dot(p_last[...], vx_scr[...])
    lam = (jnp.exp(jnp.sum(lq1_ref[...] * lk1_ref[...], axis=1, keepdims=True))
           - jnp.exp(jnp.sum(lq2_ref[...] * lk2_ref[...], axis=1, keepdims=True)) + LAMBDA_INIT)
    o_all = acc[:, 0:LANES] / acc[:, LANES:]
    o = o_all[:tq] - lam * o_all[tq:]
    ms = jnp.mean(o * o, axis=-1, keepdims=True)
    o = o * lax.rsqrt(ms + SUBLN_EPS) * sub_ref[...] * (1.0 - LAMBDA_INIT)
    o_ref[0] = o.astype(BF16)


def _diff_attention(q_src, proj, proj_meta, lq1, lk1, lq2, lk2, subln, *, tq, tk, q_shared):
    b, t, _ = proj.shape
    tq_total = q_src.shape[1]
    nq = tq_total // tq
    nk = t // tk
    if q_shared:
        q_map = lambda bi, h, qi: (0, qi, DQ_BLK + h)
    else:
        q_map = lambda bi, h, qi: (bi, qi, DQ_BLK + h)
    vec = lambda n: pl.BlockSpec((1, n), lambda bi, h, qi: (0, 0))
    return pl.pallas_call(
        functools.partial(_diff_body, tq=tq, tk=tk, rc=min(512, 2 * tq), nk=nk),
        grid=(b, DIFF_HEADS, nq),
        in_specs=[
            pl.BlockSpec((1, tq, LANES), q_map),
            pl.BlockSpec((1, t, LANES), lambda bi, h, qi: (bi, 0, DK_BLK + h)),
            pl.BlockSpec((1, t, LANES), lambda bi, h, qi: (bi, 0, DV_BLK + h)),
            pl.BlockSpec((N_META, LANES), lambda bi, h, qi: (0, DK_BLK + h)),
            pl.BlockSpec((N_META, LANES), lambda bi, h, qi: (0, DV_BLK + h)),
            vec(HEAD_DIM), vec(HEAD_DIM), vec(HEAD_DIM), vec(HEAD_DIM), vec(2 * HEAD_DIM),
        ],
        out_specs=pl.BlockSpec((1, tq, LANES), lambda bi, h, qi: (bi, qi, h)),
        out_shape=jax.ShapeDtypeStruct((b, tq_total, DIFF_WIDTH), BF16),
        scratch_shapes=[
            pltpu.VMEM((2 * tq, LANES), BF16),
            pltpu.VMEM((tk, 2 * LANES), BF16),
            pltpu.VMEM((2 * tq, tk), F32), pltpu.VMEM((2 * tq, tk), F32),
            pltpu.VMEM((2 * tq, tk), BF16), pltpu.VMEM((2 * tq, tk), BF16),
            pltpu.VMEM((2 * tq, 1), F32), pltpu.VMEM((2 * tq, 1), F32),
            pltpu.VMEM((2 * tq, 1), F32),
            pltpu.VMEM((2 * tq, 2 * LANES), F32),
        ],
        compiler_params=_cparams(("parallel", "parallel", "parallel")),
        name="diff_attention",
    )(q_src, proj, proj, proj_meta, proj_meta, lq1, lk1, lq2, lk2, subln)


def _outproj_body(x_ref, na_ref, df_ref, wo_ref, g_ref, wrh_ref, wrl_ref, x2_ref, h2_ref, afft_ref, *aff_ref):
    x2 = x_ref[...] + _dot(na_ref[...], wo_ref[0:NA_WIDTH, :]) + _dot(df_ref[...], wo_ref[NA_WIDTH:, :])
    x2_ref[...] = x2
    ms = jnp.mean(x2 * x2, axis=-1, keepdims=True)
    hf = x2 * lax.rsqrt(ms + RMS_EPS) * g_ref[...]
    hi = hf.astype(BF16)
    h2_ref[...] = hi
    lo = (hf - hi.astype(F32)).astype(BF16)
    logits = _dot_nt(wrh_ref[...], hi) + _dot_nt(wrh_ref[...], lo) + _dot_nt(wrl_ref[...], hi)
    row = lax.broadcasted_iota(I32, logits.shape, 0)
    logits = jnp.where(row < N_EXPERTS, logits, NEG_BIG)
    m = jnp.max(logits, axis=0, keepdims=True)
    e = jnp.exp(logits - m)
    aff = e / jnp.sum(e, axis=0, keepdims=True)
    afft_ref[...] = aff[0:N_EXPERTS]
    if aff_ref:
        aff_ref[0][...] = aff.T


def _outproj(x, na, df, wo, g, wrh, wrl, *, tm, token_major):
    rows, d = x.shape
    out_shape = [jax.ShapeDtypeStruct((rows, d), F32), jax.ShapeDtypeStruct((rows, d), BF16),
                 jax.ShapeDtypeStruct((N_EXPERTS, rows), F32)]
    out_specs = [pl.BlockSpec((tm, d), lambda i: (i, 0)), pl.BlockSpec((tm, d), lambda i: (i, 0)),
                 pl.BlockSpec((N_EXPERTS, tm), lambda i: (0, i))]
    if token_major:
        out_shape.append(jax.ShapeDtypeStruct((rows, LANES), F32))
        out_specs.append(pl.BlockSpec((tm, LANES), lambda i: (i, 0)))
    return pl.pallas_call(
        _outproj_body,
        grid=(rows // tm,),
        in_specs=[
            pl.BlockSpec((tm, d), lambda i: (i, 0)),
            pl.BlockSpec((tm, NA_WIDTH), lambda i: (i, 0)),
            pl.BlockSpec((tm, DIFF_WIDTH), lambda i: (i, 0)),
            pl.BlockSpec((NA_WIDTH + DIFF_WIDTH, d), lambda i: (0, 0)),
            pl.BlockSpec((1, d), lambda i: (0, 0)),
            pl.BlockSpec((LANES, d), lambda i: (0, 0)),
            pl.BlockSpec((LANES, d), lambda i: (0, 0)),
        ],
        out_specs=out_specs,
        out_shape=out_shape,
        compiler_params=_cparams(("parallel",)),
        name="outproj_router",
    )(x, na, df, wo, g, wrh, wrl)


def _select_body(ar_ref, am_ref, gr_ref, gm_ref, tri_ref, tau_ref, cut_ref, offs_ref, cnt_ref, rank_ref, sel_scr,
                 *, cap, nb, idx_bits):
    br = pltpu.bitcast(ar_ref[...], I32)
    bm = pltpu.bitcast(am_ref[...], I32)
    gr = gr_ref[...]
    gm = gm_ref[...]

    def count(mr, mm):
        return (jnp.sum(jnp.where(mr, 1.0, 0.0), axis=1, keepdims=True)
                + jnp.sum(jnp.where(mm, 1.0, 0.0), axis=1, keepdims=True))

    def value_bit(i, ans):
        cand = ans | jnp.left_shift(jnp.int32(1), 30 - i)
        return jnp.where(count(br >= cand, bm >= cand) >= cap, cand, ans)

    tau = lax.fori_loop(0, 31, value_bit, jnp.zeros((N_EXPERTS, 1), I32))
    need = cap - count(br > tau, bm > tau)
    eq_r = br == tau
    eq_m = bm == tau

    def index_bit(i, ans):
        cand = ans | jnp.left_shift(jnp.int32(1), idx_bits - 1 - i)
        return jnp.where(count(eq_r & (gr < cand), eq_m & (gm < cand)) < need, cand, ans)

    cut = lax.fori_loop(0, idx_bits, index_bit, jnp.zeros((N_EXPERTS, 1), I32))
    tau_ref[...] = jnp.broadcast_to(tau, tau_ref.shape)
    cut_ref[...] = jnp.broadcast_to(cut, cut_ref.shape)

    sel_scr[...] = jnp.where((br > tau) | (eq_r & (gr <= cut)), 1.0, 0.0)
    lane = lax.broadcasted_iota(I32, (N_EXPERTS, LANES), 1)

    def block_count(j, acc):
        start = pl.multiple_of(j * TOK_BLK, TOK_BLK)
        sel = sel_scr[:, pl.ds(start, TOK_BLK)]
        rank_ref[:, pl.ds(start, TOK_BLK)] = sel * _dot(sel.astype(BF16), tri_ref[...])
        c = jnp.sum(sel, axis=1, keepdims=True)
        return jnp.where(lane == j, c, acc)

    counts = lax.fori_loop(0, nb, block_count, jnp.zeros((N_EXPERTS, LANES), F32))
    incl = counts
    shift = 1
    while shift < LANES:
        incl = incl + jnp.where(lane >= shift, pltpu.roll(incl, shift, 1), 0.0)
        shift *= 2
    offs_ref[...] = (incl - counts).astype(I32)
    cnt_ref[...] = counts.astype(I32)


def _select(aff_t, aff_t_meta, gidx, gidx_meta, tri_u, *, cap, n_total):
    e, nr = aff_t.shape
    nb = nr // TOK_BLK
    assert nb <= LANES and nr % TOK_BLK == 0
    idx_bits = max(1, (n_total - 1).bit_length())
    shp = jax.ShapeDtypeStruct((e, LANES), I32)
    return pl.pallas_call(
        functools.partial(_select_body, cap=cap, nb=nb, idx_bits=idx_bits),
        out_shape=[shp, shp, shp, shp, jax.ShapeDtypeStruct((e, nr), F32)],
        scratch_shapes=[pltpu.VMEM((e, nr), F32)],
        compiler_params=pltpu.CompilerParams(vmem_limit_bytes=VMEM_LIMIT),
        name="ec_select",
    )(aff_t, aff_t_meta, gidx, gidx_meta, tri_u)


def _gather_body(offs_ref, cnts_ref, rank_ref, h_hbm, xe_ref, hbuf, sems, *, nb, n_steps):
    eg = pl.program_id(0)

    def h_copy(s):
        slot = s % GATHER_RING
        rows = pl.ds(pl.multiple_of((s % nb) * TOK_BLK, TOK_BLK), TOK_BLK)
        return pltpu.make_async_copy(h_hbm.at[rows, :], hbuf.at[slot], sems.at[slot])

    @pl.when(eg == 0)
    def _():
        for s in range(GATHER_RING - 1):
            h_copy(s).start()

    xe_ref[...] = jnp.zeros_like(xe_ref)
    rowid = lax.broadcasted_iota(I32, (GATHER_WIN, TOK_BLK), 0)

    def block(j, carry):
        step = eg * nb + j

        @pl.when(step + GATHER_RING - 1 < n_steps)
        def _():
            h_copy(step + GATHER_RING - 1).start()

        h_copy(step).wait()
        slot = step % GATHER_RING
        for u in range(GATHER_EXPERTS):
            e = eg * GATHER_EXPERTS + u
            rank = rank_ref[u, :, pl.ds(pl.multiple_of(j * TOK_BLK, TOK_BLK), TOK_BLK)].astype(I32)
            off = offs_ref[e * nb + j]
            off_al = (off // 16) * 16
            rel = jnp.where(rank > 0, rank - 1 + (off - off_al), -1)
            nchunk = (off - off_al + cnts_ref[e * nb + j] + GATHER_WIN - 1) // GATHER_WIN

            def chunk(c, inner, u=u, rel=rel, off_al=off_al):
                onehot = jnp.where(rowid == rel - c * GATHER_WIN, 1.0, 0.0).astype(BF16)
                g = _dot(onehot, hbuf[slot]).astype(BF16)
                start = pl.multiple_of(off_al + c * GATHER_WIN, 16)
                xe_ref[u, pl.ds(start, GATHER_WIN), :] = xe_ref[u, pl.ds(start, GATHER_WIN), :] + g
                return inner

            lax.fori_loop(0, nchunk, chunk, 0)
        return carry

    lax.fori_loop(0, nb, block, 0)


def _gather(offs_flat, cnts_flat, rank3, h2, *, cap_pad):
    e = rank3.shape[0]
    nr, d = h2.shape
    nb = nr // TOK_BLK
    return pl.pallas_call(
        functools.partial(_gather_body, nb=nb, n_steps=e // GATHER_EXPERTS * nb),
        grid_spec=pltpu.PrefetchScalarGridSpec(
            num_scalar_prefetch=2,
            grid=(e // GATHER_EXPERTS,),
            in_specs=[
                pl.BlockSpec((GATHER_EXPERTS, 1, nr), lambda eg, o, c: (eg, 0, 0)),
                pl.BlockSpec(memory_space=pl.ANY),
            ],
            out_specs=pl.BlockSpec((GATHER_EXPERTS, cap_pad, d), lambda eg, o, c: (eg, 0, 0),
                                   pipeline_mode=pl.Buffered(1)),
            scratch_shapes=[pltpu.VMEM((GATHER_RING, TOK_BLK, d), BF16),
                            pltpu.SemaphoreType.DMA((GATHER_RING,))],
        ),
        out_shape=jax.ShapeDtypeStruct((e, cap_pad, d), BF16),
        compiler_params=_cparams(("arbitrary",)),
        name="ec_gather",
    )(offs_flat, cnts_flat, rank3, h2)


def _ffn_body(x_ref, wg_ref, wu_ref, wd_ref, o_ref, *, fc):
    x = x_ref[0]
    f = wg_ref.shape[2]
    acc = jnp.zeros((x.shape[0], wd_ref.shape[2]), F32)
    for c in range(f // fc):
        a = _dot(x, wg_ref[0, :, c * fc:(c + 1) * fc])
        b = _dot(x, wu_ref[0, :, c * fc:(c + 1) * fc])
        hmid = (a * jax.nn.sigmoid(a) * b).astype(BF16)
        acc = acc + _dot(hmid, wd_ref[0, c * fc:(c + 1) * fc, :])
    o_ref[0] = acc.astype(BF16)


def _ffn(xe, wg, wu, wd, *, tm):
    e, cap_pad, d = xe.shape
    f = wg.shape[2]
    return pl.pallas_call(
        functools.partial(_ffn_body, fc=min(512, f)),
        grid=(e, cap_pad // tm),
        in_specs=[
            pl.BlockSpec((1, tm, d), lambda ei, i: (ei, i, 0)),
            pl.BlockSpec((1, d, f), lambda ei, i: (ei, 0, 0)),
            pl.BlockSpec((1, d, f), lambda ei, i: (ei, 0, 0)),
            pl.BlockSpec((1, f, d), lambda ei, i: (ei, 0, 0)),
        ],
        out_specs=pl.BlockSpec((1, tm, d), lambda ei, i: (ei, i, 0)),
        out_shape=jax.ShapeDtypeStruct((e, cap_pad, d), BF16),
        compiler_params=_cparams(("parallel", "arbitrary")),
        name="ec_ffn",
    )(xe, wg, wu, wd)


def _combine_body(offs_ref, cnts_ref, aff_ref, tau_ref, cut_ref, x2_ref, tri_ref, fn_ref, *rest, nb, t, l_total):
    ywin_refs = rest[:COMBINE_EXPERTS]
    ye_hbm, o_ref, rank_scr, gate_scr, acc_scr, ybuf, sem = rest[COMBINE_EXPERTS:]
    j = pl.program_id(0)
    eg = pl.program_id(1)

    @pl.when(eg == 0)
    def _():
        aff = aff_ref[...]
        bits = pltpu.bitcast(aff, I32)
        tau = tau_ref[0:1, :]
        cut = cut_ref[0:1, :]
        row0 = j * TOK_BLK
        gidx = row0 + (row0 // t) * (l_total - t) + (l_total - t) + lax.broadcasted_iota(I32, bits.shape, 0)
        sel = (bits > tau) | ((bits == tau) & (gidx <= cut))
        sel = sel & (lax.broadcasted_iota(I32, bits.shape, 1) < N_EXPERTS)
        self32 = jnp.where(sel, 1.0, 0.0)
        incl = _dot(tri_ref[...], self32.astype(BF16))
        rank_scr[...] = jnp.where(sel, incl - 1.0, -1.0)
        gate_scr[...] = jnp.where(sel, aff, 0.0)
        acc_scr[...] = x2_ref[...]

    lane = lax.broadcasted_iota(I32, (TOK_BLK, LANES), 1)
    colid = lax.broadcasted_iota(I32, (TOK_BLK, ROW_WIN), 1)

    def placement(e):
        pick = lane == e
        rank = jnp.sum(jnp.where(pick, rank_scr[...], 0.0), axis=1, keepdims=True).astype(I32)
        gate = jnp.sum(jnp.where(pick, gate_scr[...], 0.0), axis=1, keepdims=True)
        off = offs_ref[e * nb + j]
        off_al = (off // 16) * 16
        rel = jnp.where(rank >= 0, rank + (off - off_al), -1)
        return rel, gate, off_al, (off - off_al + cnts_ref[e * nb + j] + ROW_WIN - 1) // ROW_WIN

    acc = acc_scr[...]
    for u in range(COMBINE_EXPERTS):
        rel, gate, _, _ = placement(eg * COMBINE_EXPERTS + u)
        onehot = jnp.where(colid == rel, 1.0, 0.0).astype(BF16)
        acc = acc + _dot(onehot, ywin_refs[u][...]) * gate
    acc_scr[...] = acc

    for u in range(COMBINE_EXPERTS):
        e = eg * COMBINE_EXPERTS + u
        nchunk = placement(e)[3]

        def chunk(c, carry, e=e):
            rel, gate, off_al, _ = placement(e)
            start = pl.multiple_of(off_al + c * ROW_WIN, 16)
            cp = pltpu.make_async_copy(ye_hbm.at[e, pl.ds(start, ROW_WIN), :], ybuf, sem)
            cp.start()
            cp.wait()
            oh = jnp.where(colid == rel - c * ROW_WIN, 1.0, 0.0).astype(BF16)
            acc_scr[...] += _dot(oh, ybuf[...]) * gate
            return carry

        lax.fori_loop(1, nchunk, chunk, 0)

    @pl.when(eg == pl.num_programs(1) - 1)
    def _():
        y = acc_scr[...]
        ms = jnp.mean(y * y, axis=-1, keepdims=True)
        o_ref[...] = y * lax.rsqrt(ms + RMS_EPS) * fn_ref[...]


def _combine(offs_flat, cnts_flat, aff, tau_row, cut_row, x2, tri_l, fnorm, ye, *, t, l_total):
    nr, d = x2.shape
    nb = nr // TOK_BLK
    e = ye.shape[0]

    def win_spec(u):
        def win_map(j, eg, o, c):
            ei = eg * COMBINE_EXPERTS + u
            return (ei, (o[ei * nb + j] // 16) * 16, 0)
        return pl.BlockSpec((pl.Squeezed(), pl.Element(ROW_WIN), pl.Element(d)), win_map)

    return pl.pallas_call(
        functools.partial(_combine_body, nb=nb, t=t, l_total=l_total),
        grid_spec=pltpu.PrefetchScalarGridSpec(
            num_scalar_prefetch=2,
            grid=(nb, e // COMBINE_EXPERTS),
            in_specs=[
                pl.BlockSpec((TOK_BLK, LANES), lambda j, ei, o, c: (j, 0)),
                pl.BlockSpec((8, LANES), lambda j, ei, o, c: (0, 0)),
                pl.BlockSpec((8, LANES), lambda j, ei, o, c: (0, 0)),
                pl.BlockSpec((TOK_BLK, d), lambda j, ei, o, c: (j, 0)),
                pl.BlockSpec((TOK_BLK, TOK_BLK), lambda j, ei, o, c: (0, 0)),
                pl.BlockSpec((1, d), lambda j, ei, o, c: (0, 0)),
                *[win_spec(u) for u in range(COMBINE_EXPERTS)],
                pl.BlockSpec(memory_space=pl.ANY),
            ],
            out_specs=pl.BlockSpec((TOK_BLK, d), lambda j, ei, o, c: (j, 0)),
            scratch_shapes=[
                pltpu.VMEM((TOK_BLK, LANES), F32),
                pltpu.VMEM((TOK_BLK, LANES), F32),
                pltpu.VMEM((TOK_BLK, d), F32),
                pltpu.VMEM((ROW_WIN, d), BF16),
                pltpu.SemaphoreType.DMA(()),
            ],
        ),
        out_shape=jax.ShapeDtypeStruct((nr, d), F32),
        compiler_params=_cparams(("parallel", "arbitrary")),
        name="ec_combine",
    )(offs_flat, cnts_flat, aff, tau_row, cut_row, x2, tri_l, fnorm, *([ye] * COMBINE_EXPERTS), ye)


def _rope_tables(positions):
    inv_freq = 1.0 / (ROPE_THETA ** (jnp.arange(0, HEAD_DIM, 2, dtype=F32) / HEAD_DIM))
    ang = positions.astype(F32)[:, None] * inv_freq[None, :]
    cos, sin = jnp.cos(ang), jnp.sin(ang)
    return (jnp.concatenate([cos, cos, cos, cos], axis=1),
            jnp.concatenate([-sin, -sin, sin, sin], axis=1))


def _permute_in_weights(w_in):
    l = jnp.arange(LANES)
    g = l // 32
    within = (g % 2) * HEAD_DIM + (g // 2) * 32 + l % 32
    cols = jnp.arange(PROJ_WIDTH)
    blk = cols // LANES
    permuted = blk * LANES + within[cols % LANES]
    cols = jnp.where((blk >= DQ_BLK) & (blk < DV_BLK), permuted, cols)
    return w_in[:, cols]


def _encode_group(x, shared):
    b, t, d = x.shape
    l_total = N_META + t
    n_total = b * l_total
    cap = EC_CAPACITY_FACTOR * n_total // N_EXPERTS
    rows = b * t
    xf = x.reshape(rows, d)

    cos, sin = _rope_tables(N_META + jnp.arange(t))
    proj = _inproj(xf, shared["norm_mix"], shared["w_in"], cos, sin, tm=512).reshape(b, t, PROJ_WIDTH)
    proj_meta = shared["proj_meta"]

    na = _na_attention(proj, proj_meta, shared["na_table"], shared["mb_stack"])
    lam = shared["lambda"]
    tk = min(1024, t)
    df = _diff_attention(proj, proj, proj_meta, *lam, shared["subln"], tq=min(1024, t), tk=tk, q_shared=False)
    df_meta = _diff_attention(proj_meta[None], proj, proj_meta, *lam, shared["subln"],
                              tq=N_META, tk=tk, q_shared=True)

    wr = (shared["wr_hi"], shared["wr_lo"])
    x2, h2, aff_t, aff = _outproj(xf, na.reshape(rows, NA_WIDTH), df.reshape(rows, DIFF_WIDTH), shared["w_out"],
                                  shared["norm_ffn"], *wr, tm=512, token_major=True)
    xm = jnp.tile(shared["meta_tokens"], (b, 1))
    nam = jnp.tile(shared["na_meta"], (b, 1))
    _, _, aff_t_meta = _outproj(xm, nam, df_meta.reshape(b * N_META, DIFF_WIDTH), shared["w_out"],
                                shared["norm_ffn"], *wr, tm=b * N_META, token_major=False)

    r = jnp.arange(rows, dtype=I32)
    gidx = (r + (r // t + 1) * N_META)[None, :]
    rm = jnp.arange(b * N_META, dtype=I32)
    gidx_meta = ((rm // N_META) * l_total + rm % N_META)[None, :]
    tau, cut, offs, cnts, rank_t = _select(aff_t, aff_t_meta, gidx, gidx_meta, shared["tri_u"],
                                           cap=cap, n_total=n_total)

    nb = rows // TOK_BLK
    offs_flat = offs[:, :nb].reshape(-1)
    cnts_flat = cnts[:, :nb].reshape(-1)
    cap_pad = -(-(cap + ROW_WIN) // 256) * 256
    ffn_tm = max(tm for tm in range(16, 1025, 16) if cap_pad % tm == 0)
    xe = _gather(offs_flat, cnts_flat, rank_t.reshape(N_EXPERTS, 1, rows), h2, cap_pad=cap_pad)
    ye = _ffn(xe, shared["w_gate"], shared["w_up"], shared["w_down"], tm=ffn_tm)

    pad = jnp.zeros((8, LANES - N_EXPERTS), I32)
    tau_row = jnp.concatenate([jnp.broadcast_to(tau[:, 0][None, :], (8, N_EXPERTS)), pad], axis=1)
    cut_row = jnp.concatenate([jnp.broadcast_to(cut[:, 0][None, :], (8, N_EXPERTS)), pad], axis=1)
    y = _combine(offs_flat, cnts_flat, aff, tau_row, cut_row, x2, shared["tri_l"], shared["final_norm"], ye,
                 t=t, l_total=l_total)
    return y.reshape(b, t, d)


def kernel(x_prompt, x_sample, meta_tokens, norm_mix, w_in, na_rpb, na_meta_bias, lambda_q1, lambda_k1,
           lambda_q2, lambda_k2, diff_subln, w_out, norm_ffn, w_router, w_gate, w_up, w_down, final_norm):
    d = x_prompt.shape[-1]
    wr = jnp.zeros((LANES, d), F32).at[:N_EXPERTS].set(w_router[0].T)
    wr_hi = wr.astype(BF16)
    mb = na_meta_bias[0].astype(F32)
    idx = jnp.arange(TOK_BLK)
    shared = {
        "meta_tokens": meta_tokens,
        "norm_mix": norm_mix[0][None, :],
        "w_in": _permute_in_weights(w_in[0]).astype(BF16),
        "lambda": (lambda_q1, lambda_k1, lambda_q2, lambda_k2),
        "subln": diff_subln,
        "w_out": w_out[0].astype(BF16),
        "norm_ffn": norm_ffn[0][None, :],
        "wr_hi": wr_hi,
        "wr_lo": (wr - wr_hi.astype(F32)).astype(BF16),
        "w_gate": w_gate[0].astype(BF16),
        "w_up": w_up[0].astype(BF16),
        "w_down": w_down[0].astype(BF16),
        "final_norm": final_norm[None, :],
        "mb_stack": jnp.repeat(mb, GRID_W, axis=0).reshape(NA_HEADS // 2, 2 * GRID_W, N_META),
        "tri_u": (idx[:, None] <= idx[None, :]).astype(BF16),
        "tri_l": (idx[:, None] >= idx[None, :]).astype(BF16),
    }
    cos_m, sin_m = _rope_tables(jnp.arange(N_META))
    shared["proj_meta"] = _inproj(meta_tokens, shared["norm_mix"], shared["w_in"], cos_m, sin_m, tm=N_META)
    mb_meta = jnp.repeat(mb, N_META, axis=0).reshape(NA_HEADS // 2, 2 * N_META, N_META)
    shared["na_meta"] = _na_meta(shared["proj_meta"], mb_meta)
    shared["na_table"] = _na_bias(na_rpb[0].astype(F32))
    return (_encode_group(x_prompt, shared), _encode_group(x_sample, shared))
```

```python
import functools
import math

import jax
import jax.numpy as jnp
from jax import lax
from jax.experimental import pallas as pl
from jax.experimental.pallas import tpu as pltpu

BF16 = jnp.bfloat16
F32 = jnp.float32
I32 = jnp.int32

N_META = 16
GRID_W = 64
HEAD_DIM = 64
NA_HEADS = 8
NA_WIDTH = NA_HEADS * HEAD_DIM
NA_KH = 8
NA_KW = 16
DIFF_HEADS = 4
DIFF_WIDTH = DIFF_HEADS * 2 * HEAD_DIM
PROJ_WIDTH = 3 * NA_WIDTH + 3 * DIFF_WIDTH
ROPE_THETA = 10000.0
N_EXPERTS = 16
EC_CAPACITY_FACTOR = 2
RMS_EPS = 1e-6
SUBLN_EPS = 1e-5
LAMBDA_INIT = 0.8 - 0.6 * math.exp(-0.3 * 0)
LOG2E = math.log2(math.e)

LANES = 128
NEG_BIG = -1e30
VMEM_LIMIT = 56 * 1024 * 1024

NAQ_BLK, NAK_BLK, NAV_BLK = 0, 4, 8
DQ_BLK, DK_BLK, DV_BLK = 12, 16, 20

ROW_WIN = 256
GATHER_WIN = 192
GATHER_RING = 3
GATHER_EXPERTS = 2
TOK_BLK = 1024
COMBINE_EXPERTS = 8


def _cparams(sem, vmem=VMEM_LIMIT):
    return pltpu.CompilerParams(dimension_semantics=sem, vmem_limit_bytes=vmem)


def _dot(a, b):
    return jnp.dot(a, b, preferred_element_type=F32)


def _dot_nt(a, b):
    return lax.dot_general(a, b, (((1,), (1,)), ((), ())), preferred_element_type=F32)


def _inproj_body(x_ref, g_ref, w_ref, cos_ref, sin_ref, o_ref, *, tn):
    x = x_ref[...]
    ms = jnp.mean(x * x, axis=-1, keepdims=True)
    h = (x * lax.rsqrt(ms + RMS_EPS) * g_ref[...]).astype(BF16)
    cos = cos_ref[...]
    sin = sin_ref[...]
    for c in range(PROJ_WIDTH // tn):
        lo = c * tn
        acc = _dot(h, w_ref[:, lo:lo + tn])
        for s in range(tn // LANES):
            blk = (lo + s * LANES) // LANES
            a = acc[:, s * LANES:(s + 1) * LANES]
            if DQ_BLK <= blk < DV_BLK:
                a = a * cos + pltpu.roll(a, 64, 1) * sin
            if blk < NAK_BLK:
                a = a * (HEAD_DIM ** -0.5)
            if DQ_BLK <= blk < DK_BLK:
                a = a * (HEAD_DIM ** -0.5 * LOG2E)
            o_ref[:, lo + s * LANES:lo + (s + 1) * LANES] = a.astype(BF16)


def _inproj(x, g, w, cos, sin, *, tm):
    rows, d = x.shape
    nt = cos.shape[0] // tm
    return pl.pallas_call(
        functools.partial(_inproj_body, tn=512),
        grid=(rows // tm,),
        in_specs=[
            pl.BlockSpec((tm, d), lambda i: (i, 0)),
            pl.BlockSpec((1, d), lambda i: (0, 0)),
            pl.BlockSpec((d, PROJ_WIDTH), lambda i: (0, 0)),
            pl.BlockSpec((tm, LANES), lambda i: (i % nt, 0)),
            pl.BlockSpec((tm, LANES), lambda i: (i % nt, 0)),
        ],
        out_specs=pl.BlockSpec((tm, PROJ_WIDTH), lambda i: (i, 0)),
        out_shape=jax.ShapeDtypeStruct((rows, PROJ_WIDTH), BF16),
        compiler_params=_cparams(("parallel",)),
        name="inproj",
    )(x, g, w, cos, sin)


def _na_bias_body(rpb_ref, o_ref):
    h = pl.program_id(0)
    c = lax.broadcasted_iota(I32, (GRID_W, GRID_W), 0)
    kc = lax.broadcasted_iota(I32, (GRID_W, GRID_W), 1)
    cs = jnp.clip(c - NA_KW // 2, 0, GRID_W - NA_KW)
    valid = (kc >= cs) & (kc < cs + NA_KW)
    jm = kc - c + NA_KW - 1
    n_dr = 2 * NA_KH - 1
    n_j = 2 * NA_KW - 1
    tiles = []
    for dr in range(n_dr):
        base = (h * n_dr + dr) * n_j

        def body(j, acc, base=base):
            return jnp.where(jm == j, rpb_ref[base + j], acc)

        t = lax.fori_loop(0, n_j, body, jnp.zeros((GRID_W, GRID_W), F32))
        tiles.append(jnp.where(valid, t, NEG_BIG))
    for d in range(NA_KH):
        o_ref[0, d] = jnp.concatenate([tiles[kr - d + NA_KH - 1] for kr in range(NA_KH)], axis=1)


def _na_bias(rpb):
    h = rpb.shape[0]
    return pl.pallas_call(
        _na_bias_body,
        grid_spec=pltpu.PrefetchScalarGridSpec(
            num_scalar_prefetch=1,
            grid=(h,),
            in_specs=[],
            out_specs=pl.BlockSpec((1, NA_KH, GRID_W, NA_KH * GRID_W), lambda i, r: (i, 0, 0, 0)),
        ),
        out_shape=jax.ShapeDtypeStruct((h, NA_KH, GRID_W, NA_KH * GRID_W), F32),
        compiler_params=_cparams(("arbitrary",)),
        name="na_bias",
    )(rpb.reshape(-1))


def _split_heads(q):
    lane = lax.broadcasted_iota(I32, q.shape, 1)
    zero = jnp.zeros_like(q)
    return jnp.concatenate([jnp.where(lane < HEAD_DIM, q, zero), jnp.where(lane < HEAD_DIM, zero, q)], axis=0)


def _merge_heads(o):
    n = o.shape[0] // 2
    lane = lax.broadcasted_iota(I32, (n, LANES), 1)
    return jnp.where(lane < HEAD_DIM, o[:n], o[n:])


def _na_body(q_ref, kp_ref, kc_ref, kn_ref, vp_ref, vc_ref, vn_ref, km_ref, vm_ref, tab_ref, mb_ref,
             o_ref, kbuf, vbuf, qq_scr, s_scr, p_scr, *, rows):
    rb = pl.program_id(2)
    blk = NA_KH * GRID_W
    kbuf[0:blk] = kp_ref[0]
    kbuf[blk:2 * blk] = kc_ref[0]
    kbuf[2 * blk:3 * blk] = kn_ref[0]
    vbuf[0:blk] = vp_ref[0]
    vbuf[blk:2 * blk] = vc_ref[0]
    vbuf[2 * blk:3 * blk] = vn_ref[0]
    two = 2 * GRID_W
    starts = []
    for i in range(NA_KH):
        r = rb * NA_KH + i
        rs = jnp.clip(r - NA_KH // 2, 0, rows - NA_KH)
        d = r - rs
        start = pl.multiple_of((rs - (rb - 1) * NA_KH) * GRID_W, GRID_W)
        starts.append(start)
        qq = _split_heads(q_ref[0, i * GRID_W:(i + 1) * GRID_W, :])
        qq_scr[i * two:(i + 1) * two, :] = qq
        bias = jnp.concatenate([tab_ref[0, d], tab_ref[1, d]], axis=0)
        s_scr[i * two:(i + 1) * two, :] = _dot_nt(qq, kbuf[pl.ds(start, blk), :]) + bias
    s = s_scr[...]
    sm = (_dot_nt(qq_scr[...], km_ref[...]).reshape(NA_KH, two, N_META) + mb_ref[0][None]).reshape(NA_KH * two, N_META)
    m = jnp.maximum(jnp.max(s, axis=1, keepdims=True), jnp.max(sm, axis=1, keepdims=True))
    p = jnp.exp(s - m)
    pm = jnp.exp(sm - m)
    l = jnp.sum(p, axis=1, keepdims=True) + jnp.sum(pm, axis=1, keepdims=True)
    p_scr[...] = p.astype(BF16)
    om = _dot(pm.astype(BF16), vm_ref[...])
    inv_l = 1.0 / l
    for i in range(NA_KH):
        rsl = slice(i * two, (i + 1) * two)
        o = (_dot(p_scr[rsl, :], vbuf[pl.ds(starts[i], blk), :]) + om[rsl]) * inv_l[rsl]
        o_ref[0, i * GRID_W:(i + 1) * GRID_W, :] = _merge_heads(o).astype(BF16)


def _na_attention(proj, proj_meta, table, mb_stack):
    b, t, _ = proj.shape
    rows = t // GRID_W
    nrb = rows // NA_KH
    blk = NA_KH * GRID_W
    hp = NA_HEADS // 2

    def kv_spec(col, shift):
        return pl.BlockSpec((1, blk, LANES),
                            lambda h, bi, rb: (bi, jnp.clip(rb + shift, 0, nrb - 1), col + h))

    return pl.pallas_call(
        functools.partial(_na_body, rows=rows),
        grid=(hp, b, nrb),
        in_specs=[
            pl.BlockSpec((1, blk, LANES), lambda h, bi, rb: (bi, rb, NAQ_BLK + h)),
            kv_spec(NAK_BLK, -1), kv_spec(NAK_BLK, 0), kv_spec(NAK_BLK, 1),
            kv_spec(NAV_BLK, -1), kv_spec(NAV_BLK, 0), kv_spec(NAV_BLK, 1),
            pl.BlockSpec((N_META, LANES), lambda h, bi, rb: (0, NAK_BLK + h)),
            pl.BlockSpec((N_META, LANES), lambda h, bi, rb: (0, NAV_BLK + h)),
            pl.BlockSpec((2, NA_KH, GRID_W, blk), lambda h, bi, rb: (h, 0, 0, 0)),
            pl.BlockSpec((1, LANES, N_META), lambda h, bi, rb: (h, 0, 0)),
        ],
        out_specs=pl.BlockSpec((1, blk, LANES), lambda h, bi, rb: (bi, rb, h)),
        out_shape=jax.ShapeDtypeStruct((b, t, NA_WIDTH), BF16),
        scratch_shapes=[pltpu.VMEM((3 * blk, LANES), BF16), pltpu.VMEM((3 * blk, LANES), BF16),
                        pltpu.VMEM((2 * blk, LANES), BF16), pltpu.VMEM((2 * blk, blk), F32),
                        pltpu.VMEM((2 * blk, blk), BF16)],
        compiler_params=_cparams(("arbitrary", "arbitrary", "arbitrary")),
        name="na_attention",
    )(proj, proj, proj, proj, proj, proj, proj, proj_meta, proj_meta, table, mb_stack)


def _na_meta_body(p_ref, mb_ref, o_ref):
    for h in range(NA_HEADS // 2):
        qq = _split_heads(p_ref[:, (NAQ_BLK + h) * LANES:(NAQ_BLK + h + 1) * LANES])
        km = p_ref[:, (NAK_BLK + h) * LANES:(NAK_BLK + h + 1) * LANES]
        vm = p_ref[:, (NAV_BLK + h) * LANES:(NAV_BLK + h + 1) * LANES]
        s = _dot_nt(qq, km) + mb_ref[h, 0:2 * N_META, :]
        m = jnp.max(s, axis=1, keepdims=True)
        p = jnp.exp(s - m)
        l = jnp.sum(p, axis=1, keepdims=True)
        o = _dot(p.astype(BF16), vm) / l
        o_ref[:, h * LANES:(h + 1) * LANES] = _merge_heads(o).astype(BF16)


def _na_meta(proj_meta, mb_meta):
    return pl.pallas_call(
        _na_meta_body,
        out_shape=jax.ShapeDtypeStruct((N_META, NA_WIDTH), BF16),
        name="na_meta",
    )(proj_meta, mb_meta)


def _split_maps(q):
    lane = lax.broadcasted_iota(I32, q.shape, 1)
    is0 = ((lane >> 5) & 1) == 0
    zero = jnp.zeros_like(q)
    return jnp.concatenate([jnp.where(is0, q, zero), jnp.where(is0, zero, q)], axis=0)


def _online_softmax(s_ref, p_ref, a_ref, m_scr, rows):
    s = s_ref[rows, :]
    m_prev = m_scr[rows]
    m_new = jnp.maximum(m_prev, jnp.max(s, axis=1, keepdims=True))
    a_ref[rows] = jnp.exp2(m_prev - m_new)
    m_scr[rows] = m_new
    p_ref[rows, :] = jnp.exp2((s - m_new).astype(BF16))


def _diff_body(q_ref, k_ref, v_ref, km_ref, vm_ref, lq1_ref, lk1_ref, lq2_ref, lk2_ref, sub_ref,
               o_ref, qq_scr, vx_scr, s0, s1, p0, p1, a0, a1, m_scr, acc_scr, *, tq, tk, rc, nk):
    def kv_rows(ref, ki):
        return ref[0, pl.ds(pl.multiple_of(ki * tk, tk), tk), :]

    qq = _split_maps(q_ref[0])
    qq_scr[...] = qq
    s = _dot_nt(qq, km_ref[...])
    m = jnp.max(s, axis=1, keepdims=True)
    m_scr[...] = m
    vx_scr[:, LANES:] = jnp.ones((vx_scr.shape[0], LANES), BF16)
    vx_scr[0:N_META, 0:LANES] = vm_ref[...]
    acc_scr[...] = _dot(jnp.exp2((s - m).astype(BF16)), vx_scr[0:N_META, :])
    s0[...] = _dot_nt(qq, k_ref[0, 0:tk, :])
    p1[...] = jnp.zeros_like(p1)
    a1[...] = jnp.ones_like(a1)

    def step(ki, s_cur, s_nxt, p_cur, p_prv, a_cur, a_prv, *, has_next):
        if has_next:
            kn = kv_rows(k_ref, ki + 1)
        vx_scr[:, 0:LANES] = kv_rows(v_ref, jnp.maximum(ki - 1, 0))
        for g in range(2 * tq // rc):
            rows = slice(g * rc, (g + 1) * rc)
            if has_next:
                s_nxt[rows, :] = _dot_nt(qq_scr[rows, :], kn)
            acc_scr[rows, :] = a_prv[rows] * acc_scr[rows, :] + _dot(p_prv[rows, :], vx_scr[...])
            _online_softmax(s_cur, p_cur, a_cur, m_scr, rows)

    even = (s0, s1, p0, p1, a0, a1)
    odd = (s1, s0, p1, p0, a1, a0)

    def block(ki, carry):
        @pl.when(ki % 2 == 0)
        def _():
            step(ki, *even, has_next=True)

        @pl.when(ki % 2 == 1)
        def _():
            step(ki, *odd, has_next=True)

        return carry

    lax.fori_loop(0, nk - 1, block, 0)

    @pl.when(pl.program_id(2) >= 0)
    def _():
        step(nk - 1, *(even if (nk - 1) % 2 == 0 else odd), has_next=False)

    p_last, a_last = (p0, a0) if (nk - 1) % 2 == 0 else (p1, a1)
    vx_scr[:, 0:LANES] = v_ref[0, (nk - 1) * tk:nk * tk, :]
    acc = a_last[...] * acc_scr[...] + _dot(p_last[...], vx_scr[...])
    lam = (jnp.exp(jnp.sum(lq1_ref[...] * lk1_ref[...], axis=1, keepdims=True))
           - jnp.exp(jnp.sum(lq2_ref[...] * lk2_ref[...], axis=1, keepdims=True)) + LAMBDA_INIT)
    o_all = acc[:, 0:LANES] / acc[:, LANES:]
    o = o_all[:tq] - lam * o_all[tq:]
    ms = jnp.mean(o * o, axis=-1, keepdims=True)
    o = o * lax.rsqrt(ms + SUBLN_EPS) * sub_ref[...] * (1.0 - LAMBDA_INIT)
    o_ref[0] = o.astype(BF16)


def _diff_attention(q_src, proj, proj_meta, lq1, lk1, lq2, lk2, subln, *, tq, tk, q_shared):
    b, t, _ = proj.shape
    tq_total = q_src.shape[1]
    nq = tq_total // tq
    nk = t // tk
    if q_shared:
        q_map = lambda bi, h, qi: (0, qi, DQ_BLK + h)
    else:
        q_map = lambda bi, h, qi: (bi, qi, DQ_BLK + h)
    vec = lambda n: pl.BlockSpec((1, n), lambda bi, h, qi: (0, 0))
    return pl.pallas_call(
        functools.partial(_diff_body, tq=tq, tk=tk, rc=min(512, 2 * tq), nk=nk),
        grid=(b, DIFF_HEADS, nq),
        in_specs=[
            pl.BlockSpec((1, tq, LANES), q_map),
            pl.BlockSpec((1, t, LANES), lambda bi, h, qi: (bi, 0, DK_BLK + h)),
            pl.BlockSpec((1, t, LANES), lambda bi, h, qi: (bi, 0, DV_BLK + h)),
            pl.BlockSpec((N_META, LANES), lambda bi, h, qi: (0, DK_BLK + h)),
            pl.BlockSpec((N_META, LANES), lambda bi, h, qi: (0, DV_BLK + h)),
            vec(HEAD_DIM), vec(HEAD_DIM), vec(HEAD_DIM), vec(HEAD_DIM), vec(2 * HEAD_DIM),
        ],
        out_specs=pl.BlockSpec((1, tq, LANES), lambda bi, h, qi: (bi, qi, h)),
        out_shape=jax.ShapeDtypeStruct((b, tq_total, DIFF_WIDTH), BF16),
        scratch_shapes=[
            pltpu.VMEM((2 * tq, LANES), BF16),
            pltpu.VMEM((tk, 2 * LANES), BF16),
            pltpu.VMEM((2 * tq, tk), F32), pltpu.VMEM((2 * tq, tk), F32),
            pltpu.VMEM((2 * tq, tk), BF16), pltpu.VMEM((2 * tq, tk), BF16),
            pltpu.VMEM((2 * tq, 1), F32), pltpu.VMEM((2 * tq, 1), F32),
            pltpu.VMEM((2 * tq, 1), F32),
            pltpu.VMEM((2 * tq, 2 * LANES), F32),
        ],
        compiler_params=_cparams(("parallel", "parallel", "parallel")),
        name="diff_attention",
    )(q_src, proj, proj, proj_meta, proj_meta, lq1, lk1, lq2, lk2, subln)


def _outproj_body(x_ref, na_ref, df_ref, wo_ref, g_ref, wrh_ref, wrl_ref, x2_ref, h2_ref, afft_ref, *aff_ref):
    x2 = x_ref[...] + _dot(na_ref[...], wo_ref[0:NA_WIDTH, :]) + _dot(df_ref[...], wo_ref[NA_WIDTH:, :])
    x2_ref[...] = x2
    ms = jnp.mean(x2 * x2, axis=-1, keepdims=True)
    hf = x2 * lax.rsqrt(ms + RMS_EPS) * g_ref[...]
    hi = hf.astype(BF16)
    h2_ref[...] = hi
    lo = (hf - hi.astype(F32)).astype(BF16)
    logits = _dot_nt(wrh_ref[...], hi) + _dot_nt(wrh_ref[...], lo) + _dot_nt(wrl_ref[...], hi)
    row = lax.broadcasted_iota(I32, logits.shape, 0)
    logits = jnp.where(row < N_EXPERTS, logits, NEG_BIG)
    m = jnp.max(logits, axis=0, keepdims=True)
    e = jnp.exp(logits - m)
    aff = e / jnp.sum(e, axis=0, keepdims=True)
    afft_ref[...] = aff[0:N_EXPERTS]
    if aff_ref:
        aff_ref[0][...] = aff.T


def _outproj(x, na, df, wo, g, wrh, wrl, *, tm, token_major):
    rows, d = x.shape
    out_shape = [jax.ShapeDtypeStruct((rows, d), F32), jax.ShapeDtypeStruct((rows, d), BF16),
                 jax.ShapeDtypeStruct((N_EXPERTS, rows), F32)]
    out_specs = [pl.BlockSpec((tm, d), lambda i: (i, 0)), pl.BlockSpec((tm, d), lambda i: (i, 0)),
                 pl.BlockSpec((N_EXPERTS, tm), lambda i: (0, i))]
    if token_major:
        out_shape.append(jax.ShapeDtypeStruct((rows, LANES), F32))
        out_specs.append(pl.BlockSpec((tm, LANES), lambda i: (i, 0)))
    return pl.pallas_call(
        _outproj_body,
        grid=(rows // tm,),
        in_specs=[
            pl.BlockSpec((tm, d), lambda i: (i, 0)),
            pl.BlockSpec((tm, NA_WIDTH), lambda i: (i, 0)),
            pl.BlockSpec((tm, DIFF_WIDTH), lambda i: (i, 0)),
            pl.BlockSpec((NA_WIDTH + DIFF_WIDTH, d), lambda i: (0, 0)),
            pl.BlockSpec((1, d), lambda i: (0, 0)),
            pl.BlockSpec((LANES, d), lambda i: (0, 0)),
            pl.BlockSpec((LANES, d), lambda i: (0, 0)),
        ],
        out_specs=out_specs,
        out_shape=out_shape,
        compiler_params=_cparams(("parallel",)),
        name="outproj_router",
    )(x, na, df, wo, g, wrh, wrl)


def _select_body(ar_ref, am_ref, gr_ref, gm_ref, tri_ref, tau_ref, cut_ref, offs_ref, cnt_ref, rank_ref, sel_scr,
                 *, cap, nb, idx_bits):
    br = pltpu.bitcast(ar_ref[...], I32)
    bm = pltpu.bitcast(am_ref[...], I32)
    gr = gr_ref[...]
    gm = gm_ref[...]

    def count(mr, mm):
        return (jnp.sum(jnp.where(mr, 1.0, 0.0), axis=1, keepdims=True)
                + jnp.sum(jnp.where(mm, 1.0, 0.0), axis=1, keepdims=True))

    def value_bit(i, ans):
        cand = ans | jnp.left_shift(jnp.int32(1), 30 - i)
        return jnp.where(count(br >= cand, bm >= cand) >= cap, cand, ans)

    tau = lax.fori_loop(0, 31, value_bit, jnp.zeros((N_EXPERTS, 1), I32))
    need = cap - count(br > tau, bm > tau)
    eq_r = br == tau
    eq_m = bm == tau

    def index_bit(i, ans):
        cand = ans | jnp.left_shift(jnp.int32(1), idx_bits - 1 - i)
        return jnp.where(count(eq_r & (gr < cand), eq_m & (gm < cand)) < need, cand, ans)

    cut = lax.fori_loop(0, idx_bits, index_bit, jnp.zeros((N_EXPERTS, 1), I32))
    tau_ref[...] = jnp.broadcast_to(tau, tau_ref.shape)
    cut_ref[...] = jnp.broadcast_to(cut, cut_ref.shape)

    sel_scr[...] = jnp.where((br > tau) | (eq_r & (gr <= cut)), 1.0, 0.0)
    lane = lax.broadcasted_iota(I32, (N_EXPERTS, LANES), 1)

    def block_count(j, acc):
        start = pl.multiple_of(j * TOK_BLK, TOK_BLK)
        sel = sel_scr[:, pl.ds(start, TOK_BLK)]
        rank_ref[:, pl.ds(start, TOK_BLK)] = sel * _dot(sel.astype(BF16), tri_ref[...])
        c = jnp.sum(sel, axis=1, keepdims=True)
        return jnp.where(lane == j, c, acc)

    counts = lax.fori_loop(0, nb, block_count, jnp.zeros((N_EXPERTS, LANES), F32))
    incl = counts
    shift = 1
    while shift < LANES:
        incl = incl + jnp.where(lane >= shift, pltpu.roll(incl, shift, 1), 0.0)
        shift *= 2
    offs_ref[...] = (incl - counts).astype(I32)
    cnt_ref[...] = counts.astype(I32)


def _select(aff_t, aff_t_meta, gidx, gidx_meta, tri_u, *, cap, n_total):
    e, nr = aff_t.shape
    nb = nr // TOK_BLK
    assert nb <= LANES and nr % TOK_BLK == 0
    idx_bits = max(1, (n_total - 1).bit_length())
    shp = jax.ShapeDtypeStruct((e, LANES), I32)
    return pl.pallas_call(
        functools.partial(_select_body, cap=cap, nb=nb, idx_bits=idx_bits),
        out_shape=[shp, shp, shp, shp, jax.ShapeDtypeStruct((e, nr), F32)],
        scratch_shapes=[pltpu.VMEM((e, nr), F32)],
        compiler_params=pltpu.CompilerParams(vmem_limit_bytes=VMEM_LIMIT),
        name="ec_select",
    )(aff_t, aff_t_meta, gidx, gidx_meta, tri_u)


def _gather_body(offs_ref, cnts_ref, rank_ref, h_hbm, xe_ref, hbuf, sems, *, nb, n_steps):
    eg = pl.program_id(0)

    def h_copy(s):
        slot = s % GATHER_RING
        rows = pl.ds(pl.multiple_of((s % nb) * TOK_BLK, TOK_BLK), TOK_BLK)
        return pltpu.make_async_copy(h_hbm.at[rows, :], hbuf.at[slot], sems.at[slot])

    @pl.when(eg == 0)
    def _():
        for s in range(GATHER_RING - 1):
            h_copy(s).start()

    xe_ref[...] = jnp.zeros_like(xe_ref)
    rowid = lax.broadcasted_iota(I32, (GATHER_WIN, TOK_BLK), 0)

    def block(j, carry):
        step = eg * nb + j

        @pl.when(step + GATHER_RING - 1 < n_steps)
        def _():
            h_copy(step + GATHER_RING - 1).start()

        h_copy(step).wait()
        slot = step % GATHER_RING
        for u in range(GATHER_EXPERTS):
            e = eg * GATHER_EXPERTS + u
            rank = rank_ref[u, :, pl.ds(pl.multiple_of(j * TOK_BLK, TOK_BLK), TOK_BLK)].astype(I32)
            off = offs_ref[e * nb + j]
            off_al = (off // 16) * 16
            rel = jnp.where(rank > 0, rank - 1 + (off - off_al), -1)
            nchunk = (off - off_al + cnts_ref[e * nb + j] + GATHER_WIN - 1) // GATHER_WIN

            def chunk(c, inner, u=u, rel=rel, off_al=off_al):
                onehot = jnp.where(rowid == rel - c * GATHER_WIN, 1.0, 0.0).astype(BF16)
                g = _dot(onehot, hbuf[slot]).astype(BF16)
                start = pl.multiple_of(off_al + c * GATHER_WIN, 16)
                xe_ref[u, pl.ds(start, GATHER_WIN), :] = xe_ref[u, pl.ds(start, GATHER_WIN), :] + g
                return inner

            lax.fori_loop(0, nchunk, chunk, 0)
        return carry

    lax.fori_loop(0, nb, block, 0)


def _gather(offs_flat, cnts_flat, rank3, h2, *, cap_pad):
    e = rank3.shape[0]
    nr, d = h2.shape
    nb = nr // TOK_BLK
    return pl.pallas_call(
        functools.partial(_gather_body, nb=nb, n_steps=e // GATHER_EXPERTS * nb),
        grid_spec=pltpu.PrefetchScalarGridSpec(
            num_scalar_prefetch=2,
            grid=(e // GATHER_EXPERTS,),
            in_specs=[
                pl.BlockSpec((GATHER_EXPERTS, 1, nr), lambda eg, o, c: (eg, 0, 0)),
                pl.BlockSpec(memory_space=pl.ANY),
            ],
            out_specs=pl.BlockSpec((GATHER_EXPERTS, cap_pad, d), lambda eg, o, c: (eg, 0, 0),
                                   pipeline_mode=pl.Buffered(1)),
            scratch_shapes=[pltpu.VMEM((GATHER_RING, TOK_BLK, d), BF16),
                            pltpu.SemaphoreType.DMA((GATHER_RING,))],
        ),
        out_shape=jax.ShapeDtypeStruct((e, cap_pad, d), BF16),
        compiler_params=_cparams(("arbitrary",)),
        name="ec_gather",
    )(offs_flat, cnts_flat, rank3, h2)


def _ffn_body(x_ref, wg_ref, wu_ref, wd_ref, o_ref, *, fc):
    x = x_ref[0]
    f = wg_ref.shape[2]
    acc = jnp.zeros((x.shape[0], wd_ref.shape[2]), F32)
    for c in range(f // fc):
        a = _dot(x, wg_ref[0, :, c * fc:(c + 1) * fc])
        b = _dot(x, wu_ref[0, :, c * fc:(c + 1) * fc])
        hmid = (a * jax.nn.sigmoid(a) * b).astype(BF16)
        acc = acc + _dot(hmid, wd_ref[0, c * fc:(c + 1) * fc, :])
    o_ref[0] = acc.astype(BF16)


def _ffn(xe, wg, wu, wd, *, tm):
    e, cap_pad, d = xe.shape
    f = wg.shape[2]
    return pl.pallas_call(
        functools.partial(_ffn_body, fc=min(512, f)),
        grid=(e, cap_pad // tm),
        in_specs=[
            pl.BlockSpec((1, tm, d), lambda ei, i: (ei, i, 0)),
            pl.BlockSpec((1, d, f), lambda ei, i: (ei, 0, 0)),
            pl.BlockSpec((1, d, f), lambda ei, i: (ei, 0, 0)),
            pl.BlockSpec((1, f, d), lambda ei, i: (ei, 0, 0)),
        ],
        out_specs=pl.BlockSpec((1, tm, d), lambda ei, i: (ei, i, 0)),
        out_shape=jax.ShapeDtypeStruct((e, cap_pad, d), BF16),
        compiler_params=_cparams(("parallel", "arbitrary")),
        name="ec_ffn",
    )(xe, wg, wu, wd)


def _combine_body(offs_ref, cnts_ref, aff_ref, tau_ref, cut_ref, x2_ref, tri_ref, fn_ref, *rest, nb, t, l_total):
    ywin_refs = rest[:COMBINE_EXPERTS]
    ye_hbm, o_ref, rank_scr, gate_scr, acc_scr, ybuf, sem = rest[COMBINE_EXPERTS:]
    j = pl.program_id(0)
    eg = pl.program_id(1)

    @pl.when(eg == 0)
    def _():
        aff = aff_ref[...]
        bits = pltpu.bitcast(aff, I32)
        tau = tau_ref[0:1, :]
        cut = cut_ref[0:1, :]
        row0 = j * TOK_BLK
        gidx = row0 + (row0 // t) * (l_total - t) + (l_total - t) + lax.broadcasted_iota(I32, bits.shape, 0)
        sel = (bits > tau) | ((bits == tau) & (gidx <= cut))
        sel = sel & (lax.broadcasted_iota(I32, bits.shape, 1) < N_EXPERTS)
        self32 = jnp.where(sel, 1.0, 0.0)
        incl = _dot(tri_ref[...], self32.astype(BF16))
        rank_scr[...] = jnp.where(sel, incl - 1.0, -1.0)
        gate_scr[...] = jnp.where(sel, aff, 0.0)
        acc_scr[...] = x2_ref[...]

    lane = lax.broadcasted_iota(I32, (TOK_BLK, LANES), 1)
    colid = lax.broadcasted_iota(I32, (TOK_BLK, ROW_WIN), 1)

    def placement(e):
        pick = lane == e
        rank = jnp.sum(jnp.where(pick, rank_scr[...], 0.0), axis=1, keepdims=True).astype(I32)
        gate = jnp.sum(jnp.where(pick, gate_scr[...], 0.0), axis=1, keepdims=True)
        off = offs_ref[e * nb + j]
        off_al = (off // 16) * 16
        rel = jnp.where(rank >= 0, rank + (off - off_al), -1)
        return rel, gate, off_al, (off - off_al + cnts_ref[e * nb + j] + ROW_WIN - 1) // ROW_WIN

    acc = acc_scr[...]
    for u in range(COMBINE_EXPERTS):
        rel, gate, _, _ = placement(eg * COMBINE_EXPERTS + u)
        onehot = jnp.where(colid == rel, 1.0, 0.0).astype(BF16)
        acc = acc + _dot(onehot, ywin_refs[u][...]) * gate
    acc_scr[...] = acc

    for u in range(COMBINE_EXPERTS):
        e = eg * COMBINE_EXPERTS + u
        nchunk = placement(e)[3]

        def chunk(c, carry, e=e):
            rel, gate, off_al, _ = placement(e)
            start = pl.multiple_of(off_al + c * ROW_WIN, 16)
            cp = pltpu.make_async_copy(ye_hbm.at[e, pl.ds(start, ROW_WIN), :], ybuf, sem)
            cp.start()
            cp.wait()
            oh = jnp.where(colid == rel - c * ROW_WIN, 1.0, 0.0).astype(BF16)
            acc_scr[...] += _dot(oh, ybuf[...]) * gate
            return carry

        lax.fori_loop(1, nchunk, chunk, 0)

    @pl.when(eg == pl.num_programs(1) - 1)
    def _():
        y = acc_scr[...]
        ms = jnp.mean(y * y, axis=-1, keepdims=True)
        o_ref[...] = y * lax.rsqrt(ms + RMS_EPS) * fn_ref[...]


def _combine(offs_flat, cnts_flat, aff, tau_row, cut_row, x2, tri_l, fnorm, ye, *, t, l_total):
    nr, d = x2.shape
    nb = nr // TOK_BLK
    e = ye.shape[0]

    def win_spec(u):
        def win_map(j, eg, o, c):
            ei = eg * COMBINE_EXPERTS + u
            return (ei, (o[ei * nb + j] // 16) * 16, 0)
        return pl.BlockSpec((pl.Squeezed(), pl.Element(ROW_WIN), pl.Element(d)), win_map)

    return pl.pallas_call(
        functools.partial(_combine_body, nb=nb, t=t, l_total=l_total),
        grid_spec=pltpu.PrefetchScalarGridSpec(
            num_scalar_prefetch=2,
            grid=(nb, e // COMBINE_EXPERTS),
            in_specs=[
                pl.BlockSpec((TOK_BLK, LANES), lambda j, ei, o, c: (j, 0)),
                pl.BlockSpec((8, LANES), lambda j, ei, o, c: (0, 0)),
                pl.BlockSpec((8, LANES), lambda j, ei, o, c: (0, 0)),
                pl.BlockSpec((TOK_BLK, d), lambda j, ei, o, c: (j, 0)),
                pl.BlockSpec((TOK_BLK, TOK_BLK), lambda j, ei, o, c: (0, 0)),
                pl.BlockSpec((1, d), lambda j, ei, o, c: (0, 0)),
                *[win_spec(u) for u in range(COMBINE_EXPERTS)],
                pl.BlockSpec(memory_space=pl.ANY),
            ],
            out_specs=pl.BlockSpec((TOK_BLK, d), lambda j, ei, o, c: (j, 0)),
            scratch_shapes=[
                pltpu.VMEM((TOK_BLK, LANES), F32),
                pltpu.VMEM((TOK_BLK, LANES), F32),
                pltpu.VMEM((TOK_BLK, d), F32),
                pltpu.VMEM((ROW_WIN, d), BF16),
                pltpu.SemaphoreType.DMA(()),
            ],
        ),
        out_shape=jax.ShapeDtypeStruct((nr, d), F32),
        compiler_params=_cparams(("parallel", "arbitrary")),
        name="ec_combine",
    )(offs_flat, cnts_flat, aff, tau_row, cut_row, x2, tri_l, fnorm, *([ye] * COMBINE_EXPERTS), ye)


def _rope_tables(positions):
    inv_freq = 1.0 / (ROPE_THETA ** (jnp.arange(0, HEAD_DIM, 2, dtype=F32) / HEAD_DIM))
    ang = positions.astype(F32)[:, None] * inv_freq[None, :]
    cos, sin = jnp.cos(ang), jnp.sin(ang)
    return (jnp.concatenate([cos, cos, cos, cos], axis=1),
            jnp.concatenate([-sin, -sin, sin, sin], axis=1))


def _permute_in_weights(w_in):
    l = jnp.arange(LANES)
    g = l // 32
    within = (g % 2) * HEAD_DIM + (g // 2) * 32 + l % 32
    cols = jnp.arange(PROJ_WIDTH)
    blk = cols // LANES
    permuted = blk * LANES + within[cols % LANES]
    cols = jnp.where((blk >= DQ_BLK) & (blk < DV_BLK), permuted, cols)
    return w_in[:, cols]


def _encode_group(x, shared):
    b, t, d = x.shape
    l_total = N_META + t
    n_total = b * l_total
    cap = EC_CAPACITY_FACTOR * n_total // N_EXPERTS
    rows = b * t
    xf = x.reshape(rows, d)

    cos, sin = _rope_tables(N_META + jnp.arange(t))
    proj = _inproj(xf, shared["norm_mix"], shared["w_in"], cos, sin, tm=512).reshape(b, t, PROJ_WIDTH)
    proj_meta = shared["proj_meta"]

    na = _na_attention(proj, proj_meta, shared["na_table"], shared["mb_stack"])
    lam = shared["lambda"]
    tk = min(1024, t)
    df = _diff_attention(proj, proj, proj_meta, *lam, shared["subln"], tq=min(1024, t), tk=tk, q_shared=False)
    df_meta = _diff_attention(proj_meta[None], proj, proj_meta, *lam, shared["subln"],
                              tq=N_META, tk=tk, q_shared=True)

    wr = (shared["wr_hi"], shared["wr_lo"])
    x2, h2, aff_t, aff = _outproj(xf, na.reshape(rows, NA_WIDTH), df.reshape(rows, DIFF_WIDTH), shared["w_out"],
                                  shared["norm_ffn"], *wr, tm=512, token_major=True)
    xm = jnp.tile(shared["meta_tokens"], (b, 1))
    nam = jnp.tile(shared["na_meta"], (b, 1))
    _, _, aff_t_meta = _outproj(xm, nam, df_meta.reshape(b * N_META, DIFF_WIDTH), shared["w_out"],
                                shared["norm_ffn"], *wr, tm=b * N_META, token_major=False)

    r = jnp.arange(rows, dtype=I32)
    gidx = (r + (r // t + 1) * N_META)[None, :]
    rm = jnp.arange(b * N_META, dtype=I32)
    gidx_meta = ((rm // N_META) * l_total + rm % N_META)[None, :]
    tau, cut, offs, cnts, rank_t = _select(aff_t, aff_t_meta, gidx, gidx_meta, shared["tri_u"],
                                           cap=cap, n_total=n_total)

    nb = rows // TOK_BLK
    offs_flat = offs[:, :nb].reshape(-1)
    cnts_flat = cnts[:, :nb].reshape(-1)
    cap_pad = -(-(cap + ROW_WIN) // 256) * 256
    ffn_tm = max(tm for tm in range(16, 1025, 16) if cap_pad % tm == 0)
    xe = _gather(offs_flat, cnts_flat, rank_t.reshape(N_EXPERTS, 1, rows), h2, cap_pad=cap_pad)
    ye = _ffn(xe, shared["w_gate"], shared["w_up"], shared["w_down"], tm=ffn_tm)

    pad = jnp.zeros((8, LANES - N_EXPERTS), I32)
    tau_row = jnp.concatenate([jnp.broadcast_to(tau[:, 0][None, :], (8, N_EXPERTS)), pad], axis=1)
    cut_row = jnp.concatenate([jnp.broadcast_to(cut[:, 0][None, :], (8, N_EXPERTS)), pad], axis=1)
    y = _combine(offs_flat, cnts_flat, aff, tau_row, cut_row, x2, shared["tri_l"], shared["final_norm"], ye,
                 t=t, l_total=l_total)
    return y.reshape(b, t, d)


def kernel(x_prompt, x_sample, meta_tokens, norm_mix, w_in, na_rpb, na_meta_bias, lambda_q1, lambda_k1,
           lambda_q2, lambda_k2, diff_subln, w_out, norm_ffn, w_router, w_gate, w_up, w_down, final_norm):
    d = x_prompt.shape[-1]
    wr = jnp.zeros((LANES, d), F32).at[:N_EXPERTS].set(w_router[0].T)
    wr_hi = wr.astype(BF16)
    mb = na_meta_bias[0].astype(F32)
    idx = jnp.arange(TOK_BLK)
    shared = {
        "meta_tokens": meta_tokens,
        "norm_mix": norm_mix[0][None, :],
        "w_in": _permute_in_weights(w_in[0]).astype(BF16),
        "lambda": (lambda_q1, lambda_k1, lambda_q2, lambda_k2),
        "subln": diff_subln,
        "w_out": w_out[0].astype(BF16),
        "norm_ffn": norm_ffn[0][None, :],
        "wr_hi": wr_hi,
        "wr_lo": (wr - wr_hi.astype(F32)).astype(BF16),
        "w_gate": w_gate[0].astype(BF16),
        "w_up": w_up[0].astype(BF16),
        "w_down": w_down[0].astype(BF16),
        "final_norm": final_norm[None, :],
        "mb_stack": jnp.repeat(mb, GRID_W, axis=0).reshape(NA_HEADS // 2, 2 * GRID_W, N_META),
        "tri_u": (idx[:, None] <= idx[None, :]).astype(BF16),
        "tri_l": (idx[:, None] >= idx[None, :]).astype(BF16),
    }
    cos_m, sin_m = _rope_tables(jnp.arange(N_META))
    shared["proj_meta"] = _inproj(meta_tokens, shared["norm_mix"], shared["w_in"], cos_m, sin_m, tm=N_META)
    mb_meta = jnp.repeat(mb, N_META, axis=0).reshape(NA_HEADS // 2, 2 * N_META, N_META)
    shared["na_meta"] = _na_meta(shared["proj_meta"], mb_meta)
    shared["na_table"] = _na_bias(na_rpb[0].astype(F32))
    return (_encode_group(x_prompt, shared), _encode_group(x_sample, shared))
```
